```python
import numpy as np
import jax
import jax.numpy as jnp
from jax import lax

D_MODEL = 1024
BATCH = 16
SEQ = 2048
DEPTH = 2

HEAD_DIM = 64
GLA_HEADS = 4
GLA_WIDTH = GLA_HEADS * HEAD_DIM
GLA_GATE_RANK = 16
GLA_TAU = 16.0
GLA_CHUNK = 64
RWKV_HEADS = 4
RWKV_WIDTH = RWKV_HEADS * HEAD_DIM
RWKV_DECAY_RANK = 64
RWKV_ICLR_RANK = 64
RWKV_VRES_RANK = 32
RWKV_GATE_RANK = 160
RWKV_GN_EPS = 64e-5
NSA_HEADS = 8
NSA_KV_HEADS = 2
NSA_WIDTH = NSA_HEADS * HEAD_DIM
NSA_KV_WIDTH = NSA_KV_HEADS * HEAD_DIM
NSA_CMP_BLOCK = 32
NSA_CMP_STRIDE = 16
NSA_CMP_HIDDEN = 256
NSA_SEL_BLOCK = 64
NSA_N_SELECT = 16
NSA_WINDOW = 512
NSA_SEL_QCHUNK = 32
NSA_WIN_QBLOCK = 128
ROPE_THETA = 10000.0
FFN_HIDDEN = ((8 * D_MODEL + 3 * 256 - 1) // (3 * 256)) * 256
DEEPNORM_ALPHA = (2 * DEPTH) ** 0.25
DEEPNORM_BETA = (8 * DEPTH) ** -0.25
LN_EPS = 1e-5
MASK_NEG = -1e30

GLA_COLS = (GLA_WIDTH, GLA_WIDTH, GLA_WIDTH, GLA_WIDTH, GLA_GATE_RANK)
RWKV_COLS = (RWKV_WIDTH, RWKV_WIDTH, RWKV_WIDTH,
             RWKV_DECAY_RANK, RWKV_ICLR_RANK, RWKV_GATE_RANK)
NSA_COLS = (NSA_WIDTH,) + (NSA_KV_WIDTH,) * 6 + (3 * NSA_HEADS,)
IN_COLS = sum(GLA_COLS) + sum(RWKV_COLS) + sum(NSA_COLS)

kernel_name = 'hybrid_gla_rwkv7_nsa_deepnorm'


def _split(h, widths):
    idx = np.cumsum(np.asarray(widths))[:-1].tolist()
    return jnp.split(h, idx, axis=-1)


def _layer_norm(x, w, b):
    xf = x.astype(jnp.float32)
    mu = xf.mean(-1, keepdims=True)
    var = jnp.square(xf - mu).mean(-1, keepdims=True)
    return ((xf - mu) * lax.rsqrt(var + LN_EPS) * w + b).astype(x.dtype)


def _head_norm(x, w, b, n_heads, eps):
    shp = x.shape
    xh = x.astype(jnp.float32).reshape(shp[:-1] + (n_heads, shp[-1] // n_heads))
    mu = xh.mean(-1, keepdims=True)
    var = jnp.square(xh - mu).mean(-1, keepdims=True)
    xh = (xh - mu) * lax.rsqrt(var + eps)
    return xh.reshape(shp) * w + b


def _rope(x, pos):
    half = x.shape[-1] // 2
    inv = ROPE_THETA ** (-jnp.arange(half, dtype=jnp.float32) / half)
    ang = pos.astype(jnp.float32)[:, None] * inv
    shape = (ang.shape[0],) + (1,) * (x.ndim - 3) + (half,)
    cos = jnp.cos(ang).reshape(shape).astype(x.dtype)
    sin = jnp.sin(ang).reshape(shape).astype(x.dtype)
    x1, x2 = x[..., :half], x[..., half:]
    return jnp.concatenate([x1 * cos - x2 * sin, x1 * sin + x2 * cos], axis=-1)


def _token_shift_mix(p, mu):
    prev = jnp.pad(p, ((0, 0), (1, 0), (0, 0)))[:, :-1]
    return p + (prev - p) * mu


def _gla_mixer(q, k, v, g, a_lr, w_a2, b_a, ln_w, ln_b):
    dt = q.dtype
    B, S, _ = q.shape
    H, dk, C = GLA_HEADS, HEAD_DIM, GLA_CHUNK
    n = S // C
    f32 = jnp.float32
    log_a = jax.nn.log_sigmoid((a_lr @ w_a2 + b_a).astype(f32)) / GLA_TAU

    def chunks(t):
        return t.astype(f32).reshape(B, n, C, H, dk).transpose(1, 0, 3, 2, 4)

    qc, kc, vc, ac = chunks(q * dk ** -0.5), chunks(k), chunks(v), chunks(log_a)
    causal = jnp.tril(jnp.ones((C, C), bool))[:, :, None]

    def step(state, inp):
        qi, ki, vi, ai = inp
        b = jnp.cumsum(ai, axis=2)
        inter = jnp.einsum('bhtk,bhkv->bhtv', qi * jnp.exp(b), state)
        diff = b[:, :, :, None, :] - b[:, :, None, :, :]
        decay = jnp.exp(jnp.where(causal, diff, -jnp.inf))
        scores = jnp.einsum('bhtk,bhsk,bhtsk->bhts', qi, ki, decay)
        intra = jnp.einsum('bhts,bhsv->bhtv', scores, vi)
        b_end = b[:, :, -1:, :]
        state = (jnp.exp(b_end[:, :, 0, :, None]) * state
                 + jnp.einsum('bhsk,bhsv->bhkv', ki * jnp.exp(b_end - b), vi))
        return state, inter + intra

    s0 = jnp.zeros((B, H, dk, dk), f32)
    _, o = lax.scan(step, s0, (qc, kc, vc, ac))
    o = o.transpose(1, 0, 3, 2, 4).reshape(B, S, H * dk)
    o = _head_norm(o, ln_w, ln_b, H, LN_EPS)
    return (o * jax.nn.silu(g.astype(f32))).astype(dt)


def _rwkv7_mixer(r, k, v, w_lr, a_lr, g_lr, w0, w2, a0, a2, g2, k_k, k_a, r_k, ln_w, ln_b):
    dt = r.dtype
    B, S, _ = r.shape
    H, N = RWKV_HEADS, HEAD_DIM
    f32 = jnp.float32
    r, k, v = r.astype(f32), k.astype(f32), v.astype(f32)
    w_log = -jax.nn.softplus(-(w0 + jnp.tanh(w_lr) @ w2).astype(f32)) - 0.5
    decay = jnp.exp(-jnp.exp(w_log))
    a = jax.nn.sigmoid((a0 + a_lr @ a2).astype(f32))
    g = (jax.nn.sigmoid(g_lr) @ g2).astype(f32)
    kk = (k * k_k).reshape(B, S, H, N)
    kk = kk / jnp.maximum(jnp.sqrt(jnp.sum(kk * kk, axis=-1, keepdims=True)), 1e-12)
    k = k * (1.0 + (a - 1.0) * k_a)

    def heads(t):
        return t.reshape(B, S, H, N).transpose(1, 0, 2, 3)

    a_h = a.reshape(B, S, H, N)
    xs = (heads(r), heads(decay), heads(k), heads(v),
          (-kk).transpose(1, 0, 2, 3), (kk * a_h).transpose(1, 0, 2, 3))

    def step(state, inp):
        r_t, w_t, k_t, v_t, a_t, b_t = inp
        sa = jnp.einsum('bhvk,bhk->bhv', state, a_t)
        state = (state * w_t[:, :, None, :] + sa[..., None] * b_t[:, :, None, :]
                 + v_t[..., None] * k_t[:, :, None, :])
        return state, jnp.einsum('bhvk,bhk->bhv', state, r_t)

    s0 = jnp.zeros((B, H, N, N), f32)
    _, y = lax.scan(step, s0, xs)
    y = y.transpose(1, 0, 2, 3).reshape(B, S, H * N)
    y = _head_norm(y, ln_w, ln_b, H, RWKV_GN_EPS)
    bonus = (jnp.sum((r * k).reshape(B, S, H, N) * r_k, axis=-1, keepdims=True)
             * v.reshape(B, S, H, N)).reshape(B, S, H * N)
    return ((y + bonus) * g).astype(dt)


def _compress(t, pos_emb, w1, w2):
    B, S, G, hd = t.shape
    n_sub = NSA_CMP_BLOCK // NSA_CMP_STRIDE
    ncb = S // NSA_CMP_STRIDE
    nc = ncb - n_sub + 1
    t_s = t.reshape(B, ncb, NSA_CMP_STRIDE, G, hd)
    blocks = jnp.concatenate([t_s[:, j:j + nc] for j in range(n_sub)], axis=2)
    blocks = blocks + pos_emb[None, None, :, None, :]
    flat = blocks.transpose(0, 1, 3, 2, 4).reshape(B, nc, G, NSA_CMP_BLOCK * hd)
    return jax.nn.gelu(flat @ w1) @ w2


def _overlap_matrix(seq):
    nc = seq // NSA_CMP_STRIDE - NSA_CMP_BLOCK // NSA_CMP_STRIDE + 1
    ns = seq // NSA_SEL_BLOCK
    c0 = np.arange(nc) * NSA_CMP_STRIDE
    s0 = np.arange(ns) * NSA_SEL_BLOCK
    lo = np.maximum(c0[:, None], s0[None, :])
    hi = np.minimum(c0[:, None] + NSA_CMP_BLOCK, s0[None, :] + NSA_SEL_BLOCK)
    return (np.maximum(hi - lo, 0) / NSA_CMP_STRIDE).astype(np.float32)


def _selected_attention(q, k, v, idx, pos):
    B, S, G, HPG, hd = q.shape
    n = idx.shape[-1]
    ns = S // NSA_SEL_BLOCK
    kb = k.reshape(B, ns, NSA_SEL_BLOCK, G, hd).transpose(0, 3, 1, 2, 4)
    vb = v.reshape(B, ns, NSA_SEL_BLOCK, G, hd).transpose(0, 3, 1, 2, 4)
    qcs = NSA_SEL_QCHUNK
    nq = S // qcs
    q_ch = q.reshape(B, nq, qcs, G, HPG, hd).transpose(1, 0, 2, 3, 4, 5)
    idx_ch = idx.reshape(B, G, nq, qcs, n).transpose(2, 0, 1, 3, 4)
    pos_ch = pos.reshape(nq, qcs)
    bi = jnp.arange(B)[:, None, None, None]
    gi = jnp.arange(G)[None, :, None, None]
    offs = jnp.arange(NSA_SEL_BLOCK)

    def one(args):
        qc, ic, tc = args
        kg = kb[bi, gi, ic]
        vg = vb[bi, gi, ic]
        s = jnp.einsum('bqghd,bgqnkd->bghqnk', qc, kg).astype(jnp.float32)
        kpos = ic[..., None] * NSA_SEL_BLOCK + offs
        valid = kpos <= tc[None, None, :, None, None]
        s = jnp.where(valid[:, :, None], s, MASK_NEG).reshape(B, G, HPG, qcs, n * NSA_SEL_BLOCK)
        pr = jax.nn.softmax(s, axis=-1).reshape(B, G, HPG, qcs, n, NSA_SEL_BLOCK)
        return jnp.einsum('bghqnk,bgqnkd->bqghd', pr.astype(vg.dtype), vg)

    o = lax.map(one, (q_ch, idx_ch, pos_ch))
    return o.transpose(1, 0, 2, 3, 4, 5).reshape(B, S, G, HPG, hd)


def _window_attention(q, k, v):
    B, S, G, HPG, hd = q.shape
    qb_sz, win = NSA_WIN_QBLOCK, NSA_WINDOW
    nb = S // qb_sz
    kp = jnp.pad(k, ((0, 0), (win, 0), (0, 0), (0, 0)))
    vp = jnp.pad(v, ((0, 0), (win, 0), (0, 0), (0, 0)))
    q_bl = q.reshape(B, nb, qb_sz, G, HPG, hd).transpose(1, 0, 2, 3, 4, 5)
    rel = jnp.arange(qb_sz)[:, None] + win - jnp.arange(qb_sz + win)[None, :]

    def one(args):
        qb, i = args
        start = i * qb_sz
        kb = lax.dynamic_slice_in_dim(kp, start, qb_sz + win, axis=1)
        vb = lax.dynamic_slice_in_dim(vp, start, qb_sz + win, axis=1)
        s = jnp.einsum('bqghd,bkgd->bghqk', qb, kb).astype(jnp.float32)
        kpos = start - win + jnp.arange(qb_sz + win)
        valid = (rel >= 0) & (rel < win) & (kpos[None, :] >= 0)
        pr = jax.nn.softmax(jnp.where(valid, s, MASK_NEG), axis=-1)
        return jnp.einsum('bghqk,bkgd->bqghd', pr.astype(vb.dtype), vb)

    o = lax.map(one, (q_bl, jnp.arange(nb)))
    return o.transpose(1, 0, 2, 3, 4, 5).reshape(B, S, G, HPG, hd)


def _nsa_mixer(h, pos, pos_k, pos_v, wk1, wk2, wv1, wv2):
    dt = h.dtype
    B, S, _ = h.shape
    G, HPG, hd = NSA_KV_HEADS, NSA_HEADS // NSA_KV_HEADS, HEAD_DIM
    q, kc, vc, ks, vs, kw, vw, gate = _split(h, NSA_COLS)
    q = _rope(q.reshape(B, S, G, HPG, hd), pos) * hd ** -0.5
    kc, ks, kw = (_rope(t.reshape(B, S, G, hd), pos) for t in (kc, ks, kw))
    vc, vs, vw = (t.reshape(B, S, G, hd) for t in (vc, vs, vw))

    k_cmp = _compress(kc, pos_k, wk1, wk2)
    v_cmp = _compress(vc, pos_v, wv1, wv2)
    nc = k_cmp.shape[1]
    s_cmp = jnp.einsum('bsghd,bcgd->bghsc', q, k_cmp).astype(jnp.float32)
    block_end = jnp.arange(nc) * NSA_CMP_STRIDE + NSA_CMP_BLOCK - 1
    valid_c = block_end[None, :] <= pos[:, None]
    p_cmp = jax.nn.softmax(jnp.where(valid_c, s_cmp, MASK_NEG), axis=-1) * valid_c
    o_cmp = jnp.einsum('bghsc,bcgd->bsghd', p_cmp.astype(v_cmp.dtype), v_cmp)

    ns = S // NSA_SEL_BLOCK
    overlap = jnp.asarray(_overlap_matrix(S))
    imp = jnp.einsum('bghsc,cj->bgsj', p_cmp, overlap)
    cur = pos // NSA_SEL_BLOCK
    j = jnp.arange(ns)[None, :]
    forced = (j == 0) | (j == cur[:, None]) | (j == cur[:, None] - 1)
    sel_score = jnp.where(forced, 1e9, jnp.where(j > cur[:, None], -1e9, imp))
    _, idx = lax.top_k(sel_score, min(NSA_N_SELECT, ns))
    o_slc = _selected_attention(q, ks, vs, idx, pos)

    o_win = _window_attention(q, kw, vw)

    gt = jax.nn.sigmoid(gate.astype(jnp.float32)).reshape(B, S, G, HPG, 3)
    o = gt[..., 0:1] * o_cmp + gt[..., 1:2] * o_slc + gt[..., 2:3] * o_win
    return o.reshape(B, S, NSA_WIDTH).astype(dt)


def setup_inputs(seed: int = 0) -> dict:
    key = jax.random.key(seed)
    ks = jax.random.split(key, 40)
    D, L = D_MODEL, DEPTH

    def nrm(k, shape, scale):
        return jax.random.normal(k, shape, jnp.float32) * scale

    def gain(k, shape):
        return 1.0 + 0.05 * jax.random.normal(k, shape, jnp.float32)

    rw_in = sum(RWKV_COLS)
    return {
        'x': nrm(ks[0], (BATCH, SEQ, D), 1.0),
        'w_in': nrm(ks[1], (L, D, IN_COLS), D ** -0.5),
        'w_in_vres': nrm(ks[2], (L - 1, D, RWKV_VRES_RANK), D ** -0.5),
        'gla_w_a2': nrm(ks[3], (L, GLA_GATE_RANK, GLA_WIDTH), GLA_GATE_RANK ** -0.5),
        'gla_b_a': nrm(ks[4], (L, GLA_WIDTH), 0.1),
        'gla_ln_w': gain(ks[5], (L, GLA_WIDTH)),
        'gla_ln_b': nrm(ks[6], (L, GLA_WIDTH), 0.02),
        'rwkv_mu': jax.random.uniform(ks[7], (L, rw_in), jnp.float32),
        'rwkv_mu_vres': jax.random.uniform(ks[8], (L - 1, RWKV_VRES_RANK), jnp.float32),
        'rwkv_w0': jax.random.uniform(ks[9], (L, RWKV_WIDTH), jnp.float32, -5.0, 1.0),
        'rwkv_w2': nrm(ks[10], (L, RWKV_DECAY_RANK, RWKV_WIDTH), 0.5 * RWKV_DECAY_RANK ** -0.5),
        'rwkv_a0': nrm(ks[11], (L, RWKV_WIDTH), 0.1),
        'rwkv_a2': nrm(ks[12], (L, RWKV_ICLR_RANK, RWKV_WIDTH), RWKV_ICLR_RANK ** -0.5),
        'rwkv_v0': nrm(ks[13], (L - 1, RWKV_WIDTH), 0.1),
        'rwkv_v2': nrm(ks[14], (L - 1, RWKV_VRES_RANK, RWKV_WIDTH), RWKV_VRES_RANK ** -0.5),
        'rwkv_g2': nrm(ks[15], (L, RWKV_GATE_RANK, RWKV_WIDTH), RWKV_GATE_RANK ** -0.5),
        'rwkv_k_k': 0.85 + 0.05 * jax.random.normal(ks[16], (L, RWKV_WIDTH), jnp.float32),
        'rwkv_k_a': gain(ks[17], (L, RWKV_WIDTH)),
        'rwkv_r_k': nrm(ks[18], (L, RWKV_HEADS, HEAD_DIM), 0.1),
        'rwkv_ln_w': gain(ks[19], (L, RWKV_WIDTH)),
        'rwkv_ln_b': nrm(ks[20], (L, RWKV_WIDTH), 0.02),
        'nsa_pos_k': nrm(ks[21], (L, NSA_CMP_BLOCK, HEAD_DIM), 0.02),
        'nsa_pos_v': nrm(ks[22], (L, NSA_CMP_BLOCK, HEAD_DIM), 0.02),
        'nsa_wk1': nrm(ks[23], (L, NSA_CMP_BLOCK * HEAD_DIM, NSA_CMP_HIDDEN), (NSA_CMP_BLOCK * HEAD_DIM) ** -0.5),
        'nsa_wk2': nrm(ks[24], (L, NSA_CMP_HIDDEN, HEAD_DIM), NSA_CMP_HIDDEN ** -0.5),
        'nsa_wv1': nrm(ks[25], (L, NSA_CMP_BLOCK * HEAD_DIM, NSA_CMP_HIDDEN), (NSA_CMP_BLOCK * HEAD_DIM) ** -0.5),
        'nsa_wv2': nrm(ks[26], (L, NSA_CMP_HIDDEN, HEAD_DIM), NSA_CMP_HIDDEN ** -0.5),
        'w_out': nrm(ks[27], (L, D, D), DEEPNORM_BETA * D ** -0.5),
        'ln1_w': gain(ks[28], (L, D)),
        'ln1_b': nrm(ks[29], (L, D), 0.02),
        'ffn_w_gate': nrm(ks[30], (L, D, FFN_HIDDEN), D ** -0.5),
        'ffn_w_up': nrm(ks[31], (L, D, FFN_HIDDEN), D ** -0.5),
        'ffn_w_down': nrm(ks[32], (L, FFN_HIDDEN, D), DEEPNORM_BETA * FFN_HIDDEN ** -0.5),
        'ln2_w': gain(ks[33], (L, D)),
        'ln2_b': nrm(ks[34], (L, D), 0.02),
    }


def reference(x, w_in, w_in_vres, gla_w_a2, gla_b_a, gla_ln_w, gla_ln_b,
              rwkv_mu, rwkv_mu_vres, rwkv_w0, rwkv_w2, rwkv_a0, rwkv_a2, rwkv_v0, rwkv_v2,
              rwkv_g2, rwkv_k_k, rwkv_k_a, rwkv_r_k, rwkv_ln_w, rwkv_ln_b,
              nsa_pos_k, nsa_pos_v, nsa_wk1, nsa_wk2, nsa_wv1, nsa_wv2,
              w_out, ln1_w, ln1_b, ffn_w_gate, ffn_w_up, ffn_w_down, ln2_w, ln2_b):
    pos = jnp.arange(x.shape[1], dtype=jnp.int32)
    v_first = None
    for l in range(DEPTH):
        w_cols = w_in[l] if l == 0 else jnp.concatenate([w_in[l], w_in_vres[l - 1]], axis=1)
        h = x @ w_cols
        h_gla, h_rwkv, h_nsa, h_vres = _split(
            h, (sum(GLA_COLS), sum(RWKV_COLS), sum(NSA_COLS), h.shape[-1] - IN_COLS))

        gq, gk, gv, gg, galr = _split(h_gla, GLA_COLS)
        o_gla = _gla_mixer(gq, gk, gv, gg, galr, gla_w_a2[l], gla_b_a[l], gla_ln_w[l], gla_ln_b[l])

        rr, rk, rv, rwlr, ralr, rglr = _split(_token_shift_mix(h_rwkv, rwkv_mu[l]), RWKV_COLS)
        if l == 0:
            v_first = rv
        else:
            vlr = _token_shift_mix(h_vres, rwkv_mu_vres[l - 1])
            rv = rv + (v_first - rv) * jax.nn.sigmoid(rwkv_v0[l - 1] + vlr @ rwkv_v2[l - 1])
        o_rwkv = _rwkv7_mixer(rr, rk, rv, rwlr, ralr, rglr, rwkv_w0[l], rwkv_w2[l], rwkv_a0[l],
                              rwkv_a2[l], rwkv_g2[l], rwkv_k_k[l], rwkv_k_a[l], rwkv_r_k[l],
                              rwkv_ln_w[l], rwkv_ln_b[l])

        o_nsa = _nsa_mixer(h_nsa, pos, nsa_pos_k[l], nsa_pos_v[l], nsa_wk1[l], nsa_wk2[l],
                           nsa_wv1[l], nsa_wv2[l])

        mix = jnp.concatenate([o_gla, o_rwkv, o_nsa], axis=-1) @ w_out[l]
        x = _layer_norm(DEEPNORM_ALPHA * x + mix, ln1_w[l], ln1_b[l])

        ffn = (jax.nn.silu(x @ ffn_w_gate[l]) * (x @ ffn_w_up[l])) @ ffn_w_down[l]
        x = _layer_norm(DEEPNORM_ALPHA * x + ffn, ln2_w[l], ln2_b[l])
    return x
```

```python
import functools

import numpy as np
import jax
import jax.numpy as jnp
from jax import lax
from jax.experimental import pallas as pl
from jax.experimental.pallas import tpu as pltpu

F32 = jnp.float32
BF16 = jnp.bfloat16
HIGHEST = lax.Precision.HIGHEST

D_MODEL = 1024
HEAD_DIM = 64
N_LIN_HEADS = 4
LIN_WIDTH = N_LIN_HEADS * HEAD_DIM
GLA_GATE_RANK = 16
GLA_TAU = 16.0
RWKV_DECAY_RANK = 64
RWKV_ICLR_RANK = 64
RWKV_VRES_RANK = 32
RWKV_GATE_RANK = 160
RWKV_GN_EPS = 64e-5
NSA_HEADS = 8
NSA_KV_HEADS = 2
NSA_HPG = NSA_HEADS // NSA_KV_HEADS
NSA_WIDTH = NSA_HEADS * HEAD_DIM
NSA_KV_WIDTH = NSA_KV_HEADS * HEAD_DIM
NSA_CMP_BLOCK = 32
NSA_CMP_STRIDE = 16
NSA_CMP_HIDDEN = 256
NSA_SEL_BLOCK = 64
NSA_N_SELECT = 16
NSA_WINDOW = 512
ROPE_THETA = 10000.0
FFN_HIDDEN = 2816
LN_EPS = 1e-5
MASK_NEG = -1e30

LANES = 128
CHUNK = 64
SUB = 16
QBLK = 128
VMEM_LIMIT = 56 * 1024 * 1024

GLA_SLAB = 4 * LIN_WIDTH + LANES
RWKV_SLAB = 3 * LIN_WIDTH + LANES + LANES + 2 * LANES + LANES
NSA_SLAB = NSA_WIDTH + 6 * NSA_KV_WIDTH + LANES
RW_WLR, RW_ALR, RW_GLR, RW_VLR = 768, 896, 1024, 1280


def _dot(a, b, dims=None, hi=False):
    dims = dims or (((a.ndim - 1,), (0,)), ((), ()))
    if hi:
        return lax.dot_general(a, b, dims, precision=HIGHEST, preferred_element_type=F32)
    return lax.dot_general(a.astype(BF16), b.astype(BF16), dims, preferred_element_type=F32)


_NT = (((1,), (1,)), ((), ()))
_TN = (((0,), (0,)), ((), ()))


def _sigmoid(x):
    return 1.0 / (1.0 + jnp.exp(-x))


def _softplus(x):
    return jnp.maximum(x, 0.0) + jnp.log(1.0 + jnp.exp(-jnp.abs(x)))


def _head_norm(x, w, b, eps):
    mu = jnp.mean(x, axis=-1, keepdims=True)
    xc = x - mu
    var = jnp.mean(xc * xc, axis=-1, keepdims=True)
    return xc * lax.rsqrt(var + eps) * w + b


def _layer_norm(x, w, b):
    mu = jnp.mean(x, axis=-1, keepdims=True)
    xc = x - mu
    var = jnp.mean(xc * xc, axis=-1, keepdims=True)
    return xc * lax.rsqrt(var + LN_EPS) * w + b


def _params(*sem):
    return pltpu.CompilerParams(dimension_semantics=sem, vmem_limit_bytes=VMEM_LIMIT)


def _full(shape):
    nd = len(shape)
    return pl.BlockSpec(shape, lambda *_: (0,) * nd)


def _proj_kernel(x_ref, wg_ref, wr_ref, wn_ref, og_ref, or_ref, on_ref):
    xb = x_ref[...].astype(BF16)
    og_ref[...] = jnp.dot(xb, wg_ref[...], preferred_element_type=F32)
    or_ref[...] = jnp.dot(xb, wr_ref[...], preferred_element_type=F32)
    on_ref[...] = jnp.dot(xb, wn_ref[...], preferred_element_type=F32)


def _proj(x2, wg, wr, wn, tm=256):
    t = x2.shape[0]
    return pl.pallas_call(
        _proj_kernel,
        grid=(t // tm,),
        in_specs=[pl.BlockSpec((tm, D_MODEL), lambda i: (i, 0)),
                  _full(wg.shape), _full(wr.shape), _full(wn.shape)],
        out_specs=[pl.BlockSpec((tm, GLA_SLAB), lambda i: (i, 0)),
                   pl.BlockSpec((tm, RWKV_SLAB), lambda i: (i, 0)),
                   pl.BlockSpec((tm, NSA_SLAB), lambda i: (i, 0))],
        out_shape=[jax.ShapeDtypeStruct((t, GLA_SLAB), F32),
                   jax.ShapeDtypeStruct((t, RWKV_SLAB), F32),
                   jax.ShapeDtypeStruct((t, NSA_SLAB), F32)],
        compiler_params=_params("parallel"),
    )(x2, wg, wr, wn)


def _group_rows(x, u):
    return jnp.concatenate(
        [jnp.broadcast_to(x[SUB * i + u:SUB * i + u + 1, :], (SUB, x.shape[1]))
         for i in range(CHUNK // SUB)], axis=0)


def _gla_kernel(h_ref, wa2_ref, ba_ref, lnw_ref, lnb_ref, tril_ref, blk_ref, ones_ref, dm_ref,
                o_ref, st_ref, *, n_chunks):
    @pl.when(pl.program_id(1) == 0)
    def _():
        st_ref[...] = jnp.zeros_like(st_ref)

    row_in_sub = lax.broadcasted_iota(jnp.int32, (CHUNK, LIN_WIDTH), 0) % SUB
    dm = dm_ref[...]

    def chunk(ci, carry):
        r0 = pl.multiple_of(ci * CHUNK, CHUNK)
        p = h_ref[0, pl.ds(r0, CHUNK), :]
        q = p[:, 0:256] * (HEAD_DIM ** -0.5)
        k = p[:, 256:512]
        v = p[:, 512:768]
        g = p[:, 768:1024]
        z = _dot(p[:, 1024:1152], wa2_ref[...], hi=True) + ba_ref[...]
        log_a = -_softplus(-z) * (1.0 / GLA_TAU)
        b = _dot(tril_ref[...], log_a, hi=True)
        b_end = _dot(blk_ref[...], log_a, hi=True)
        qe = q * jnp.exp(b)
        kd = k * jnp.exp(b_end - b)
        p_end = jnp.exp(b_end)

        at = jnp.zeros((CHUNK, LIN_WIDTH), F32)
        for u in range(SUB):
            bt = _group_rows(b, u)
            qt = _group_rows(q, u)
            e = jnp.exp(jnp.where(row_in_sub <= u, bt - b, MASK_NEG)) * (qt * k)
            a_rep = _dot(e, ones_ref[...])
            at = at + jnp.where(dm == u, a_rep, 0.0)

        for h in range(N_LIN_HEADS):
            hs = slice(HEAD_DIM * h, HEAD_DIM * (h + 1))
            o_diag = _dot(at[:, hs], v[:, hs], _TN)
            st = st_ref[h]
            inter = []
            for i in range(CHUNK // SUB):
                rs = slice(SUB * i, SUB * (i + 1))
                inter.append(_dot(qe[rs, hs], st, _NT))
                st = st * p_end[SUB * i:SUB * i + 1, hs] + _dot(v[rs, hs], kd[rs, hs], _TN)
            st_ref[h] = st
            o_h = o_diag + jnp.concatenate(inter, axis=0)
            o_h = _head_norm(o_h, lnw_ref[:, hs], lnb_ref[:, hs], LN_EPS)
            g_h = g[:, hs]
            o_ref[0, pl.ds(r0, CHUNK), hs] = o_h * (g_h * _sigmoid(g_h))
        return carry

    lax.fori_loop(0, n_chunks, chunk, 0)


def _gla(hg, wa2, ba, lnw, lnb, ts=512):
    bsz, s, _ = hg.shape
    r = np.arange(CHUNK)
    tril = ((r[:, None] // SUB == r[None, :] // SUB) & (r[None, :] <= r[:, None])).astype(np.float32)
    blk = (r[:, None] // SUB == r[None, :] // SUB).astype(np.float32)
    c = np.arange(LIN_WIDTH)
    ones = (c[:, None] // HEAD_DIM == c[None, :] // HEAD_DIM).astype(np.float32)
    dm = ((c[None, :] % HEAD_DIM) - SUB * (r[:, None] // SUB)).astype(np.int32)
    consts = [jnp.asarray(tril), jnp.asarray(blk), jnp.asarray(ones, BF16), jnp.asarray(dm)]
    small = [wa2, ba, lnw, lnb] + consts
    return pl.pallas_call(
        functools.partial(_gla_kernel, n_chunks=ts // CHUNK),
        grid=(bsz, s // ts),
        in_specs=[pl.BlockSpec((1, ts, GLA_SLAB), lambda b, j: (b, j, 0))] + [_full(a.shape) for a in small],
        out_specs=pl.BlockSpec((1, ts, LIN_WIDTH), lambda b, j: (b, j, 0)),
        out_shape=jax.ShapeDtypeStruct((bsz, s, LIN_WIDTH), F32),
        scratch_shapes=[pltpu.VMEM((N_LIN_HEADS, HEAD_DIM, HEAD_DIM), F32)],
        compiler_params=_params("parallel", "arbitrary"),
    )(hg, *small)


def _rwkv_kernel(*refs, n_chunks, has_vres):
    if has_vres:
        (h_ref, vf_ref, mu_ref, w0_ref, w2_ref, a0_ref, a2_ref, g2_ref, kk_ref, ka_ref, rk_ref,
         lnw_ref, lnb_ref, v0_ref, v2_ref, tril_ref, o_ref, st_ref, prev_ref) = refs
    else:
        (h_ref, mu_ref, w0_ref, w2_ref, a0_ref, a2_ref, g2_ref, kk_ref, ka_ref, rk_ref,
         lnw_ref, lnb_ref, tril_ref, o_ref, vf_ref, st_ref, prev_ref) = refs

    @pl.when(pl.program_id(1) == 0)
    def _():
        st_ref[...] = jnp.zeros_like(st_ref)
        prev_ref[...] = jnp.zeros_like(prev_ref)

    row = lax.broadcasted_iota(jnp.int32, (CHUNK, RWKV_SLAB), 0)
    ti = lax.broadcasted_iota(jnp.int32, (CHUNK, CHUNK), 0)
    si = lax.broadcasted_iota(jnp.int32, (CHUNK, CHUNK), 1)
    strict = si < ti
    incl = si <= ti
    eye = (si == ti).astype(F32)

    def chunk(ci, carry):
        r0 = pl.multiple_of(ci * CHUNK, CHUNK)
        p = h_ref[0, pl.ds(r0, CHUNK), :]
        prev = jnp.where(row == 0, prev_ref[...], pltpu.roll(p, 1, 0))
        prev_ref[...] = p[CHUNK - 1:CHUNK, :]
        xm = p + (prev - p) * mu_ref[...]
        r = xm[:, 0:256]
        k = xm[:, 256:512]
        v = xm[:, 512:768]
        w_log = -_softplus(-(w0_ref[...] + _dot(jnp.tanh(xm[:, RW_WLR:RW_ALR]), w2_ref[...], hi=True))) - 0.5
        lw = -jnp.exp(w_log)
        a = _sigmoid(a0_ref[...] + _dot(xm[:, RW_ALR:RW_GLR], a2_ref[...], hi=True))
        g = _dot(_sigmoid(xm[:, RW_GLR:RW_VLR]), g2_ref[...], hi=True)
        if has_vres:
            mixv = _sigmoid(v0_ref[...] + _dot(xm[:, RW_VLR:RWKV_SLAB], v2_ref[...], hi=True))
            v = v + (vf_ref[0, pl.ds(r0, CHUNK), :] - v) * mixv
        else:
            vf_ref[0, pl.ds(r0, CHUNK), :] = v
        kkv = k * kk_ref[...]
        k2 = k * (1.0 + (a - 1.0) * ka_ref[...])
        c = _dot(tril_ref[...], lw, hi=True)
        c_end = c[CHUNK - 1:CHUNK, :]
        e_c = jnp.exp(c)
        e_nc = jnp.exp(-c)
        e_prev = jnp.exp(c - lw)
        e_rest = jnp.exp(c_end - c)
        p_end = jnp.exp(c_end)

        for h in range(N_LIN_HEADS):
            hs = slice(HEAD_DIM * h, HEAD_DIM * (h + 1))
            kk_h = kkv[:, hs]
            nrm = jnp.sqrt(jnp.sum(kk_h * kk_h, axis=-1, keepdims=True))
            kk_h = kk_h / jnp.maximum(nrm, 1e-12)
            av = -kk_h
            bv = kk_h * a[:, hs]
            r_h, k_h, v_h = r[:, hs], k2[:, hs], v[:, hs]
            s0 = st_ref[h]
            ar = jnp.concatenate([av * e_prev[:, hs], r_h * e_c[:, hs]], axis=0)
            bk = jnp.concatenate([bv * e_nc[:, hs], k_h * e_nc[:, hs]], axis=0)
            gm = _dot(ar, bk, _NT, hi=True)
            n_ab = jnp.where(strict, gm[0:CHUNK, 0:CHUNK], 0.0)
            m_ak = jnp.where(strict, gm[0:CHUNK, CHUNK:], 0.0)
            r_b = jnp.where(incl, gm[CHUNK:, 0:CHUNK], 0.0)
            r_k = jnp.where(incl, gm[CHUNK:, CHUNK:], 0.0)
            a_s = _dot(ar, s0, _NT, hi=True)
            rhs = a_s[0:CHUNK] + _dot(m_ak, v_h, hi=True)
            inv = eye + n_ab
            pw = n_ab
            for _ in range(5):
                pw = _dot(pw, pw, hi=True)
                inv = inv + _dot(inv, pw, hi=True)
            u = _dot(inv, rhs, hi=True)
            y = a_s[CHUNK:] + _dot(r_b, u, hi=True) + _dot(r_k, v_h, hi=True)
            st_ref[h] = (s0 * p_end[:, hs] + _dot(u, bv * e_rest[:, hs], _TN, hi=True)
                         + _dot(v_h, k_h * e_rest[:, hs], _TN, hi=True))
            y = _head_norm(y, lnw_ref[:, hs], lnb_ref[:, hs], RWKV_GN_EPS)
            bonus = jnp.sum(r_h * k_h * rk_ref[:, hs], axis=-1, keepdims=True) * v_h
            o_ref[0, pl.ds(r0, CHUNK), hs] = (y + bonus) * g[:, hs]
        return carry

    lax.fori_loop(0, n_chunks, chunk, 0)


def _rwkv(hr, v_first, small, ts=512):
    bsz, s, _ = hr.shape
    has_vres = v_first is not None
    r = np.arange(CHUNK)
    tril = jnp.asarray((r[None, :] <= r[:, None]).astype(np.float32))
    small = list(small) + [tril]
    seq_spec = pl.BlockSpec((1, ts, LIN_WIDTH), lambda b, j: (b, j, 0))
    in_specs = [pl.BlockSpec((1, ts, RWKV_SLAB), lambda b, j: (b, j, 0))]
    args = [hr]
    if has_vres:
        in_specs.append(seq_spec)
        args.append(v_first)
    in_specs += [_full(a.shape) for a in small]
    out_sd = jax.ShapeDtypeStruct((bsz, s, LIN_WIDTH), F32)
    return pl.pallas_call(
        functools.partial(_rwkv_kernel, n_chunks=ts // CHUNK, has_vres=has_vres),
        grid=(bsz, s // ts),
        in_specs=in_specs,
        out_specs=seq_spec if has_vres else [seq_spec, seq_spec],
        out_shape=out_sd if has_vres else [out_sd, out_sd],
        scratch_shapes=[pltpu.VMEM((N_LIN_HEADS, HEAD_DIM, HEAD_DIM), F32),
                        pltpu.VMEM((1, RWKV_SLAB), F32)],
        compiler_params=_params("parallel", "arbitrary"),
    )(*args, *small)


def _nsa_prep_kernel(h_ref, cos_ref, sin_ref, rot_ref, q_ref, kv_ref, gt_ref):
    cos = cos_ref[...]
    sin = sin_ref[...]
    rot = rot_ref[...]

    def rope(x):
        return x * cos + _dot(x, rot, hi=True) * sin

    for i in range(NSA_WIDTH // LANES):
        cs = slice(LANES * i, LANES * (i + 1))
        q_ref[0, :, cs] = rope(h_ref[0, :, cs]) * (HEAD_DIM ** -0.5)
    for i in range(6):
        src = slice(NSA_WIDTH + LANES * i, NSA_WIDTH + LANES * (i + 1))
        dst = slice(LANES * i, LANES * (i + 1))
        x = h_ref[0, :, src]
        kv_ref[0, :, dst] = rope(x) if i % 2 == 0 else x
    gt_ref[0] = _sigmoid(h_ref[0, :, NSA_WIDTH + 6 * LANES:NSA_SLAB])


def _nsa_prep(hn, cos2, sin2, rot, ts=512):
    bsz, s, _ = hn.shape
    return pl.pallas_call(
        _nsa_prep_kernel,
        grid=(bsz, s // ts),
        in_specs=[pl.BlockSpec((1, ts, NSA_SLAB), lambda b, j: (b, j, 0)),
                  pl.BlockSpec((ts, LANES), lambda b, j: (j, 0)),
                  pl.BlockSpec((ts, LANES), lambda b, j: (j, 0)),
                  _full(rot.shape)],
        out_specs=[pl.BlockSpec((1, ts, NSA_WIDTH), lambda b, j: (b, j, 0)),
                   pl.BlockSpec((1, ts, 6 * LANES), lambda b, j: (b, j, 0)),
                   pl.BlockSpec((1, ts, LANES), lambda b, j: (b, j, 0))],
        out_shape=[jax.ShapeDtypeStruct((bsz, s, NSA_WIDTH), F32),
                   jax.ShapeDtypeStruct((bsz, s, 6 * LANES), F32),
                   jax.ShapeDtypeStruct((bsz, s, LANES), F32)],
        compiler_params=_params("parallel", "parallel"),
    )(hn, cos2, sin2, rot)


def _nsa_cmp_kernel(kc_ref, vc_ref, pk_ref, pv_ref, wk1_ref, wk2_ref, wv1_ref, wv2_ref, ko_ref, vo_ref,
                    *, n_blk):
    half = NSA_CMP_BLOCK // 2

    def compress(t_ref, pos_ref, w1_ref, w2_ref, o_ref):
        for g in range(NSA_KV_HEADS):
            gs = slice(HEAD_DIM * g, HEAD_DIM * (g + 1))
            top = jnp.zeros((n_blk, NSA_CMP_HIDDEN), F32)
            bot = jnp.zeros((n_blk, NSA_CMP_HIDDEN), F32)
            for l in range(half):
                rows = t_ref[0, pl.ds(l, n_blk, stride=NSA_CMP_STRIDE), :][:, gs]
                top = top + _dot(rows + pos_ref[l:l + 1, :], w1_ref[l], hi=True)
                bot = bot + _dot(rows + pos_ref[half + l:half + l + 1, :], w1_ref[half + l], hi=True)
            hid = top + pltpu.roll(bot, n_blk - 1, 0)
            act = 0.5 * hid * (1.0 + jnp.tanh(0.7978845608028654 * (hid + 0.044715 * hid * hid * hid)))
            o_ref[0, :, gs] = _dot(act, w2_ref[...], hi=True)

    compress(kc_ref, pk_ref, wk1_ref, wk2_ref, ko_ref)
    compress(vc_ref, pv_ref, wv1_ref, wv2_ref, vo_ref)


def _nsa_cmp(kv, pos_k, pos_v, wk1, wk2, wv1, wv2):
    bsz, s, _ = kv.shape
    n_blk = s // NSA_CMP_STRIDE
    wk1 = wk1.reshape(NSA_CMP_BLOCK, HEAD_DIM, NSA_CMP_HIDDEN)
    wv1 = wv1.reshape(NSA_CMP_BLOCK, HEAD_DIM, NSA_CMP_HIDDEN)
    small = [pos_k, pos_v, wk1, wk2, wv1, wv2]
    out_spec = pl.BlockSpec((1, n_blk, LANES), lambda b: (b, 0, 0))
    out_sd = jax.ShapeDtypeStruct((bsz, n_blk, LANES), F32)
    return pl.pallas_call(
        functools.partial(_nsa_cmp_kernel, n_blk=n_blk),
        grid=(bsz,),
        in_specs=[pl.BlockSpec((1, s, LANES), lambda b: (b, 0, 0)),
                  pl.BlockSpec((1, s, LANES), lambda b: (b, 0, 1))] + [_full(a.shape) for a in small],
        out_specs=[out_spec, out_spec],
        out_shape=[out_sd, out_sd],
        compiler_params=_params("parallel"),
    )(kv, kv, *small)


def _nsa_attn_kernel(q_ref, gt_ref, kc_ref, vc_ref, ks_ref, vs_ref, kw_ref, vw_ref, ov_ref, ex_ref,
                     o_ref, acc_ref, *, n_cmp):
    qi = pl.program_id(1)
    rows = NSA_HPG * QBLK
    n_sel_blk = ov_ref.shape[1]
    t_pos = qi * QBLK + lax.broadcasted_iota(jnp.int32, (rows, QBLK), 0) % QBLK
    lane = lax.broadcasted_iota(jnp.int32, (rows, QBLK), 1)
    gt = gt_ref[0]
    zeros64 = jnp.zeros((QBLK, HEAD_DIM), F32)

    def softmax_step(s, valid, v_blk, m, l):
        s = jnp.where(valid, s, MASK_NEG)
        m_new = jnp.maximum(m, jnp.max(s, axis=-1, keepdims=True))
        alpha = jnp.exp(m - m_new)
        p = jnp.where(valid, jnp.exp(s - m_new), 0.0)
        acc_ref[...] = acc_ref[...] * alpha + _dot(p, v_blk)
        return m_new, l * alpha + jnp.sum(p, axis=-1, keepdims=True)

    for g in range(NSA_KV_HEADS):
        gs = slice(HEAD_DIM * g, HEAD_DIM * (g + 1))
        pieces = []
        for hp in range(NSA_HPG):
            qh = q_ref[0, :, (g * NSA_HPG + hp) * HEAD_DIM:(g * NSA_HPG + hp + 1) * HEAD_DIM]
            pieces.append(jnp.concatenate([qh, zeros64] if g == 0 else [zeros64, qh], axis=1))
        qs = jnp.concatenate(pieces, axis=0)

        s_c = _dot(qs, kc_ref[0], _NT, hi=True)
        valid_c = (lane * NSA_CMP_STRIDE + (NSA_CMP_BLOCK - 1) <= t_pos) & (lane < n_cmp)
        s_c = jnp.where(valid_c, s_c, MASK_NEG)
        e_c = jnp.exp(s_c - jnp.max(s_c, axis=-1, keepdims=True))
        p_c = jnp.where(valid_c, e_c / jnp.sum(e_c, axis=-1, keepdims=True), 0.0)
        o_cmp = _dot(p_c, vc_ref[0])
        p_sum = p_c[0:QBLK] + p_c[QBLK:2 * QBLK] + p_c[2 * QBLK:3 * QBLK] + p_c[3 * QBLK:]
        imp = _dot(p_sum, ov_ref[...], hi=True)

        tq = qi * QBLK + lax.broadcasted_iota(jnp.int32, (QBLK, n_sel_blk), 0)
        j = lax.broadcasted_iota(jnp.int32, (QBLK, n_sel_blk), 1)
        cur = tq // NSA_SEL_BLOCK
        forced = (j == 0) | (j == cur) | (j == cur - 1)
        score = jnp.where(forced, 1e9, jnp.where(j > cur, -1e9, imp))
        rank = jnp.zeros((QBLK, n_sel_blk), F32)
        for i in range(n_sel_blk):
            s_i = score[:, i:i + 1]
            ahead = (s_i > score) | ((s_i == score) & (j > i))
            rank = rank + jnp.where(ahead, 1.0, 0.0)
        sel = jnp.where(rank < NSA_N_SELECT, 1.0, 0.0)

        acc_ref[...] = jnp.zeros_like(acc_ref)

        def slc_step(kb, ml):
            k0 = pl.multiple_of(kb * QBLK, QBLK)
            s = _dot(qs, ks_ref[0, pl.ds(k0, QBLK), :], _NT)
            chosen = jnp.concatenate([_dot(sel, ex_ref[kb])] * NSA_HPG, axis=0)
            valid = (chosen > 0.5) & (kb * QBLK + lane <= t_pos)
            return softmax_step(s, valid, vs_ref[0, pl.ds(k0, QBLK), :], *ml)

        m0 = jnp.full((rows, 1), MASK_NEG, F32)
        l0 = jnp.zeros((rows, 1), F32)
        _, l_s = lax.fori_loop(0, qi + 1, slc_step, (m0, l0))
        o_slc = acc_ref[...] / l_s

        acc_ref[...] = jnp.zeros_like(acc_ref)
        ml = (m0, l0)
        for d in range(NSA_WINDOW // QBLK, -1, -1):
            kb = qi - d
            kbc = jnp.maximum(kb, 0)
            k0 = pl.multiple_of(kbc * QBLK, QBLK)
            s = _dot(qs, kw_ref[0, pl.ds(k0, QBLK), :], _NT)
            k_pos = kb * QBLK + lane
            rel = t_pos - k_pos
            valid = (rel >= 0) & (rel < NSA_WINDOW) & (k_pos >= 0)
            ml = softmax_step(s, valid, vw_ref[0, pl.ds(k0, QBLK), :], *ml)
        o_win = acc_ref[...] / ml[1]

        for hp in range(NSA_HPG):
            rs = slice(QBLK * hp, QBLK * (hp + 1))
            c0 = (g * NSA_HPG + hp) * 3
            o_h = (gt[:, c0:c0 + 1] * o_cmp[rs, gs] + gt[:, c0 + 1:c0 + 2] * o_slc[rs, gs]
                   + gt[:, c0 + 2:c0 + 3] * o_win[rs, gs])
            o_ref[0, :, (g * NSA_HPG + hp) * HEAD_DIM:(g * NSA_HPG + hp + 1) * HEAD_DIM] = o_h


def _nsa_attn(qr, kv, gt, kcmp, vcmp):
    bsz, s, _ = qr.shape
    n_blk = s // NSA_CMP_STRIDE
    n_cmp = n_blk - NSA_CMP_BLOCK // NSA_CMP_STRIDE + 1
    n_sel = s // NSA_SEL_BLOCK
    assert n_blk == QBLK, "compressed-score tiles are laid out one stride block per lane"
    c0 = np.arange(n_blk) * NSA_CMP_STRIDE
    s0 = np.arange(n_sel) * NSA_SEL_BLOCK
    lo = np.maximum(c0[:, None], s0[None, :])
    hi = np.minimum(c0[:, None] + NSA_CMP_BLOCK, s0[None, :] + NSA_SEL_BLOCK)
    overlap = (np.maximum(hi - lo, 0) / NSA_CMP_STRIDE).astype(np.float32)
    overlap[n_cmp:] = 0.0
    key = np.arange(s)
    expand = (key[None, :] // NSA_SEL_BLOCK == np.arange(n_sel)[:, None]).astype(np.float32)
    expand = expand.reshape(n_sel, s // QBLK, QBLK).transpose(1, 0, 2)
    consts = [jnp.asarray(overlap), jnp.asarray(expand, BF16)]

    def kv_spec(col):
        return pl.BlockSpec((1, s, LANES), lambda b, i: (b, 0, col))

    cmp_spec = pl.BlockSpec((1, n_blk, LANES), lambda b, i: (b, 0, 0))
    return pl.pallas_call(
        functools.partial(_nsa_attn_kernel, n_cmp=n_cmp),
        grid=(bsz, s // QBLK),
        in_specs=[pl.BlockSpec((1, QBLK, NSA_WIDTH), lambda b, i: (b, i, 0)),
                  pl.BlockSpec((1, QBLK, LANES), lambda b, i: (b, i, 0)),
                  cmp_spec, cmp_spec, kv_spec(2), kv_spec(3), kv_spec(4), kv_spec(5)]
        + [_full(a.shape) for a in consts],
        out_specs=pl.BlockSpec((1, QBLK, NSA_WIDTH), lambda b, i: (b, i, 0)),
        out_shape=jax.ShapeDtypeStruct((bsz, s, NSA_WIDTH), F32),
        scratch_shapes=[pltpu.VMEM((NSA_HPG * QBLK, LANES), F32)],
        compiler_params=_params("parallel", "arbitrary"),
    )(qr, gt, kcmp, vcmp, kv, kv, kv, kv, *consts)


def _out_ln_kernel(x_ref, og_ref, or_ref, on_ref, w_ref, lw_ref, lb_ref, o_ref, *, alpha):
    mix = (_dot(og_ref[...], w_ref[0:LIN_WIDTH, :])
           + _dot(or_ref[...], w_ref[LIN_WIDTH:2 * LIN_WIDTH, :])
           + _dot(on_ref[...], w_ref[2 * LIN_WIDTH:, :]))
    o_ref[...] = _layer_norm(alpha * x_ref[...] + mix, lw_ref[...], lb_ref[...])


def _out_ln(x2, og, orw, on, w, lw, lb, alpha, tm=512):
    t = x2.shape[0]

    def rows(width):
        return pl.BlockSpec((tm, width), lambda i: (i, 0))

    return pl.pallas_call(
        functools.partial(_out_ln_kernel, alpha=alpha),
        grid=(t // tm,),
        in_specs=[rows(D_MODEL), rows(LIN_WIDTH), rows(LIN_WIDTH), rows(NSA_WIDTH),
                  _full(w.shape), _full(lw.shape), _full(lb.shape)],
        out_specs=rows(D_MODEL),
        out_shape=jax.ShapeDtypeStruct((t, D_MODEL), F32),
        compiler_params=_params("parallel"),
    )(x2, og, orw, on, w, lw, lb)


def _ffn_ln_kernel(x_ref, wg_ref, wu_ref, wd_ref, lw_ref, lb_ref, o_ref, acc_ref, *, alpha):
    j = pl.program_id(1)
    x = x_ref[...]
    xb = x.astype(BF16)
    gate = jnp.dot(xb, wg_ref[...], preferred_element_type=F32)
    up = jnp.dot(xb, wu_ref[...], preferred_element_type=F32)
    part = _dot(gate * _sigmoid(gate) * up, wd_ref[...])

    @pl.when(j == 0)
    def _():
        acc_ref[...] = part

    @pl.when(j > 0)
    def _():
        acc_ref[...] += part

    @pl.when(j == pl.num_programs(1) - 1)
    def _():
        o_ref[...] = _layer_norm(alpha * x + acc_ref[...], lw_ref[...], lb_ref[...])


def _ffn_ln(x2, wg, wu, wd, lw, lb, alpha, tm=512, th=1408):
    t = x2.shape[0]
    return pl.pallas_call(
        functools.partial(_ffn_ln_kernel, alpha=alpha),
        grid=(t // tm, FFN_HIDDEN // th),
        in_specs=[pl.BlockSpec((tm, D_MODEL), lambda i, j: (i, 0)),
                  pl.BlockSpec((D_MODEL, th), lambda i, j: (0, j)),
                  pl.BlockSpec((D_MODEL, th), lambda i, j: (0, j)),
                  pl.BlockSpec((th, D_MODEL), lambda i, j: (j, 0)),
                  _full(lw.shape), _full(lb.shape)],
        out_specs=pl.BlockSpec((tm, D_MODEL), lambda i, j: (i, 0)),
        out_shape=jax.ShapeDtypeStruct((t, D_MODEL), F32),
        scratch_shapes=[pltpu.VMEM((tm, D_MODEL), F32)],
        compiler_params=_params("parallel", "arbitrary"),
    )(x2, wg, wu, wd, lw, lb)


def _pad_cols(w, width):
    return jnp.pad(w, ((0, 0), (0, width - w.shape[1])))


def _pad_rows(w, height):
    return jnp.pad(w, ((0, height - w.shape[0]), (0, 0)))


def _rwkv_slab_cols(w, w_vres):
    o = 3 * LIN_WIDTH
    parts = [w[:, :o],
             _pad_cols(w[:, o:o + RWKV_DECAY_RANK], LANES),
             _pad_cols(w[:, o + RWKV_DECAY_RANK:o + RWKV_DECAY_RANK + RWKV_ICLR_RANK], LANES),
             _pad_cols(w[:, o + RWKV_DECAY_RANK + RWKV_ICLR_RANK:], 2 * LANES),
             _pad_cols(w_vres, LANES)]
    return jnp.concatenate(parts, axis=1)


def _rope_tables(s):
    half = HEAD_DIM // 2
    inv = ROPE_THETA ** (-jnp.arange(half, dtype=F32) / half)
    ang = jnp.arange(s, dtype=jnp.int32).astype(F32)[:, None] * inv
    cos, sin = jnp.cos(ang), jnp.sin(ang)
    cos2 = jnp.tile(jnp.concatenate([cos, cos], axis=1), (1, LANES // HEAD_DIM))
    sin2 = jnp.tile(jnp.concatenate([-sin, sin], axis=1), (1, LANES // HEAD_DIM))
    i = np.arange(LANES)
    src = np.where(i % HEAD_DIM < half, i + half, i - half)
    rot = np.zeros((LANES, LANES), np.float32)
    rot[src, i] = 1.0
    return cos2, sin2, jnp.asarray(rot)


def kernel(x, w_in, w_in_vres, gla_w_a2, gla_b_a, gla_ln_w, gla_ln_b, rwkv_mu, rwkv_mu_vres, rwkv_w0, rwkv_w2, rwkv_a0, rwkv_a2, rwkv_v0, rwkv_v2, rwkv_g2, rwkv_k_k, rwkv_k_a, rwkv_r_k, rwkv_ln_w, rwkv_ln_b, nsa_pos_k, nsa_pos_v, nsa_wk1, nsa_wk2, nsa_wv1, nsa_wv2, w_out, ln1_w, ln1_b, ffn_w_gate, ffn_w_up, ffn_w_down, ln2_w, ln2_b):
    bsz, s, d = x.shape
    depth = w_in.shape[0]
    alpha = float((2 * depth) ** 0.25)
    gla_cols = 4 * LIN_WIDTH + GLA_GATE_RANK
    rwkv_cols = 3 * LIN_WIDTH + RWKV_DECAY_RANK + RWKV_ICLR_RANK + RWKV_GATE_RANK
    cos2, sin2, rot = _rope_tables(s)
    row = lambda a: a.reshape(1, -1)

    x2 = x.reshape(bsz * s, d)
    v_first = None
    for l in range(depth):
        w = w_in[l]
        w_r = w[:, gla_cols:gla_cols + rwkv_cols]
        if l == 0:
            w_vres = jnp.zeros((d, RWKV_VRES_RANK), F32)
            mu_vres = jnp.zeros((1, RWKV_VRES_RANK), F32)
        else:
            w_vres = w_in_vres[l - 1]
            mu_vres = row(rwkv_mu_vres[l - 1])
        wg = _pad_cols(w[:, :gla_cols], GLA_SLAB).astype(BF16)
        wr = _rwkv_slab_cols(w_r, w_vres).astype(BF16)
        wn = _pad_cols(w[:, gla_cols + rwkv_cols:], NSA_SLAB).astype(BF16)
        hg, hr, hn = _proj(x2, wg, wr, wn)
        hg = hg.reshape(bsz, s, GLA_SLAB)
        hr = hr.reshape(bsz, s, RWKV_SLAB)
        hn = hn.reshape(bsz, s, NSA_SLAB)

        o_gla = _gla(hg, _pad_rows(gla_w_a2[l], LANES), row(gla_b_a[l]), row(gla_ln_w[l]), row(gla_ln_b[l]))

        mu = _rwkv_slab_cols(row(rwkv_mu[l]), mu_vres)
        small = [mu, row(rwkv_w0[l]), _pad_rows(rwkv_w2[l], LANES), row(rwkv_a0[l]),
                 _pad_rows(rwkv_a2[l], LANES), _pad_rows(rwkv_g2[l], 2 * LANES), row(rwkv_k_k[l]),
                 row(rwkv_k_a[l]), row(rwkv_r_k[l]), row(rwkv_ln_w[l]), row(rwkv_ln_b[l])]
        if l == 0:
            o_rwkv, v_first = _rwkv(hr, None, small)
        else:
            small += [row(rwkv_v0[l - 1]), _pad_rows(rwkv_v2[l - 1], LANES)]
            o_rwkv = _rwkv(hr, v_first, small)

        qr, kv, gt = _nsa_prep(hn, cos2, sin2, rot)
        kcmp, vcmp = _nsa_cmp(kv, nsa_pos_k[l], nsa_pos_v[l], nsa_wk1[l], nsa_wk2[l], nsa_wv1[l], nsa_wv2[l])
        o_nsa = _nsa_attn(qr, kv, gt, kcmp, vcmp)

        x2 = _out_ln(x2, o_gla.reshape(bsz * s, LIN_WIDTH), o_rwkv.reshape(bsz * s, LIN_WIDTH),
                     o_nsa.reshape(bsz * s, NSA_WIDTH), w_out[l].astype(BF16), row(ln1_w[l]), row(ln1_b[l]), alpha)
        x2 = _ffn_ln(x2, ffn_w_gate[l].astype(BF16), ffn_w_up[l].astype(BF16), ffn_w_down[l].astype(BF16),
                     row(ln2_w[l]), row(ln2_b[l]), alpha)
    return x2.reshape(bsz, s, d)
```

```python
import functools

import numpy as np
import jax
import jax.numpy as jnp
from jax import lax
from jax.experimental import pallas as pl
from jax.experimental.pallas import tpu as pltpu

F32 = jnp.float32
BF16 = jnp.bfloat16
HIGHEST = lax.Precision.HIGHEST

D_MODEL = 1024
HEAD_DIM = 64
N_LIN_HEADS = 4
LIN_WIDTH = N_LIN_HEADS * HEAD_DIM
GLA_GATE_RANK = 16
GLA_TAU = 16.0
RWKV_DECAY_RANK = 64
RWKV_ICLR_RANK = 64
RWKV_VRES_RANK = 32
RWKV_GATE_RANK = 160
RWKV_GN_EPS = 64e-5
NSA_HEADS = 8
NSA_KV_HEADS = 2
NSA_HPG = NSA_HEADS // NSA_KV_HEADS
NSA_WIDTH = NSA_HEADS * HEAD_DIM
NSA_KV_WIDTH = NSA_KV_HEADS * HEAD_DIM
NSA_CMP_BLOCK = 32
NSA_CMP_STRIDE = 16
NSA_CMP_HIDDEN = 256
NSA_SEL_BLOCK = 64
NSA_N_SELECT = 16
NSA_WINDOW = 512
ROPE_THETA = 10000.0
FFN_HIDDEN = 2816
LN_EPS = 1e-5
MASK_NEG = -1e30

LANES = 128
CHUNK = 64
SUB = 16
QBLK = 128
SLC_KB = 4
VMEM_LIMIT = 56 * 1024 * 1024

GLA_SLAB = 4 * LIN_WIDTH + LANES
RWKV_SLAB = 3 * LIN_WIDTH + LANES + LANES + 2 * LANES + LANES
NSA_SLAB = NSA_WIDTH + 6 * NSA_KV_WIDTH + LANES
RW_WLR, RW_ALR, RW_GLR, RW_VLR = 768, 896, 1024, 1280


def _dot(a, b, dims=None, hi=False):
    dims = dims or (((a.ndim - 1,), (0,)), ((), ()))
    if hi:
        return lax.dot_general(a, b, dims, precision=HIGHEST, preferred_element_type=F32)
    return lax.dot_general(a.astype(BF16), b.astype(BF16), dims, preferred_element_type=F32)


_NT = (((1,), (1,)), ((), ()))
_TN = (((0,), (0,)), ((), ()))


def _sigmoid(x):
    return 1.0 / (1.0 + jnp.exp(-x))


def _softplus(x):
    return jnp.maximum(x, 0.0) + jnp.log(1.0 + jnp.exp(-jnp.abs(x)))


def _head_norm(x, w, b, eps):
    mu = jnp.mean(x, axis=-1, keepdims=True)
    xc = x - mu
    var = jnp.mean(xc * xc, axis=-1, keepdims=True)
    return xc * lax.rsqrt(var + eps) * w + b


def _layer_norm(x, w, b):
    mu = jnp.mean(x, axis=-1, keepdims=True)
    xc = x - mu
    var = jnp.mean(xc * xc, axis=-1, keepdims=True)
    return xc * lax.rsqrt(var + LN_EPS) * w + b


def _params(*sem):
    return pltpu.CompilerParams(dimension_semantics=sem, vmem_limit_bytes=VMEM_LIMIT)


def _full(shape):
    nd = len(shape)
    return pl.BlockSpec(shape, lambda *_: (0,) * nd)


def _proj_kernel(x_ref, wg_ref, wr_ref, wn_ref, og_ref, or_ref, on_ref):
    xb = x_ref[...].astype(BF16)
    og_ref[...] = jnp.dot(xb, wg_ref[...], preferred_element_type=F32)
    or_ref[...] = jnp.dot(xb, wr_ref[...], preferred_element_type=F32)
    on_ref[...] = jnp.dot(xb, wn_ref[...], preferred_element_type=F32)


def _proj(x2, wg, wr, wn, tm=256):
    t = x2.shape[0]
    return pl.pallas_call(
        _proj_kernel,
        grid=(t // tm,),
        in_specs=[pl.BlockSpec((tm, D_MODEL), lambda i: (i, 0)),
                  _full(wg.shape), _full(wr.shape), _full(wn.shape)],
        out_specs=[pl.BlockSpec((tm, GLA_SLAB), lambda i: (i, 0)),
                   pl.BlockSpec((tm, RWKV_SLAB), lambda i: (i, 0)),
                   pl.BlockSpec((tm, NSA_SLAB), lambda i: (i, 0))],
        out_shape=[jax.ShapeDtypeStruct((t, GLA_SLAB), F32),
                   jax.ShapeDtypeStruct((t, RWKV_SLAB), F32),
                   jax.ShapeDtypeStruct((t, NSA_SLAB), F32)],
        compiler_params=_params("parallel"),
    )(x2, wg, wr, wn)


def _group_rows(x, u):
    return jnp.concatenate(
        [jnp.broadcast_to(x[SUB * i + u:SUB * i + u + 1, :], (SUB, x.shape[1]))
         for i in range(CHUNK // SUB)], axis=0)


def _gla_kernel(h_ref, wa2_ref, ba_ref, lnw_ref, lnb_ref, tril_ref, blk_ref, ones_ref, dm_ref,
                o_ref, st_ref, *, n_chunks):
    @pl.when(pl.program_id(1) == 0)
    def _():
        st_ref[...] = jnp.zeros_like(st_ref)

    row_in_sub = lax.broadcasted_iota(jnp.int32, (CHUNK, LIN_WIDTH), 0) % SUB
    dm = dm_ref[...]

    def chunk(ci, carry):
        r0 = pl.multiple_of(ci * CHUNK, CHUNK)
        p = h_ref[0, pl.ds(r0, CHUNK), :]
        q = p[:, 0:256] * (HEAD_DIM ** -0.5)
        k = p[:, 256:512]
        v = p[:, 512:768]
        g = p[:, 768:1024]
        z = _dot(p[:, 1024:1152], wa2_ref[...], hi=True) + ba_ref[...]
        log_a = -_softplus(-z) * (1.0 / GLA_TAU)
        b = _dot(tril_ref[...], log_a, hi=True)
        b_end = _dot(blk_ref[...], log_a, hi=True)
        qe = q * jnp.exp(b)
        kd = k * jnp.exp(b_end - b)
        p_end = jnp.exp(b_end)

        at = jnp.zeros((CHUNK, LIN_WIDTH), F32)
        for u in range(SUB):
            bt = _group_rows(b, u)
            qt = _group_rows(q, u)
            e = jnp.exp(jnp.where(row_in_sub <= u, bt - b, MASK_NEG)) * (qt * k)
            a_rep = _dot(e, ones_ref[...])
            at = at + jnp.where(dm == u, a_rep, 0.0)

        for h in range(N_LIN_HEADS):
            hs = slice(HEAD_DIM * h, HEAD_DIM * (h + 1))
            o_diag = _dot(at[:, hs], v[:, hs], _TN)
            st = st_ref[h]
            inter = []
            for i in range(CHUNK // SUB):
                rs = slice(SUB * i, SUB * (i + 1))
                inter.append(_dot(qe[rs, hs], st, _NT))
                st = st * p_end[SUB * i:SUB * i + 1, hs] + _dot(v[rs, hs], kd[rs, hs], _TN)
            st_ref[h] = st
            o_h = o_diag + jnp.concatenate(inter, axis=0)
            o_h = _head_norm(o_h, lnw_ref[:, hs], lnb_ref[:, hs], LN_EPS)
            g_h = g[:, hs]
            o_ref[0, pl.ds(r0, CHUNK), hs] = o_h * (g_h * _sigmoid(g_h))
        return carry

    lax.fori_loop(0, n_chunks, chunk, 0)


def _gla(hg, wa2, ba, lnw, lnb, ts=512):
    bsz, s, _ = hg.shape
    r = np.arange(CHUNK)
    tril = ((r[:, None] // SUB == r[None, :] // SUB) & (r[None, :] <= r[:, None])).astype(np.float32)
    blk = (r[:, None] // SUB == r[None, :] // SUB).astype(np.float32)
    c = np.arange(LIN_WIDTH)
    ones = (c[:, None] // HEAD_DIM == c[None, :] // HEAD_DIM).astype(np.float32)
    dm = ((c[None, :] % HEAD_DIM) - SUB * (r[:, None] // SUB)).astype(np.int32)
    consts = [jnp.asarray(tril), jnp.asarray(blk), jnp.asarray(ones, BF16), jnp.asarray(dm)]
    small = [wa2, ba, lnw, lnb] + consts
    return pl.pallas_call(
        functools.partial(_gla_kernel, n_chunks=ts // CHUNK),
        grid=(bsz, s // ts),
        in_specs=[pl.BlockSpec((1, ts, GLA_SLAB), lambda b, j: (b, j, 0))] + [_full(a.shape) for a in small],
        out_specs=pl.BlockSpec((1, ts, LIN_WIDTH), lambda b, j: (b, j, 0)),
        out_shape=jax.ShapeDtypeStruct((bsz, s, LIN_WIDTH), F32),
        scratch_shapes=[pltpu.VMEM((N_LIN_HEADS, HEAD_DIM, HEAD_DIM), F32)],
        compiler_params=_params("parallel", "arbitrary"),
    )(hg, *small)


def _rwkv_kernel(*refs, n_chunks, has_vres):
    if has_vres:
        (h_ref, vf_ref, mu_ref, w0_ref, w2_ref, a0_ref, a2_ref, g2_ref, kk_ref, ka_ref, rk_ref,
         lnw_ref, lnb_ref, v0_ref, v2_ref, tril_ref, ones_ref, same_ref, strict_ref, incl_ref, eye_ref,
         o_ref, st_ref, prev_ref) = refs
    else:
        (h_ref, mu_ref, w0_ref, w2_ref, a0_ref, a2_ref, g2_ref, kk_ref, ka_ref, rk_ref,
         lnw_ref, lnb_ref, tril_ref, ones_ref, same_ref, strict_ref, incl_ref, eye_ref,
         o_ref, vf_ref, st_ref, prev_ref) = refs

    @pl.when(pl.program_id(1) == 0)
    def _():
        st_ref[...] = jnp.zeros_like(st_ref)
        prev_ref[...] = jnp.zeros_like(prev_ref)

    row = lax.broadcasted_iota(jnp.int32, (CHUNK, RWKV_SLAB), 0)
    heads = range(N_LIN_HEADS)

    def tile_rows(m):
        return jnp.concatenate([m] * N_LIN_HEADS, axis=0)

    def own_head(m):
        return tile_rows(m) * same_ref[...]

    def stack(m):
        return jnp.concatenate([m[:, HEAD_DIM * h:HEAD_DIM * (h + 1)] for h in heads], axis=0)

    def unstack(m):
        return jnp.concatenate([m[CHUNK * h:CHUNK * (h + 1), :] for h in heads], axis=1)

    def chunk(ci, carry):
        r0 = pl.multiple_of(ci * CHUNK, CHUNK)
        p = h_ref[0, pl.ds(r0, CHUNK), :]
        prev = jnp.where(row == 0, prev_ref[...], pltpu.roll(p, 1, 0))
        prev_ref[...] = p[CHUNK - 1:CHUNK, :]
        xm = p + (prev - p) * mu_ref[...]
        r = xm[:, 0:256]
        k = xm[:, 256:512]
        v = xm[:, 512:768]
        w_log = -_softplus(-(w0_ref[...] + _dot(jnp.tanh(xm[:, RW_WLR:RW_ALR]), w2_ref[...], hi=True))) - 0.5
        lw = -jnp.exp(w_log)
        a = _sigmoid(a0_ref[...] + _dot(xm[:, RW_ALR:RW_GLR], a2_ref[...], hi=True))
        g = _dot(_sigmoid(xm[:, RW_GLR:RW_VLR]), g2_ref[...], hi=True)
        if has_vres:
            mixv = _sigmoid(v0_ref[...] + _dot(xm[:, RW_VLR:RWKV_SLAB], v2_ref[...], hi=True))
            v = v + (vf_ref[0, pl.ds(r0, CHUNK), :] - v) * mixv
        else:
            vf_ref[0, pl.ds(r0, CHUNK), :] = v
        kkv = k * kk_ref[...]
        nrm = jnp.sqrt(_dot(kkv * kkv, ones_ref[...], hi=True))
        kkn = kkv / jnp.maximum(nrm, 1e-12)
        k2 = k * (1.0 + (a - 1.0) * ka_ref[...])
        av = -kkn
        bv = kkn * a
        c = _dot(tril_ref[...], lw, hi=True)
        c_end = c[CHUNK - 1:CHUNK, :]
        e_c = jnp.exp(c)
        e_nc = jnp.exp(-c)
        e_rest = jnp.exp(c_end - c)

        s0 = st_ref[...]
        ar = jnp.concatenate([own_head(av * jnp.exp(c - lw)), own_head(r * e_c)], axis=0)
        bk = jnp.concatenate([tile_rows(bv * e_nc), tile_rows(k2 * e_nc)], axis=0)
        gm = _dot(ar, bk, _NT)
        n_ab = gm[0:256, 0:256] * strict_ref[...]
        m_ak = gm[0:256, 256:512] * strict_ref[...]
        incl = incl_ref[...]
        r_bk = gm[256:512, :] * jnp.concatenate([incl, incl], axis=1)
        a_s = _dot(ar, s0, _NT)
        v_st = stack(v)
        rhs = a_s[0:256] + _dot(m_ak, v_st)
        inv = eye_ref[...] + n_ab
        pw = n_ab
        for _ in range(5):
            pw = _dot(pw, pw)
            inv = inv + _dot(inv, pw)
        uv = jnp.concatenate([_dot(inv, rhs), v_st], axis=0)
        y = unstack(_head_norm(a_s[256:512] + _dot(r_bk, uv), 1.0, 0.0, RWKV_GN_EPS))
        bk_rest = jnp.concatenate([own_head(bv * e_rest), own_head(k2 * e_rest)], axis=0)
        st_ref[...] = s0 * jnp.exp(c_end) + _dot(uv, bk_rest, _TN)
        bonus = _dot(r * k2 * rk_ref[...], ones_ref[...], hi=True) * v
        o_ref[0, pl.ds(r0, CHUNK), :] = (y * lnw_ref[...] + lnb_ref[...] + bonus) * g
        return carry

    lax.fori_loop(0, n_chunks, chunk, 0, unroll=2)


def _rwkv(hr, v_first, small, ts=512):
    bsz, s, _ = hr.shape
    has_vres = v_first is not None
    r = np.arange(CHUNK)
    tril = (r[None, :] <= r[:, None]).astype(np.float32)
    i = np.arange(LIN_WIDTH)
    same = (i[:, None] // HEAD_DIM == i[None, :] // HEAD_DIM).astype(np.float32)
    t_in, s_in = i[:, None] % CHUNK, i[None, :] % CHUNK
    consts = [tril, same, same, same * (s_in < t_in), same * (s_in <= t_in), np.eye(LIN_WIDTH, dtype=np.float32)]
    small = list(small) + [jnp.asarray(c, F32) for c in consts]
    seq_spec = pl.BlockSpec((1, ts, LIN_WIDTH), lambda b, j: (b, j, 0))
    in_specs = [pl.BlockSpec((1, ts, RWKV_SLAB), lambda b, j: (b, j, 0))]
    args = [hr]
    if has_vres:
        in_specs.append(seq_spec)
        args.append(v_first)
    in_specs += [_full(a.shape) for a in small]
    out_sd = jax.ShapeDtypeStruct((bsz, s, LIN_WIDTH), F32)
    return pl.pallas_call(
        functools.partial(_rwkv_kernel, n_chunks=ts // CHUNK, has_vres=has_vres),
        grid=(bsz, s // ts),
        in_specs=in_specs,
        out_specs=seq_spec if has_vres else [seq_spec, seq_spec],
        out_shape=out_sd if has_vres else [out_sd, out_sd],
        scratch_shapes=[pltpu.VMEM((HEAD_DIM, LIN_WIDTH), F32),
                        pltpu.VMEM((1, RWKV_SLAB), F32)],
        compiler_params=_params("parallel", "arbitrary"),
    )(*args, *small)


def _nsa_prep_kernel(h_ref, cos_ref, sin_ref, rot_ref, q_ref, kvc_ref, kk_ref, vt_ref, gt_ref, *, n_qblk):
    cos = cos_ref[...]
    sin = sin_ref[...]
    rot = rot_ref[...]

    def rope(x):
        return x * cos + _dot(x, rot, hi=True) * sin

    def col(i):
        return h_ref[0, :, NSA_WIDTH + LANES * i:NSA_WIDTH + LANES * (i + 1)]

    for i in range(NSA_WIDTH // LANES):
        cs = slice(LANES * i, LANES * (i + 1))
        q_ref[0, :, cs] = rope(h_ref[0, :, cs]) * (HEAD_DIM ** -0.5)
    kvc_ref[0, :, 0:LANES] = rope(col(0))
    kvc_ref[0, :, LANES:2 * LANES] = col(1)
    kk_ref[0, :, 0:LANES] = rope(col(2)).astype(BF16)
    kk_ref[0, :, LANES:2 * LANES] = rope(col(4)).astype(BF16)
    for i in range(n_qblk):
        rs = slice(QBLK * i, QBLK * (i + 1))
        vt_ref[0, i, 0:LANES, :] = col(3)[rs, :].T.astype(BF16)
        vt_ref[0, i, LANES:2 * LANES, :] = col(5)[rs, :].T.astype(BF16)
    gt_ref[0] = _sigmoid(h_ref[0, :, NSA_WIDTH + 6 * LANES:NSA_SLAB])


def _nsa_prep(hn, cos2, sin2, rot, ts=512):
    bsz, s, _ = hn.shape
    n_qblk = ts // QBLK

    def seq(width):
        return pl.BlockSpec((1, ts, width), lambda b, j: (b, j, 0))

    return pl.pallas_call(
        functools.partial(_nsa_prep_kernel, n_qblk=n_qblk),
        grid=(bsz, s // ts),
        in_specs=[seq(NSA_SLAB),
                  pl.BlockSpec((ts, LANES), lambda b, j: (j, 0)),
                  pl.BlockSpec((ts, LANES), lambda b, j: (j, 0)),
                  _full(rot.shape)],
        out_specs=[seq(NSA_WIDTH), seq(2 * LANES), seq(2 * LANES),
                   pl.BlockSpec((1, n_qblk, 2 * LANES, QBLK), lambda b, j: (b, j, 0, 0)),
                   seq(LANES)],
        out_shape=[jax.ShapeDtypeStruct((bsz, s, NSA_WIDTH), F32),
                   jax.ShapeDtypeStruct((bsz, s, 2 * LANES), F32),
                   jax.ShapeDtypeStruct((bsz, s, 2 * LANES), BF16),
                   jax.ShapeDtypeStruct((bsz, s // QBLK, 2 * LANES, QBLK), BF16),
                   jax.ShapeDtypeStruct((bsz, s, LANES), F32)],
        compiler_params=_params("parallel", "parallel"),
    )(hn, cos2, sin2, rot)


def _nsa_cmp_kernel(kc_ref, vc_ref, pk_ref, pv_ref, wk1_ref, wk2_ref, wv1_ref, wv2_ref, ko_ref, vo_ref,
                    *, n_blk):
    half = NSA_CMP_BLOCK // 2

    def compress(t_ref, pos_ref, w1_ref, w2_ref, o_ref):
        for g in range(NSA_KV_HEADS):
            gs = slice(HEAD_DIM * g, HEAD_DIM * (g + 1))
            top = jnp.zeros((n_blk, NSA_CMP_HIDDEN), F32)
            bot = jnp.zeros((n_blk, NSA_CMP_HIDDEN), F32)
            for l in range(half):
                rows = t_ref[0, pl.ds(l, n_blk, stride=NSA_CMP_STRIDE), :][:, gs]
                top = top + _dot(rows + pos_ref[l:l + 1, :], w1_ref[l], hi=True)
                bot = bot + _dot(rows + pos_ref[half + l:half + l + 1, :], w1_ref[half + l], hi=True)
            hid = top + pltpu.roll(bot, n_blk - 1, 0)
            act = 0.5 * hid * (1.0 + jnp.tanh(0.7978845608028654 * (hid + 0.044715 * hid * hid * hid)))
            o_ref[0, :, gs] = _dot(act, w2_ref[...], hi=True)

    compress(kc_ref, pk_ref, wk1_ref, wk2_ref, ko_ref)
    compress(vc_ref, pv_ref, wv1_ref, wv2_ref, vo_ref)


def _nsa_cmp(kv, pos_k, pos_v, wk1, wk2, wv1, wv2):
    bsz, s, _ = kv.shape
    n_blk = s // NSA_CMP_STRIDE
    wk1 = wk1.reshape(NSA_CMP_BLOCK, HEAD_DIM, NSA_CMP_HIDDEN)
    wv1 = wv1.reshape(NSA_CMP_BLOCK, HEAD_DIM, NSA_CMP_HIDDEN)
    small = [pos_k, pos_v, wk1, wk2, wv1, wv2]
    out_spec = pl.BlockSpec((1, n_blk, LANES), lambda b: (b, 0, 0))
    out_sd = jax.ShapeDtypeStruct((bsz, n_blk, LANES), F32)
    return pl.pallas_call(
        functools.partial(_nsa_cmp_kernel, n_blk=n_blk),
        grid=(bsz,),
        in_specs=[pl.BlockSpec((1, s, LANES), lambda b: (b, 0, 0)),
                  pl.BlockSpec((1, s, LANES), lambda b: (b, 0, 1))] + [_full(a.shape) for a in small],
        out_specs=[out_spec, out_spec],
        out_shape=[out_sd, out_sd],
        compiler_params=_params("parallel"),
    )(kv, kv, *small)


def _nsa_attn_kernel(q_ref, gt_ref, kc_ref, vc_ref, ks_ref, kw_ref, vt_ref, ovt_ref, ext_ref,
                     o_ref, acc_ref, *, n_cmp):
    qi = pl.program_id(1)
    cols = NSA_HPG * QBLK
    n_sel_blk = ovt_ref.shape[0]
    key_off = lax.broadcasted_iota(jnp.int32, (QBLK, cols), 0)
    t_pos = qi * QBLK + (lax.broadcasted_iota(jnp.int32, (QBLK, cols), 1) & (QBLK - 1))
    gt_t = gt_ref[0].T
    zeros64 = jnp.zeros((QBLK, HEAD_DIM), F32)
    m0 = jnp.full((1, cols), MASK_NEG, F32)
    l0 = jnp.zeros((1, cols), F32)

    o_groups = []
    for g in range(NSA_KV_HEADS):
        gs = slice(HEAD_DIM * g, HEAD_DIM * (g + 1))
        pieces = []
        for hp in range(NSA_HPG):
            qh = q_ref[0, :, (g * NSA_HPG + hp) * HEAD_DIM:(g * NSA_HPG + hp + 1) * HEAD_DIM]
            pieces.append(jnp.concatenate([qh, zeros64] if g == 0 else [zeros64, qh], axis=1))
        qs = jnp.concatenate(pieces, axis=0)
        qs_b = qs.astype(BF16)

        s_c = _dot(kc_ref[0], qs, _NT, hi=True)
        valid_c = (key_off * NSA_CMP_STRIDE + (NSA_CMP_BLOCK - 1) <= t_pos) & (key_off < n_cmp)
        s_c = jnp.where(valid_c, s_c, MASK_NEG)
        e_c = jnp.exp(s_c - jnp.max(s_c, axis=0, keepdims=True))
        p_c = jnp.where(valid_c, e_c / jnp.sum(e_c, axis=0, keepdims=True), 0.0)
        o_cmp = _dot(vc_ref[0], p_c, _TN)[gs, :]
        p_sum = p_c[:, 0:QBLK] + p_c[:, QBLK:2 * QBLK] + p_c[:, 2 * QBLK:3 * QBLK] + p_c[:, 3 * QBLK:]
        imp = _dot(ovt_ref[...], p_sum, hi=True)

        j = lax.broadcasted_iota(jnp.int32, (n_sel_blk, QBLK), 0)
        tq = qi * QBLK + lax.broadcasted_iota(jnp.int32, (n_sel_blk, QBLK), 1)
        cur = tq // NSA_SEL_BLOCK
        forced = (j == 0) | (j == cur) | (j == cur - 1)
        score = jnp.where(forced, 1e9, jnp.where(j > cur, -1e9, imp))
        rank = jnp.zeros((n_sel_blk, QBLK), F32)
        for i in range(n_sel_blk):
            s_i = score[i:i + 1, :]
            ahead = (s_i > score) | ((s_i == score) & (j > i))
            rank = rank + jnp.where(ahead, 1.0, 0.0)
        sel = jnp.where(rank < NSA_N_SELECT, 1.0, 0.0)

        def attend(k_ref, v_row0, kb0, n_kb, m, l, bias=None):
            k0 = pl.multiple_of(kb0 * QBLK, QBLK)
            s = lax.dot_general(k_ref[0, pl.ds(k0, n_kb * QBLK), :], qs_b, _NT, preferred_element_type=F32)
            key_pos = k0 + lax.broadcasted_iota(jnp.int32, (n_kb * QBLK, cols), 0)
            t_all = jnp.concatenate([t_pos] * n_kb, axis=0)
            valid = key_pos <= t_all
            if bias is None:
                valid = valid & (key_pos > t_all - NSA_WINDOW)
            else:
                s = s + bias
            s = jnp.where(valid, s, MASK_NEG)
            m_new = jnp.maximum(m, jnp.max(s, axis=0, keepdims=True))
            alpha = jnp.exp(m - m_new)
            p = jnp.exp(s - m_new).astype(BF16)
            pv = jnp.zeros((HEAD_DIM, cols), F32)
            for i in range(n_kb):
                vt_blk = vt_ref[0, kb0 + i, v_row0:v_row0 + HEAD_DIM, :]
                pv = pv + jnp.dot(vt_blk, p[QBLK * i:QBLK * (i + 1), :], preferred_element_type=F32)
            acc_ref[...] = acc_ref[...] * alpha + pv
            return m_new, l * alpha + jnp.sum(p.astype(F32), axis=0, keepdims=True)

        acc_ref[...] = jnp.zeros_like(acc_ref)

        def slc_step(i, ml):
            chosen = _dot(ext_ref[i], sel)
            bias = jnp.concatenate([jnp.where(chosen > 0.5, 0.0, MASK_NEG)] * NSA_HPG, axis=1)
            return attend(ks_ref, HEAD_DIM * g, i * SLC_KB, SLC_KB, *ml, bias=bias)

        ml = lax.fori_loop(0, qi // SLC_KB + 1, slc_step, (m0, l0))
        o_slc = acc_ref[...] / ml[1]

        acc_ref[...] = jnp.zeros_like(acc_ref)
        n_win = NSA_WINDOW // QBLK + 1
        ml = attend(kw_ref, LANES + HEAD_DIM * g, jnp.maximum(qi - (n_win - 1), 0), n_win, m0, l0)
        o_win = acc_ref[...] / ml[1]

        def gate(branch):
            return jnp.concatenate(
                [gt_t[(g * NSA_HPG + hp) * 3 + branch:(g * NSA_HPG + hp) * 3 + branch + 1, :]
                 for hp in range(NSA_HPG)], axis=1)

        o_groups.append(gate(0) * o_cmp + gate(1) * o_slc + gate(2) * o_win)

    o_all = jnp.concatenate(o_groups, axis=0)
    for hp in range(NSA_HPG):
        tile = o_all[:, QBLK * hp:QBLK * (hp + 1)].T
        for g in range(NSA_KV_HEADS):
            c0 = (g * NSA_HPG + hp) * HEAD_DIM
            o_ref[0, :, c0:c0 + HEAD_DIM] = tile[:, HEAD_DIM * g:HEAD_DIM * (g + 1)]


def _nsa_attn(qr, kk, vt, gt, kcmp, vcmp):
    bsz, s, _ = qr.shape
    n_blk = s // NSA_CMP_STRIDE
    n_cmp = n_blk - NSA_CMP_BLOCK // NSA_CMP_STRIDE + 1
    n_sel = s // NSA_SEL_BLOCK
    assert n_blk == QBLK, "compressed-score tiles are laid out one stride block per sublane row"
    c0 = np.arange(n_blk) * NSA_CMP_STRIDE
    s0 = np.arange(n_sel) * NSA_SEL_BLOCK
    lo = np.maximum(c0[:, None], s0[None, :])
    hi = np.minimum(c0[:, None] + NSA_CMP_BLOCK, s0[None, :] + NSA_SEL_BLOCK)
    overlap = (np.maximum(hi - lo, 0) / NSA_CMP_STRIDE).astype(np.float32)
    overlap[n_cmp:] = 0.0
    key = np.arange(s)
    expand = (key[:, None] // NSA_SEL_BLOCK == np.arange(n_sel)[None, :]).astype(np.float32)
    expand = expand.reshape(s // (SLC_KB * QBLK), SLC_KB * QBLK, n_sel)
    consts = [jnp.asarray(overlap.T), jnp.asarray(expand, BF16)]

    cmp_spec = pl.BlockSpec((1, n_blk, LANES), lambda b, i: (b, 0, 0))
    return pl.pallas_call(
        functools.partial(_nsa_attn_kernel, n_cmp=n_cmp),
        grid=(bsz, s // QBLK),
        in_specs=[pl.BlockSpec((1, QBLK, NSA_WIDTH), lambda b, i: (b, i, 0)),
                  pl.BlockSpec((1, QBLK, LANES), lambda b, i: (b, i, 0)),
                  cmp_spec, cmp_spec,
                  pl.BlockSpec((1, s, LANES), lambda b, i: (b, 0, 0)),
                  pl.BlockSpec((1, s, LANES), lambda b, i: (b, 0, 1)),
                  pl.BlockSpec((1, s // QBLK, 2 * LANES, QBLK), lambda b, i: (b, 0, 0, 0))]
        + [_full(a.shape) for a in consts],
        out_specs=pl.BlockSpec((1, QBLK, NSA_WIDTH), lambda b, i: (b, i, 0)),
        out_shape=jax.ShapeDtypeStruct((bsz, s, NSA_WIDTH), F32),
        scratch_shapes=[pltpu.VMEM((HEAD_DIM, NSA_HPG * QBLK), F32)],
        compiler_params=_params("parallel", "arbitrary"),
    )(qr, gt, kcmp, vcmp, kk, kk, vt, *consts)


def _out_ln_kernel(x_ref, og_ref, or_ref, on_ref, w_ref, lw_ref, lb_ref, o_ref, *, alpha):
    mix = (_dot(og_ref[...], w_ref[0:LIN_WIDTH, :])
           + _dot(or_ref[...], w_ref[LIN_WIDTH:2 * LIN_WIDTH, :])
           + _dot(on_ref[...], w_ref[2 * LIN_WIDTH:, :]))
    o_ref[...] = _layer_norm(alpha * x_ref[...] + mix, lw_ref[...], lb_ref[...])


def _out_ln(x2, og, orw, on, w, lw, lb, alpha, tm=512):
    t = x2.shape[0]

    def rows(width):
        return pl.BlockSpec((tm, width), lambda i: (i, 0))

    return pl.pallas_call(
        functools.partial(_out_ln_kernel, alpha=alpha),
        grid=(t // tm,),
        in_specs=[rows(D_MODEL), rows(LIN_WIDTH), rows(LIN_WIDTH), rows(NSA_WIDTH),
                  _full(w.shape), _full(lw.shape), _full(lb.shape)],
        out_specs=rows(D_MODEL),
        out_shape=jax.ShapeDtypeStruct((t, D_MODEL), F32),
        compiler_params=_params("parallel"),
    )(x2, og, orw, on, w, lw, lb)


def _ffn_ln_kernel(x_ref, wg_ref, wu_ref, wd_ref, lw_ref, lb_ref, o_ref, acc_ref, *, alpha):
    j = pl.program_id(1)
    x = x_ref[...]
    xb = x.astype(BF16)
    gate = jnp.dot(xb, wg_ref[...], preferred_element_type=F32)
    up = jnp.dot(xb, wu_ref[...], preferred_element_type=F32)
    part = _dot(gate * _sigmoid(gate) * up, wd_ref[...])

    @pl.when(j == 0)
    def _():
        acc_ref[...] = part

    @pl.when(j > 0)
    def _():
        acc_ref[...] += part

    @pl.when(j == pl.num_programs(1) - 1)
    def _():
        o_ref[...] = _layer_norm(alpha * x + acc_ref[...], lw_ref[...], lb_ref[...])


def _ffn_ln(x2, wg, wu, wd, lw, lb, alpha, tm=512, th=1408):
    t = x2.shape[0]
    return pl.pallas_call(
        functools.partial(_ffn_ln_kernel, alpha=alpha),
        grid=(t // tm, FFN_HIDDEN // th),
        in_specs=[pl.BlockSpec((tm, D_MODEL), lambda i, j: (i, 0)),
                  pl.BlockSpec((D_MODEL, th), lambda i, j: (0, j)),
                  pl.BlockSpec((D_MODEL, th), lambda i, j: (0, j)),
                  pl.BlockSpec((th, D_MODEL), lambda i, j: (j, 0)),
                  _full(lw.shape), _full(lb.shape)],
        out_specs=pl.BlockSpec((tm, D_MODEL), lambda i, j: (i, 0)),
        out_shape=jax.ShapeDtypeStruct((t, D_MODEL), F32),
        scratch_shapes=[pltpu.VMEM((tm, D_MODEL), F32)],
        compiler_params=_params("parallel", "arbitrary"),
    )(x2, wg, wu, wd, lw, lb)


def _pad_cols(w, width):
    return jnp.pad(w, ((0, 0), (0, width - w.shape[1])))


def _pad_rows(w, height):
    return jnp.pad(w, ((0, height - w.shape[0]), (0, 0)))


def _rwkv_slab_cols(w, w_vres):
    o = 3 * LIN_WIDTH
    parts = [w[:, :o],
             _pad_cols(w[:, o:o + RWKV_DECAY_RANK], LANES),
             _pad_cols(w[:, o + RWKV_DECAY_RANK:o + RWKV_DECAY_RANK + RWKV_ICLR_RANK], LANES),
             _pad_cols(w[:, o + RWKV_DECAY_RANK + RWKV_ICLR_RANK:], 2 * LANES),
             _pad_cols(w_vres, LANES)]
    return jnp.concatenate(parts, axis=1)


def _rope_tables(s):
    half = HEAD_DIM // 2
    inv = ROPE_THETA ** (-jnp.arange(half, dtype=F32) / half)
    ang = jnp.arange(s, dtype=jnp.int32).astype(F32)[:, None] * inv
    cos, sin = jnp.cos(ang), jnp.sin(ang)
    cos2 = jnp.tile(jnp.concatenate([cos, cos], axis=1), (1, LANES // HEAD_DIM))
    sin2 = jnp.tile(jnp.concatenate([-sin, sin], axis=1), (1, LANES // HEAD_DIM))
    i = np.arange(LANES)
    src = np.where(i % HEAD_DIM < half, i + half, i - half)
    rot = np.zeros((LANES, LANES), np.float32)
    rot[src, i] = 1.0
    return cos2, sin2, jnp.asarray(rot)


def kernel(x, w_in, w_in_vres, gla_w_a2, gla_b_a, gla_ln_w, gla_ln_b, rwkv_mu, rwkv_mu_vres, rwkv_w0, rwkv_w2, rwkv_a0, rwkv_a2, rwkv_v0, rwkv_v2, rwkv_g2, rwkv_k_k, rwkv_k_a, rwkv_r_k, rwkv_ln_w, rwkv_ln_b, nsa_pos_k, nsa_pos_v, nsa_wk1, nsa_wk2, nsa_wv1, nsa_wv2, w_out, ln1_w, ln1_b, ffn_w_gate, ffn_w_up, ffn_w_down, ln2_w, ln2_b):
    bsz, s, d = x.shape
    depth = w_in.shape[0]
    alpha = float((2 * depth) ** 0.25)
    gla_cols = 4 * LIN_WIDTH + GLA_GATE_RANK
    rwkv_cols = 3 * LIN_WIDTH + RWKV_DECAY_RANK + RWKV_ICLR_RANK + RWKV_GATE_RANK
    cos2, sin2, rot = _rope_tables(s)
    row = lambda a: a.reshape(1, -1)

    x2 = x.reshape(bsz * s, d)
    v_first = None
    for l in range(depth):
        w = w_in[l]
        w_r = w[:, gla_cols:gla_cols + rwkv_cols]
        if l == 0:
            w_vres = jnp.zeros((d, RWKV_VRES_RANK), F32)
            mu_vres = jnp.zeros((1, RWKV_VRES_RANK), F32)
        else:
            w_vres = w_in_vres[l - 1]
            mu_vres = row(rwkv_mu_vres[l - 1])
        wg = _pad_cols(w[:, :gla_cols], GLA_SLAB).astype(BF16)
        wr = _rwkv_slab_cols(w_r, w_vres).astype(BF16)
        wn = _pad_cols(w[:, gla_cols + rwkv_cols:], NSA_SLAB).astype(BF16)
        hg, hr, hn = _proj(x2, wg, wr, wn)
        hg = hg.reshape(bsz, s, GLA_SLAB)
        hr = hr.reshape(bsz, s, RWKV_SLAB)
        hn = hn.reshape(bsz, s, NSA_SLAB)

        o_gla = _gla(hg, _pad_rows(gla_w_a2[l], LANES), row(gla_b_a[l]), row(gla_ln_w[l]), row(gla_ln_b[l]))

        mu = _rwkv_slab_cols(row(rwkv_mu[l]), mu_vres)
        small = [mu, row(rwkv_w0[l]), _pad_rows(rwkv_w2[l], LANES), row(rwkv_a0[l]),
                 _pad_rows(rwkv_a2[l], LANES), _pad_rows(rwkv_g2[l], 2 * LANES), row(rwkv_k_k[l]),
                 row(rwkv_k_a[l]), row(rwkv_r_k[l]), row(rwkv_ln_w[l]), row(rwkv_ln_b[l])]
        if l == 0:
            o_rwkv, v_first = _rwkv(hr, None, small)
        else:
            small += [row(rwkv_v0[l - 1]), _pad_rows(rwkv_v2[l - 1], LANES)]
            o_rwkv = _rwkv(hr, v_first, small)

        qr, kvc, kk, vt, gt = _nsa_prep(hn, cos2, sin2, rot)
        kcmp, vcmp = _nsa_cmp(kvc, nsa_pos_k[l], nsa_pos_v[l], nsa_wk1[l], nsa_wk2[l], nsa_wv1[l], nsa_wv2[l])
        o_nsa = _nsa_attn(qr, kk, vt, gt, kcmp, vcmp)

        x2 = _out_ln(x2, o_gla.reshape(bsz * s, LIN_WIDTH), o_rwkv.reshape(bsz * s, LIN_WIDTH),
                     o_nsa.reshape(bsz * s, NSA_WIDTH), w_out[l].astype(BF16), row(ln1_w[l]), row(ln1_b[l]), alpha)
        x2 = _ffn_ln(x2, ffn_w_gate[l].astype(BF16), ffn_w_up[l].astype(BF16), ffn_w_down[l].astype(BF16),
                     row(ln2_w[l]), row(ln2_b[l]), alpha)
    return x2.reshape(bsz, s, d)
```

```python
import functools

import numpy as np
import jax
import jax.numpy as jnp
from jax import lax
from jax.experimental import pallas as pl
from jax.experimental.pallas import tpu as pltpu

F32 = jnp.float32
BF16 = jnp.bfloat16
HIGHEST = lax.Precision.HIGHEST

D_MODEL = 1024
HEAD_DIM = 64
N_LIN_HEADS = 4
LIN_WIDTH = N_LIN_HEADS * HEAD_DIM
GLA_GATE_RANK = 16
GLA_TAU = 16.0
RWKV_DECAY_RANK = 64
RWKV_ICLR_RANK = 64
RWKV_VRES_RANK = 32
RWKV_GATE_RANK = 160
RWKV_GN_EPS = 64e-5
NSA_HEADS = 8
NSA_KV_HEADS = 2
NSA_HPG = NSA_HEADS // NSA_KV_HEADS
NSA_WIDTH = NSA_HEADS * HEAD_DIM
NSA_KV_WIDTH = NSA_KV_HEADS * HEAD_DIM
NSA_CMP_BLOCK = 32
NSA_CMP_STRIDE = 16
NSA_CMP_HIDDEN = 256
NSA_SEL_BLOCK = 64
NSA_N_SELECT = 16
NSA_WINDOW = 512
ROPE_THETA = 10000.0
FFN_HIDDEN = 2816
LN_EPS = 1e-5
MASK_NEG = -1e30

LANES = 128
CHUNK = 64
SUB = 16
QBLK = 128
SLC_KB = 4
VT_ROWS = HEAD_DIM + 16
LOG2E = 1.4426950408889634
VMEM_LIMIT = 56 * 1024 * 1024

GLA_SLAB = 4 * LIN_WIDTH + LANES
RWKV_SLAB = 3 * LIN_WIDTH + LANES + LANES + 2 * LANES + LANES
NSA_SLAB = NSA_WIDTH + 6 * NSA_KV_WIDTH + LANES
RW_WLR, RW_ALR, RW_GLR, RW_VLR = 768, 896, 1024, 1280


def _dot(a, b, dims=None, hi=False):
    dims = dims or (((a.ndim - 1,), (0,)), ((), ()))
    if hi:
        return lax.dot_general(a, b, dims, precision=HIGHEST, preferred_element_type=F32)
    return lax.dot_general(a.astype(BF16), b.astype(BF16), dims, preferred_element_type=F32)


_NT = (((1,), (1,)), ((), ()))
_TN = (((0,), (0,)), ((), ()))


def _sigmoid(x):
    return 1.0 / (1.0 + jnp.exp(-x))


def _softplus(x):
    return jnp.maximum(x, 0.0) + jnp.log(1.0 + jnp.exp(-jnp.abs(x)))


def _head_norm(x, w, b, eps):
    mu = jnp.mean(x, axis=-1, keepdims=True)
    xc = x - mu
    var = jnp.mean(xc * xc, axis=-1, keepdims=True)
    return xc * lax.rsqrt(var + eps) * w + b


def _layer_norm(x, w, b):
    mu = jnp.mean(x, axis=-1, keepdims=True)
    xc = x - mu
    var = jnp.mean(xc * xc, axis=-1, keepdims=True)
    return xc * lax.rsqrt(var + LN_EPS) * w + b


def _params(*sem):
    return pltpu.CompilerParams(dimension_semantics=sem, vmem_limit_bytes=VMEM_LIMIT)


def _full(shape):
    nd = len(shape)
    return pl.BlockSpec(shape, lambda *_: (0,) * nd)


def _proj_kernel(x_ref, wg_ref, wr_ref, wn_ref, og_ref, or_ref, on_ref):
    xb = x_ref[...].astype(BF16)
    og_ref[...] = jnp.dot(xb, wg_ref[...], preferred_element_type=F32)
    or_ref[...] = jnp.dot(xb, wr_ref[...], preferred_element_type=F32)
    on_ref[...] = jnp.dot(xb, wn_ref[...], preferred_element_type=F32)


def _proj(x2, wg, wr, wn, tm=256):
    t = x2.shape[0]
    return pl.pallas_call(
        _proj_kernel,
        grid=(t // tm,),
        in_specs=[pl.BlockSpec((tm, D_MODEL), lambda i: (i, 0)),
                  _full(wg.shape), _full(wr.shape), _full(wn.shape)],
        out_specs=[pl.BlockSpec((tm, GLA_SLAB), lambda i: (i, 0)),
                   pl.BlockSpec((tm, RWKV_SLAB), lambda i: (i, 0)),
                   pl.BlockSpec((tm, NSA_SLAB), lambda i: (i, 0))],
        out_shape=[jax.ShapeDtypeStruct((t, GLA_SLAB), F32),
                   jax.ShapeDtypeStruct((t, RWKV_SLAB), F32),
                   jax.ShapeDtypeStruct((t, NSA_SLAB), F32)],
        compiler_params=_params("parallel"),
    )(x2, wg, wr, wn)


def _group_rows(x, u):
    return jnp.concatenate(
        [jnp.broadcast_to(x[SUB * i + u:SUB * i + u + 1, :], (SUB, x.shape[1]))
         for i in range(CHUNK // SUB)], axis=0)


def _gla_kernel(h_ref, wa2_ref, ba_ref, lnw_ref, lnb_ref, tril_ref, blk_ref, ones_ref, dm_ref, same_ref,
                same_sub_ref, o_ref, st_ref, *, n_chunks):
    @pl.when(pl.program_id(1) == 0)
    def _():
        st_ref[...] = jnp.zeros_like(st_ref)

    row_in_sub = lax.broadcasted_iota(jnp.int32, (CHUNK, LIN_WIDTH), 0) % SUB
    dm = dm_ref[...]
    heads = range(N_LIN_HEADS)
    n_sub = CHUNK // SUB

    def own_head(m, mask_ref):
        return jnp.concatenate([m] * N_LIN_HEADS, axis=0) * mask_ref[...]

    def chunk(ci, carry):
        r0 = pl.multiple_of(ci * CHUNK, CHUNK)
        p = h_ref[0, pl.ds(r0, CHUNK), :]
        q = p[:, 0:256] * (HEAD_DIM ** -0.5)
        k = p[:, 256:512]
        v = p[:, 512:768]
        g = p[:, 768:1024]
        z = _dot(p[:, 1024:1152], wa2_ref[...], hi=True) + ba_ref[...]
        log_a = -_softplus(-z) * (1.0 / GLA_TAU)
        b = _dot(tril_ref[...], log_a, hi=True)
        b_end = _dot(blk_ref[...], log_a, hi=True)
        qe = q * jnp.exp(b)
        kd = k * jnp.exp(b_end - b)
        p_end = jnp.exp(b_end)
        v_st = jnp.concatenate([v[:, HEAD_DIM * h:HEAD_DIM * (h + 1)] for h in heads], axis=0)

        at = jnp.zeros((CHUNK, LIN_WIDTH), F32)
        for u0 in range(0, SUB, N_LIN_HEADS):
            es = []
            for u in range(u0, u0 + N_LIN_HEADS):
                bt = _group_rows(b, u)
                qt = _group_rows(q, u)
                es.append(jnp.exp(jnp.where(row_in_sub <= u, bt - b, MASK_NEG)) * (qt * k))
            a_rep = _dot(jnp.concatenate(es, axis=0), ones_ref[...])
            for i, u in enumerate(range(u0, u0 + N_LIN_HEADS)):
                at = at + jnp.where(dm == u, a_rep[CHUNK * i:CHUNK * (i + 1), :], 0.0)
        o_st = _dot(own_head(at, same_ref), v_st, _TN)

        st = st_ref[...]
        inter = []
        for i in range(n_sub):
            rs = slice(SUB * i, SUB * (i + 1))
            inter.append(_dot(own_head(qe[rs], same_sub_ref), st, _NT))
            v_sub = jnp.concatenate([v_st[CHUNK * h + SUB * i:CHUNK * h + SUB * (i + 1), :] for h in heads], axis=0)
            st = st * p_end[SUB * i:SUB * i + 1, :] + _dot(v_sub, own_head(kd[rs], same_sub_ref), _TN)
        st_ref[...] = st
        o_st = o_st + jnp.concatenate(
            [inter[i][SUB * h:SUB * (h + 1), :] for h in heads for i in range(n_sub)], axis=0)
        o_st = _head_norm(o_st, 1.0, 0.0, LN_EPS)
        o = jnp.concatenate([o_st[CHUNK * h:CHUNK * (h + 1), :] for h in heads], axis=1)
        o_ref[0, pl.ds(r0, CHUNK), :] = (o * lnw_ref[...] + lnb_ref[...]) * (g * _sigmoid(g))
        return carry

    lax.fori_loop(0, n_chunks, chunk, 0)


def _gla(hg, wa2, ba, lnw, lnb, ts=512):
    bsz, s, _ = hg.shape
    r = np.arange(CHUNK)
    tril = ((r[:, None] // SUB == r[None, :] // SUB) & (r[None, :] <= r[:, None])).astype(np.float32)
    blk = (r[:, None] // SUB == r[None, :] // SUB).astype(np.float32)
    c = np.arange(LIN_WIDTH)
    same = (c[:, None] // HEAD_DIM == c[None, :] // HEAD_DIM).astype(np.float32)
    same_sub = (r[:, None] // SUB == c[None, :] // HEAD_DIM).astype(np.float32)
    dm = ((c[None, :] % HEAD_DIM) - SUB * (r[:, None] // SUB)).astype(np.int32)
    consts = [jnp.asarray(tril), jnp.asarray(blk), jnp.asarray(same, BF16), jnp.asarray(dm), jnp.asarray(same),
              jnp.asarray(same_sub)]
    small = [wa2, ba, lnw, lnb] + consts
    return pl.pallas_call(
        functools.partial(_gla_kernel, n_chunks=ts // CHUNK),
        grid=(bsz, s // ts),
        in_specs=[pl.BlockSpec((1, ts, GLA_SLAB), lambda b, j: (b, j, 0))] + [_full(a.shape) for a in small],
        out_specs=pl.BlockSpec((1, ts, LIN_WIDTH), lambda b, j: (b, j, 0)),
        out_shape=jax.ShapeDtypeStruct((bsz, s, LIN_WIDTH), F32),
        scratch_shapes=[pltpu.VMEM((HEAD_DIM, LIN_WIDTH), F32)],
        compiler_params=_params("parallel", "arbitrary"),
    )(hg, *small)


def _rwkv_kernel(*refs, n_chunks, has_vres):
    if has_vres:
        (h_ref, vf_ref, mu_ref, w0_ref, w2_ref, a0_ref, a2_ref, g2_ref, kk_ref, ka_ref, rk_ref,
         lnw_ref, lnb_ref, v0_ref, v2_ref, tril_ref, ones_ref, same_ref, strict_ref, incl_ref, eye_ref,
         o_ref, st_ref, prev_ref) = refs
    else:
        (h_ref, mu_ref, w0_ref, w2_ref, a0_ref, a2_ref, g2_ref, kk_ref, ka_ref, rk_ref,
         lnw_ref, lnb_ref, tril_ref, ones_ref, same_ref, strict_ref, incl_ref, eye_ref,
         o_ref, vf_ref, st_ref, prev_ref) = refs

    @pl.when(pl.program_id(1) == 0)
    def _():
        st_ref[...] = jnp.zeros_like(st_ref)
        prev_ref[...] = jnp.zeros_like(prev_ref)

    row = lax.broadcasted_iota(jnp.int32, (CHUNK, RWKV_SLAB), 0)
    heads = range(N_LIN_HEADS)

    def tile_rows(m):
        return jnp.concatenate([m] * N_LIN_HEADS, axis=0)

    def own_head(m):
        return tile_rows(m) * same_ref[...]

    def stack(m):
        return jnp.concatenate([m[:, HEAD_DIM * h:HEAD_DIM * (h + 1)] for h in heads], axis=0)

    def unstack(m):
        return jnp.concatenate([m[CHUNK * h:CHUNK * (h + 1), :] for h in heads], axis=1)

    def chunk(ci, carry):
        r0 = pl.multiple_of(ci * CHUNK, CHUNK)
        p = h_ref[0, pl.ds(r0, CHUNK), :]
        prev = jnp.where(row == 0, prev_ref[...], pltpu.roll(p, 1, 0))
        prev_ref[...] = p[CHUNK - 1:CHUNK, :]
        xm = p + (prev - p) * mu_ref[...]
        r = xm[:, 0:256]
        k = xm[:, 256:512]
        v = xm[:, 512:768]
        w_log = -_softplus(-(w0_ref[...] + _dot(jnp.tanh(xm[:, RW_WLR:RW_ALR]), w2_ref[...], hi=True))) - 0.5
        lw = -jnp.exp(w_log)
        a = _sigmoid(a0_ref[...] + _dot(xm[:, RW_ALR:RW_GLR], a2_ref[...], hi=True))
        g = _dot(_sigmoid(xm[:, RW_GLR:RW_VLR]), g2_ref[...], hi=True)
        if has_vres:
            mixv = _sigmoid(v0_ref[...] + _dot(xm[:, RW_VLR:RWKV_SLAB], v2_ref[...], hi=True))
            v = v + (vf_ref[0, pl.ds(r0, CHUNK), :] - v) * mixv
        else:
            vf_ref[0, pl.ds(r0, CHUNK), :] = v
        kkv = k * kk_ref[...]
        nrm = jnp.sqrt(_dot(kkv * kkv, ones_ref[...], hi=True))
        kkn = kkv / jnp.maximum(nrm, 1e-12)
        k2 = k * (1.0 + (a - 1.0) * ka_ref[...])
        av = -kkn
        bv = kkn * a
        c = _dot(tril_ref[...], lw, hi=True)
        c_end = c[CHUNK - 1:CHUNK, :]
        e_c = jnp.exp(c)
        e_nc = jnp.exp(-c)
        e_rest = jnp.exp(c_end - c)

        s0 = st_ref[...]
        ar = jnp.concatenate([own_head(av * jnp.exp(c - lw)), own_head(r * e_c)], axis=0)
        bk = jnp.concatenate([tile_rows(bv * e_nc), tile_rows(k2 * e_nc)], axis=0)
        gm = _dot(ar, bk, _NT)
        n_ab = gm[0:256, 0:256] * strict_ref[...]
        m_ak = gm[0:256, 256:512] * strict_ref[...]
        incl = incl_ref[...]
        r_bk = gm[256:512, :] * jnp.concatenate([incl, incl], axis=1)
        a_s = _dot(ar, s0, _NT)
        v_st = stack(v)
        rhs = a_s[0:256] + _dot(m_ak, v_st)
        inv = eye_ref[...] + n_ab
        pw = n_ab
        for _ in range(5):
            pw = _dot(pw, pw)
            inv = inv + _dot(inv, pw)
        uv = jnp.concatenate([_dot(inv, rhs), v_st], axis=0)
        y = unstack(_head_norm(a_s[256:512] + _dot(r_bk, uv), 1.0, 0.0, RWKV_GN_EPS))
        bk_rest = jnp.concatenate([own_head(bv * e_rest), own_head(k2 * e_rest)], axis=0)
        st_ref[...] = s0 * jnp.exp(c_end) + _dot(uv, bk_rest, _TN)
        bonus = _dot(r * k2 * rk_ref[...], ones_ref[...], hi=True) * v
        o_ref[0, pl.ds(r0, CHUNK), :] = (y * lnw_ref[...] + lnb_ref[...] + bonus) * g
        return carry

    lax.fori_loop(0, n_chunks, chunk, 0, unroll=2)


def _rwkv(hr, v_first, small, ts=512):
    bsz, s, _ = hr.shape
    has_vres = v_first is not None
    r = np.arange(CHUNK)
    tril = (r[None, :] <= r[:, None]).astype(np.float32)
    i = np.arange(LIN_WIDTH)
    same = (i[:, None] // HEAD_DIM == i[None, :] // HEAD_DIM).astype(np.float32)
    t_in, s_in = i[:, None] % CHUNK, i[None, :] % CHUNK
    consts = [tril, same, same, same * (s_in < t_in), same * (s_in <= t_in), np.eye(LIN_WIDTH, dtype=np.float32)]
    small = list(small) + [jnp.asarray(c, F32) for c in consts]
    seq_spec = pl.BlockSpec((1, ts, LIN_WIDTH), lambda b, j: (b, j, 0))
    in_specs = [pl.BlockSpec((1, ts, RWKV_SLAB), lambda b, j: (b, j, 0))]
    args = [hr]
    if has_vres:
        in_specs.append(seq_spec)
        args.append(v_first)
    in_specs += [_full(a.shape) for a in small]
    out_sd = jax.ShapeDtypeStruct((bsz, s, LIN_WIDTH), F32)
    return pl.pallas_call(
        functools.partial(_rwkv_kernel, n_chunks=ts // CHUNK, has_vres=has_vres),
        grid=(bsz, s // ts),
        in_specs=in_specs,
        out_specs=seq_spec if has_vres else [seq_spec, seq_spec],
        out_shape=out_sd if has_vres else [out_sd, out_sd],
        scratch_shapes=[pltpu.VMEM((HEAD_DIM, LIN_WIDTH), F32),
                        pltpu.VMEM((1, RWKV_SLAB), F32)],
        compiler_params=_params("parallel", "arbitrary"),
    )(*args, *small)


def _nsa_prep_kernel(h_ref, cos_ref, sin_ref, rot_ref, q_ref, kvc_ref, kk_ref, vt_ref, gt_ref, *, n_qblk):
    cos = cos_ref[...]
    sin = sin_ref[...]
    rot = rot_ref[...]

    def rope(x):
        return x * cos + _dot(x, rot, hi=True) * sin

    def col(i):
        return h_ref[0, :, NSA_WIDTH + LANES * i:NSA_WIDTH + LANES * (i + 1)]

    for i in range(NSA_WIDTH // LANES):
        cs = slice(LANES * i, LANES * (i + 1))
        q_ref[0, :, cs] = rope(h_ref[0, :, cs]) * (HEAD_DIM ** -0.5)
    kvc_ref[0, :, 0:LANES] = rope(col(0))
    kvc_ref[0, :, LANES:2 * LANES] = col(1)
    kk_ref[0, :, 0:LANES] = rope(col(2)).astype(BF16)
    kk_ref[0, :, LANES:2 * LANES] = rope(col(4)).astype(BF16)
    pad = VT_ROWS - HEAD_DIM
    ones_rows = jnp.where(lax.broadcasted_iota(jnp.int32, (pad, QBLK), 0) == 0, 1.0, 0.0).astype(BF16)
    for i in range(n_qblk):
        rs = slice(QBLK * i, QBLK * (i + 1))
        for branch, c in enumerate((3, 5)):
            v_t = col(c)[rs, :].T.astype(BF16)
            for g in range(NSA_KV_HEADS):
                r0 = (branch * NSA_KV_HEADS + g) * VT_ROWS
                vt_ref[0, i, r0:r0 + HEAD_DIM, :] = v_t[HEAD_DIM * g:HEAD_DIM * (g + 1), :]
                vt_ref[0, i, r0 + HEAD_DIM:r0 + VT_ROWS, :] = ones_rows
    gt_ref[0] = _sigmoid(h_ref[0, :, NSA_WIDTH + 6 * LANES:NSA_SLAB])


def _nsa_prep(hn, cos2, sin2, rot, ts=512):
    bsz, s, _ = hn.shape
    n_qblk = ts // QBLK

    def seq(width):
        return pl.BlockSpec((1, ts, width), lambda b, j: (b, j, 0))

    return pl.pallas_call(
        functools.partial(_nsa_prep_kernel, n_qblk=n_qblk),
        grid=(bsz, s // ts),
        in_specs=[seq(NSA_SLAB),
                  pl.BlockSpec((ts, LANES), lambda b, j: (j, 0)),
                  pl.BlockSpec((ts, LANES), lambda b, j: (j, 0)),
                  _full(rot.shape)],
        out_specs=[seq(NSA_WIDTH), seq(2 * LANES), seq(2 * LANES),
                   pl.BlockSpec((1, n_qblk, 4 * VT_ROWS, QBLK), lambda b, j: (b, j, 0, 0)),
                   seq(LANES)],
        out_shape=[jax.ShapeDtypeStruct((bsz, s, NSA_WIDTH), F32),
                   jax.ShapeDtypeStruct((bsz, s, 2 * LANES), F32),
                   jax.ShapeDtypeStruct((bsz, s, 2 * LANES), BF16),
                   jax.ShapeDtypeStruct((bsz, s // QBLK, 4 * VT_ROWS, QBLK), BF16),
                   jax.ShapeDtypeStruct((bsz, s, LANES), F32)],
        compiler_params=_params("parallel", "parallel"),
    )(hn, cos2, sin2, rot)


def _nsa_cmp_kernel(kc_ref, vc_ref, pk_ref, pv_ref, wk1_ref, wk2_ref, wv1_ref, wv2_ref, ko_ref, vo_ref,
                    *, n_blk):
    half = NSA_CMP_BLOCK // 2

    def compress(t_ref, pos_ref, w1_ref, w2_ref, o_ref):
        for g in range(NSA_KV_HEADS):
            gs = slice(HEAD_DIM * g, HEAD_DIM * (g + 1))
            top = jnp.zeros((n_blk, NSA_CMP_HIDDEN), F32)
            bot = jnp.zeros((n_blk, NSA_CMP_HIDDEN), F32)
            for l in range(half):
                rows = t_ref[0, pl.ds(l, n_blk, stride=NSA_CMP_STRIDE), :][:, gs]
                top = top + _dot(rows + pos_ref[l:l + 1, :], w1_ref[l], hi=True)
                bot = bot + _dot(rows + pos_ref[half + l:half + l + 1, :], w1_ref[half + l], hi=True)
            hid = top + pltpu.roll(bot, n_blk - 1, 0)
            act = 0.5 * hid * (1.0 + jnp.tanh(0.7978845608028654 * (hid + 0.044715 * hid * hid * hid)))
            o_ref[0, :, gs] = _dot(act, w2_ref[...], hi=True)

    compress(kc_ref, pk_ref, wk1_ref, wk2_ref, ko_ref)
    compress(vc_ref, pv_ref, wv1_ref, wv2_ref, vo_ref)


def _nsa_cmp(kv, pos_k, pos_v, wk1, wk2, wv1, wv2):
    bsz, s, _ = kv.shape
    n_blk = s // NSA_CMP_STRIDE
    wk1 = wk1.reshape(NSA_CMP_BLOCK, HEAD_DIM, NSA_CMP_HIDDEN)
    wv1 = wv1.reshape(NSA_CMP_BLOCK, HEAD_DIM, NSA_CMP_HIDDEN)
    small = [pos_k, pos_v, wk1, wk2, wv1, wv2]
    out_spec = pl.BlockSpec((1, n_blk, LANES), lambda b: (b, 0, 0))
    out_sd = jax.ShapeDtypeStruct((bsz, n_blk, LANES), F32)
    return pl.pallas_call(
        functools.partial(_nsa_cmp_kernel, n_blk=n_blk),
        grid=(bsz,),
        in_specs=[pl.BlockSpec((1, s, LANES), lambda b: (b, 0, 0)),
                  pl.BlockSpec((1, s, LANES), lambda b: (b, 0, 1))] + [_full(a.shape) for a in small],
        out_specs=[out_spec, out_spec],
        out_shape=[out_sd, out_sd],
        compiler_params=_params("parallel"),
    )(kv, kv, *small)


def _nsa_attn_kernel(q_ref, gt_ref, kc_ref, vc_ref, ks_ref, kw_ref, vt_ref, ovt_ref, ext_ref,
                     o_ref, acc_ref, *, n_cmp):
    qi = pl.program_id(1)
    cols = NSA_HPG * QBLK
    n_sel_blk = ovt_ref.shape[0]
    key_off = lax.broadcasted_iota(jnp.int32, (QBLK, cols), 0)
    t_pos = qi * QBLK + (lax.broadcasted_iota(jnp.int32, (QBLK, cols), 1) & (QBLK - 1))
    gt_t = gt_ref[0].T
    zeros64 = jnp.zeros((QBLK, HEAD_DIM), F32)
    m0 = jnp.full((1, cols), MASK_NEG, F32)

    groups = range(NSA_KV_HEADS)
    qs_b, o_cmp, sel = [], [], []
    for g in groups:
        gs = slice(HEAD_DIM * g, HEAD_DIM * (g + 1))
        pieces = []
        for hp in range(NSA_HPG):
            qh = q_ref[0, :, (g * NSA_HPG + hp) * HEAD_DIM:(g * NSA_HPG + hp + 1) * HEAD_DIM]
            pieces.append(jnp.concatenate([qh, zeros64] if g == 0 else [zeros64, qh], axis=1))
        qs = jnp.concatenate(pieces, axis=0)
        qs_b.append((qs * LOG2E).astype(BF16))

        s_c = _dot(kc_ref[0], qs, _NT, hi=True)
        valid_c = (key_off * NSA_CMP_STRIDE + (NSA_CMP_BLOCK - 1) <= t_pos) & (key_off < n_cmp)
        s_c = jnp.where(valid_c, s_c, MASK_NEG)
        e_c = jnp.exp(s_c - jnp.max(s_c, axis=0, keepdims=True))
        p_c = jnp.where(valid_c, e_c / jnp.sum(e_c, axis=0, keepdims=True), 0.0)
        o_cmp.append(_dot(vc_ref[0], p_c, _TN)[gs, :])
        p_sum = p_c[:, 0:QBLK] + p_c[:, QBLK:2 * QBLK] + p_c[:, 2 * QBLK:3 * QBLK] + p_c[:, 3 * QBLK:]
        imp = _dot(ovt_ref[...], p_sum, hi=True)

        j = lax.broadcasted_iota(jnp.int32, (n_sel_blk, QBLK), 0)
        tq = qi * QBLK + lax.broadcasted_iota(jnp.int32, (n_sel_blk, QBLK), 1)
        cur = tq // NSA_SEL_BLOCK
        forced = (j == 0) | (j == cur) | (j == cur - 1)
        score = jnp.where(forced, 1e9, jnp.where(j > cur, -1e9, imp))
        rank = jnp.zeros((n_sel_blk, QBLK), F32)
        for i in range(n_sel_blk):
            s_i = score[i:i + 1, :]
            ahead = (s_i > score) | ((s_i == score) & (j > i))
            rank = rank + jnp.where(ahead, 1.0, 0.0)
        sel.append(jnp.where(rank < NSA_N_SELECT, 1.0, 0.0))

    def attend(k_ref, g, slab, kb0, n_kb, m, bias=None, window=False, causal=True):
        k0 = pl.multiple_of(kb0 * QBLK, QBLK)
        s = lax.dot_general(k_ref[0, pl.ds(k0, n_kb * QBLK), :], qs_b[g], _NT, preferred_element_type=F32)
        if bias is not None:
            s = s + bias
        if causal:
            key_pos = k0 + lax.broadcasted_iota(jnp.int32, (n_kb * QBLK, cols), 0)
            t_all = jnp.concatenate([t_pos] * n_kb, axis=0)
            valid = key_pos <= t_all
            if window:
                valid = valid & (key_pos > t_all - NSA_WINDOW)
            s = jnp.where(valid, s, MASK_NEG)
        m_new = jnp.maximum(m, jnp.max(s, axis=0, keepdims=True))
        p = jnp.exp2(s - m_new).astype(BF16)
        pv = jnp.zeros((VT_ROWS, cols), F32)
        for i in range(n_kb):
            vt_blk = vt_ref[0, kb0 + i, slab * VT_ROWS:(slab + 1) * VT_ROWS, :]
            pv = pv + jnp.dot(vt_blk, p[QBLK * i:QBLK * (i + 1), :], preferred_element_type=F32)
        acc_ref[slab] = acc_ref[slab] * jnp.exp2(m - m_new) + pv
        return m_new

    def normalised(slab):
        return acc_ref[slab, 0:HEAD_DIM, :] / acc_ref[slab, HEAD_DIM:HEAD_DIM + 1, :]

    acc_ref[...] = jnp.zeros_like(acc_ref)

    def slc_step(i, ms, causal=False):
        out = []
        for g in groups:
            chosen = _dot(ext_ref[i], sel[g])
            bias = jnp.concatenate([jnp.where(chosen > 0.5, 0.0, MASK_NEG)] * NSA_HPG, axis=1)
            out.append(attend(ks_ref, g, g, i * SLC_KB, SLC_KB, ms[g], bias=bias, causal=causal))
        return tuple(out)

    ms = lax.fori_loop(0, qi // SLC_KB, slc_step, (m0,) * NSA_KV_HEADS)
    slc_step(qi // SLC_KB, ms, causal=True)

    n_win = NSA_WINDOW // QBLK + 1
    for g in groups:
        attend(kw_ref, g, NSA_KV_HEADS + g, jnp.maximum(qi - (n_win - 1), 0), n_win, m0, window=True)

    o_groups = []
    for g in groups:
        def gate(branch):
            return jnp.concatenate(
                [gt_t[(g * NSA_HPG + hp) * 3 + branch:(g * NSA_HPG + hp) * 3 + branch + 1, :]
                 for hp in range(NSA_HPG)], axis=1)

        o_groups.append(gate(0) * o_cmp[g] + gate(1) * normalised(g) + gate(2) * normalised(NSA_KV_HEADS + g))

    o_all = jnp.concatenate(o_groups, axis=0)
    for hp in range(NSA_HPG):
        tile = o_all[:, QBLK * hp:QBLK * (hp + 1)].T
        for g in range(NSA_KV_HEADS):
            c0 = (g * NSA_HPG + hp) * HEAD_DIM
            o_ref[0, :, c0:c0 + HEAD_DIM] = tile[:, HEAD_DIM * g:HEAD_DIM * (g + 1)]


def _nsa_attn(qr, kk, vt, gt, kcmp, vcmp):
    bsz, s, _ = qr.shape
    n_blk = s // NSA_CMP_STRIDE
    n_cmp = n_blk - NSA_CMP_BLOCK // NSA_CMP_STRIDE + 1
    n_sel = s // NSA_SEL_BLOCK
    assert n_blk == QBLK, "compressed-score tiles are laid out one stride block per sublane row"
    c0 = np.arange(n_blk) * NSA_CMP_STRIDE
    s0 = np.arange(n_sel) * NSA_SEL_BLOCK
    lo = np.maximum(c0[:, None], s0[None, :])
    hi = np.minimum(c0[:, None] + NSA_CMP_BLOCK, s0[None, :] + NSA_SEL_BLOCK)
    overlap = (np.maximum(hi - lo, 0) / NSA_CMP_STRIDE).astype(np.float32)
    overlap[n_cmp:] = 0.0
    key = np.arange(s)
    expand = (key[:, None] // NSA_SEL_BLOCK == np.arange(n_sel)[None, :]).astype(np.float32)
    expand = expand.reshape(s // (SLC_KB * QBLK), SLC_KB * QBLK, n_sel)
    consts = [jnp.asarray(overlap.T), jnp.asarray(expand, BF16)]

    cmp_spec = pl.BlockSpec((1, n_blk, LANES), lambda b, i: (b, 0, 0))
    return pl.pallas_call(
        functools.partial(_nsa_attn_kernel, n_cmp=n_cmp),
        grid=(bsz, s // QBLK),
        in_specs=[pl.BlockSpec((1, QBLK, NSA_WIDTH), lambda b, i: (b, i, 0)),
                  pl.BlockSpec((1, QBLK, LANES), lambda b, i: (b, i, 0)),
                  cmp_spec, cmp_spec,
                  pl.BlockSpec((1, s, LANES), lambda b, i: (b, 0, 0)),
                  pl.BlockSpec((1, s, LANES), lambda b, i: (b, 0, 1)),
                  pl.BlockSpec((1, s // QBLK, 4 * VT_ROWS, QBLK), lambda b, i: (b, 0, 0, 0))]
        + [_full(a.shape) for a in consts],
        out_specs=pl.BlockSpec((1, QBLK, NSA_WIDTH), lambda b, i: (b, i, 0)),
        out_shape=jax.ShapeDtypeStruct((bsz, s, NSA_WIDTH), F32),
        scratch_shapes=[pltpu.VMEM((2 * NSA_KV_HEADS, VT_ROWS, NSA_HPG * QBLK), F32)],
        compiler_params=_params("parallel", "arbitrary"),
    )(qr, gt, kcmp, vcmp, kk, kk, vt, *consts)


def _out_ln_kernel(x_ref, og_ref, or_ref, on_ref, w_ref, lw_ref, lb_ref, o_ref, *, alpha):
    mix = (_dot(og_ref[...], w_ref[0:LIN_WIDTH, :])
           + _dot(or_ref[...], w_ref[LIN_WIDTH:2 * LIN_WIDTH, :])
           + _dot(on_ref[...], w_ref[2 * LIN_WIDTH:, :]))
    o_ref[...] = _layer_norm(alpha * x_ref[...] + mix, lw_ref[...], lb_ref[...])


def _out_ln(x2, og, orw, on, w, lw, lb, alpha, tm=512):
    t = x2.shape[0]

    def rows(width):
        return pl.BlockSpec((tm, width), lambda i: (i, 0))

    return pl.pallas_call(
        functools.partial(_out_ln_kernel, alpha=alpha),
        grid=(t // tm,),
        in_specs=[rows(D_MODEL), rows(LIN_WIDTH), rows(LIN_WIDTH), rows(NSA_WIDTH),
                  _full(w.shape), _full(lw.shape), _full(lb.shape)],
        out_specs=rows(D_MODEL),
        out_shape=jax.ShapeDtypeStruct((t, D_MODEL), F32),
        compiler_params=_params("parallel"),
    )(x2, og, orw, on, w, lw, lb)


def _ffn_ln_kernel(x_ref, wg_ref, wu_ref, wd_ref, lw_ref, lb_ref, o_ref, acc_ref, *, alpha):
    j = pl.program_id(1)
    x = x_ref[...]
    xb = x.astype(BF16)
    gate = jnp.dot(xb, wg_ref[...], preferred_element_type=F32)
    up = jnp.dot(xb, wu_ref[...], preferred_element_type=F32)
    part = _dot(gate * _sigmoid(gate) * up, wd_ref[...])

    @pl.when(j == 0)
    def _():
        acc_ref[...] = part

    @pl.when(j > 0)
    def _():
        acc_ref[...] += part

    @pl.when(j == pl.num_programs(1) - 1)
    def _():
        o_ref[...] = _layer_norm(alpha * x + acc_ref[...], lw_ref[...], lb_ref[...])


def _ffn_ln(x2, wg, wu, wd, lw, lb, alpha, tm=512, th=1408):
    t = x2.shape[0]
    return pl.pallas_call(
        functools.partial(_ffn_ln_kernel, alpha=alpha),
        grid=(t // tm, FFN_HIDDEN // th),
        in_specs=[pl.BlockSpec((tm, D_MODEL), lambda i, j: (i, 0)),
                  pl.BlockSpec((D_MODEL, th), lambda i, j: (0, j)),
                  pl.BlockSpec((D_MODEL, th), lambda i, j: (0, j)),
                  pl.BlockSpec((th, D_MODEL), lambda i, j: (j, 0)),
                  _full(lw.shape), _full(lb.shape)],
        out_specs=pl.BlockSpec((tm, D_MODEL), lambda i, j: (i, 0)),
        out_shape=jax.ShapeDtypeStruct((t, D_MODEL), F32),
        scratch_shapes=[pltpu.VMEM((tm, D_MODEL), F32)],
        compiler_params=_params("parallel", "arbitrary"),
    )(x2, wg, wu, wd, lw, lb)


def _pad_cols(w, width):
    return jnp.pad(w, ((0, 0), (0, width - w.shape[1])))


def _pad_rows(w, height):
    return jnp.pad(w, ((0, height - w.shape[0]), (0, 0)))


def _rwkv_slab_cols(w, w_vres):
    o = 3 * LIN_WIDTH
    parts = [w[:, :o],
             _pad_cols(w[:, o:o + RWKV_DECAY_RANK], LANES),
             _pad_cols(w[:, o + RWKV_DECAY_RANK:o + RWKV_DECAY_RANK + RWKV_ICLR_RANK], LANES),
             _pad_cols(w[:, o + RWKV_DECAY_RANK + RWKV_ICLR_RANK:], 2 * LANES),
             _pad_cols(w_vres, LANES)]
    return jnp.concatenate(parts, axis=1)


def _rope_tables(s):
    half = HEAD_DIM // 2
    inv = ROPE_THETA ** (-jnp.arange(half, dtype=F32) / half)
    ang = jnp.arange(s, dtype=jnp.int32).astype(F32)[:, None] * inv
    cos, sin = jnp.cos(ang), jnp.sin(ang)
    cos2 = jnp.tile(jnp.concatenate([cos, cos], axis=1), (1, LANES // HEAD_DIM))
    sin2 = jnp.tile(jnp.concatenate([-sin, sin], axis=1), (1, LANES // HEAD_DIM))
    i = np.arange(LANES)
    src = np.where(i % HEAD_DIM < half, i + half, i - half)
    rot = np.zeros((LANES, LANES), np.float32)
    rot[src, i] = 1.0
    return cos2, sin2, jnp.asarray(rot)


def kernel(x, w_in, w_in_vres, gla_w_a2, gla_b_a, gla_ln_w, gla_ln_b, rwkv_mu, rwkv_mu_vres, rwkv_w0, rwkv_w2, rwkv_a0, rwkv_a2, rwkv_v0, rwkv_v2, rwkv_g2, rwkv_k_k, rwkv_k_a, rwkv_r_k, rwkv_ln_w, rwkv_ln_b, nsa_pos_k, nsa_pos_v, nsa_wk1, nsa_wk2, nsa_wv1, nsa_wv2, w_out, ln1_w, ln1_b, ffn_w_gate, ffn_w_up, ffn_w_down, ln2_w, ln2_b):
    bsz, s, d = x.shape
    depth = w_in.shape[0]
    alpha = float((2 * depth) ** 0.25)
    gla_cols = 4 * LIN_WIDTH + GLA_GATE_RANK
    rwkv_cols = 3 * LIN_WIDTH + RWKV_DECAY_RANK + RWKV_ICLR_RANK + RWKV_GATE_RANK
    cos2, sin2, rot = _rope_tables(s)
    row = lambda a: a.reshape(1, -1)

    x2 = x.reshape(bsz * s, d)
    v_first = None
    for l in range(depth):
        w = w_in[l]
        w_r = w[:, gla_cols:gla_cols + rwkv_cols]
        if l == 0:
            w_vres = jnp.zeros((d, RWKV_VRES_RANK), F32)
            mu_vres = jnp.zeros((1, RWKV_VRES_RANK), F32)
        else:
            w_vres = w_in_vres[l - 1]
            mu_vres = row(rwkv_mu_vres[l - 1])
        wg = _pad_cols(w[:, :gla_cols], GLA_SLAB).astype(BF16)
        wr = _rwkv_slab_cols(w_r, w_vres).astype(BF16)
        wn = _pad_cols(w[:, gla_cols + rwkv_cols:], NSA_SLAB).astype(BF16)
        hg, hr, hn = _proj(x2, wg, wr, wn)
        hg = hg.reshape(bsz, s, GLA_SLAB)
        hr = hr.reshape(bsz, s, RWKV_SLAB)
        hn = hn.reshape(bsz, s, NSA_SLAB)

        o_gla = _gla(hg, _pad_rows(gla_w_a2[l], LANES), row(gla_b_a[l]), row(gla_ln_w[l]), row(gla_ln_b[l]))

        mu = _rwkv_slab_cols(row(rwkv_mu[l]), mu_vres)
        small = [mu, row(rwkv_w0[l]), _pad_rows(rwkv_w2[l], LANES), row(rwkv_a0[l]),
                 _pad_rows(rwkv_a2[l], LANES), _pad_rows(rwkv_g2[l], 2 * LANES), row(rwkv_k_k[l]),
                 row(rwkv_k_a[l]), row(rwkv_r_k[l]), row(rwkv_ln_w[l]), row(rwkv_ln_b[l])]
        if l == 0:
            o_rwkv, v_first = _rwkv(hr, None, small)
        else:
            small += [row(rwkv_v0[l - 1]), _pad_rows(rwkv_v2[l - 1], LANES)]
            o_rwkv = _rwkv(hr, v_first, small)

        qr, kvc, kk, vt, gt = _nsa_prep(hn, cos2, sin2, rot)
        kcmp, vcmp = _nsa_cmp(kvc, nsa_pos_k[l], nsa_pos_v[l], nsa_wk1[l], nsa_wk2[l], nsa_wv1[l], nsa_wv2[l])
        o_nsa = _nsa_attn(qr, kk, vt, gt, kcmp, vcmp)

        x2 = _out_ln(x2, o_gla.reshape(bsz * s, LIN_WIDTH), o_rwkv.reshape(bsz * s, LIN_WIDTH),
                     o_nsa.reshape(bsz * s, NSA_WIDTH), w_out[l].astype(BF16), row(ln1_w[l]), row(ln1_b[l]), alpha)
        x2 = _ffn_ln(x2, ffn_w_gate[l].astype(BF16), ffn_w_up[l].astype(BF16), ffn_w_down[l].astype(BF16),
                     row(ln2_w[l]), row(ln2_b[l]), alpha)
    return x2.reshape(bsz, s, d)
```

```python
import functools

import numpy as np
import jax
import jax.numpy as jnp
from jax import lax
from jax.experimental import pallas as pl
from jax.experimental.pallas import tpu as pltpu

F32 = jnp.float32
BF16 = jnp.bfloat16
HIGHEST = lax.Precision.HIGHEST

D_MODEL = 1024
HEAD_DIM = 64
N_LIN_HEADS = 4
LIN_WIDTH = N_LIN_HEADS * HEAD_DIM
GLA_GATE_RANK = 16
GLA_TAU = 16.0
RWKV_DECAY_RANK = 64
RWKV_ICLR_RANK = 64
RWKV_VRES_RANK = 32
RWKV_GATE_RANK = 160
RWKV_GN_EPS = 64e-5
NSA_HEADS = 8
NSA_KV_HEADS = 2
NSA_HPG = NSA_HEADS // NSA_KV_HEADS
NSA_WIDTH = NSA_HEADS * HEAD_DIM
NSA_KV_WIDTH = NSA_KV_HEADS * HEAD_DIM
NSA_CMP_BLOCK = 32
NSA_CMP_STRIDE = 16
NSA_CMP_HIDDEN = 256
NSA_SEL_BLOCK = 64
NSA_N_SELECT = 16
NSA_WINDOW = 512
ROPE_THETA = 10000.0
FFN_HIDDEN = 2816
LN_EPS = 1e-5
MASK_NEG = -1e30

LANES = 128
CHUNK = 64
SUB = 16
RWKV_GROUP = 4
GLA_GROUP = 4
QBLK = 128
SLC_KB = 4
VT_ROWS = HEAD_DIM + 16
LOG2E = 1.4426950408889634
VMEM_LIMIT = 56 * 1024 * 1024

GLA_SLAB = 4 * LIN_WIDTH + LANES
RWKV_SLAB = 3 * LIN_WIDTH + LANES + LANES + 2 * LANES + LANES
NSA_SLAB = NSA_WIDTH + 6 * NSA_KV_WIDTH + LANES
RW_WLR, RW_ALR, RW_GLR, RW_VLR = 768, 896, 1024, 1280


def _dot(a, b, dims=None, hi=False):
    dims = dims or (((a.ndim - 1,), (0,)), ((), ()))
    if hi:
        return lax.dot_general(a, b, dims, precision=HIGHEST, preferred_element_type=F32)
    return lax.dot_general(a.astype(BF16), b.astype(BF16), dims, preferred_element_type=F32)


_NT = (((1,), (1,)), ((), ()))
_TN = (((0,), (0,)), ((), ()))


def _sigmoid(x):
    return 1.0 / (1.0 + jnp.exp(-x))


def _softplus(x):
    return jnp.maximum(x, 0.0) + jnp.log(1.0 + jnp.exp(-jnp.abs(x)))


def _head_norm(x, w, b, eps):
    mu = jnp.mean(x, axis=-1, keepdims=True)
    xc = x - mu
    var = jnp.mean(xc * xc, axis=-1, keepdims=True)
    return xc * lax.rsqrt(var + eps) * w + b


def _layer_norm(x, w, b):
    mu = jnp.mean(x, axis=-1, keepdims=True)
    xc = x - mu
    var = jnp.mean(xc * xc, axis=-1, keepdims=True)
    return xc * lax.rsqrt(var + LN_EPS) * w + b


def _params(*sem):
    return pltpu.CompilerParams(dimension_semantics=sem, vmem_limit_bytes=VMEM_LIMIT)


def _full(shape):
    nd = len(shape)
    return pl.BlockSpec(shape, lambda *_: (0,) * nd)


def _proj_kernel(x_ref, wg_ref, wr_ref, wn_ref, og_ref, or_ref, on_ref):
    xb = x_ref[...].astype(BF16)
    og_ref[...] = jnp.dot(xb, wg_ref[...], preferred_element_type=F32)
    or_ref[...] = jnp.dot(xb, wr_ref[...], preferred_element_type=F32)
    on_ref[...] = jnp.dot(xb, wn_ref[...], preferred_element_type=F32)


def _proj(x2, wg, wr, wn, tm=256):
    t = x2.shape[0]
    return pl.pallas_call(
        _proj_kernel,
        grid=(t // tm,),
        in_specs=[pl.BlockSpec((tm, D_MODEL), lambda i: (i, 0)),
                  _full(wg.shape), _full(wr.shape), _full(wn.shape)],
        out_specs=[pl.BlockSpec((tm, GLA_SLAB), lambda i: (i, 0)),
                   pl.BlockSpec((tm, RWKV_SLAB), lambda i: (i, 0)),
                   pl.BlockSpec((tm, NSA_SLAB), lambda i: (i, 0))],
        out_shape=[jax.ShapeDtypeStruct((t, GLA_SLAB), F32),
                   jax.ShapeDtypeStruct((t, RWKV_SLAB), F32),
                   jax.ShapeDtypeStruct((t, NSA_SLAB), F32)],
        compiler_params=_params("parallel"),
    )(x2, wg, wr, wn)


def _group_rows(x, u):
    return jnp.concatenate(
        [jnp.broadcast_to(x[SUB * i + u:SUB * i + u + 1, :], (SUB, x.shape[1]))
         for i in range(CHUNK // SUB)], axis=0)


def _gla_kernel(h_ref, wa2_ref, ba_ref, lnw_ref, lnb_ref, tril_ref, blk_ref, ones_ref, dm_ref, same_ref,
                same_sub_ref, o_ref, st_ref, *, n_chunks):
    @pl.when(pl.program_id(1) == 0)
    def _():
        st_ref[...] = jnp.zeros_like(st_ref)

    row_in_sub = lax.broadcasted_iota(jnp.int32, (CHUNK, LIN_WIDTH), 0) % SUB
    dm = dm_ref[...]
    heads = range(N_LIN_HEADS)
    n_sub = CHUNK // SUB

    def own_head(m, mask_ref):
        return jnp.concatenate([m] * N_LIN_HEADS, axis=0) * mask_ref[...]

    def each(f, *lists):
        return [f(*args) for args in zip(*lists)]

    def chunk_group(gi, carry):
        rows = [pl.ds(pl.multiple_of((gi * GLA_GROUP + j) * CHUNK, CHUNK), CHUNK) for j in range(GLA_GROUP)]
        ps = [h_ref[0, rw, :] for rw in rows]
        q = each(lambda p: p[:, 0:256] * (HEAD_DIM ** -0.5), ps)
        k = each(lambda p: p[:, 256:512], ps)
        v = each(lambda p: p[:, 512:768], ps)
        g = each(lambda p: p[:, 768:1024], ps)
        z = each(lambda p: _dot(p[:, 1024:1152], wa2_ref[...], hi=True) + ba_ref[...], ps)
        log_a = each(lambda x: -_softplus(-x) * (1.0 / GLA_TAU), z)
        b = each(lambda x: _dot(tril_ref[...], x, hi=True), log_a)
        b_end = each(lambda x: _dot(blk_ref[...], x, hi=True), log_a)
        qe = each(lambda x, bb: x * jnp.exp(bb), q, b)
        kd = each(lambda x, be, bb: x * jnp.exp(be - bb), k, b_end, b)
        p_end = each(jnp.exp, b_end)
        v_st = each(lambda x: jnp.concatenate([x[:, HEAD_DIM * h:HEAD_DIM * (h + 1)] for h in heads], axis=0), v)

        at = [jnp.zeros((CHUNK, LIN_WIDTH), F32) for _ in ps]
        for u0 in range(0, SUB, N_LIN_HEADS):
            us = range(u0, u0 + N_LIN_HEADS)
            es = each(lambda bb, qq, kk: jnp.concatenate(
                [jnp.exp(jnp.where(row_in_sub <= u, _group_rows(bb, u) - bb, MASK_NEG)) * (_group_rows(qq, u) * kk)
                 for u in us], axis=0), b, q, k)
            a_rep = each(lambda e: _dot(e, ones_ref[...]), es)
            for i, u in enumerate(us):
                at = each(lambda acc, ar: acc + jnp.where(dm == u, ar[CHUNK * i:CHUNK * (i + 1), :], 0.0), at, a_rep)
        o_st = each(lambda x, vs: _dot(own_head(x, same_ref), vs, _TN), at, v_st)

        upd, q_own = [], []
        for i in range(n_sub):
            rs = slice(SUB * i, SUB * (i + 1))
            upd.append(each(lambda vs, kk: _dot(
                jnp.concatenate([vs[CHUNK * h + SUB * i:CHUNK * h + SUB * (i + 1), :] for h in heads], axis=0),
                own_head(kk[rs], same_sub_ref), _TN), v_st, kd))
            q_own.append(each(lambda x: own_head(x[rs], same_sub_ref), qe))
        st = st_ref[...]
        for c, rw in enumerate(rows):
            inter = []
            for i in range(n_sub):
                inter.append(_dot(q_own[i][c], st, _NT))
                st = st * p_end[c][SUB * i:SUB * i + 1, :] + upd[i][c]
            o_c = o_st[c] + jnp.concatenate(
                [inter[i][SUB * h:SUB * (h + 1), :] for h in heads for i in range(n_sub)], axis=0)
            o_c = _head_norm(o_c, 1.0, 0.0, LN_EPS)
            o = jnp.concatenate([o_c[CHUNK * h:CHUNK * (h + 1), :] for h in heads], axis=1)
            o_ref[0, rw, :] = (o * lnw_ref[...] + lnb_ref[...]) * (g[c] * _sigmoid(g[c]))
        st_ref[...] = st
        return carry

    lax.fori_loop(0, n_chunks // GLA_GROUP, chunk_group, 0)


def _gla(hg, wa2, ba, lnw, lnb, ts=512):
    bsz, s, _ = hg.shape
    r = np.arange(CHUNK)
    tril = ((r[:, None] // SUB == r[None, :] // SUB) & (r[None, :] <= r[:, None])).astype(np.float32)
    blk = (r[:, None] // SUB == r[None, :] // SUB).astype(np.float32)
    c = np.arange(LIN_WIDTH)
    same = (c[:, None] // HEAD_DIM == c[None, :] // HEAD_DIM).astype(np.float32)
    same_sub = (r[:, None] // SUB == c[None, :] // HEAD_DIM).astype(np.float32)
    dm = ((c[None, :] % HEAD_DIM) - SUB * (r[:, None] // SUB)).astype(np.int32)
    consts = [jnp.asarray(tril), jnp.asarray(blk), jnp.asarray(same, BF16), jnp.asarray(dm), jnp.asarray(same),
              jnp.asarray(same_sub)]
    small = [wa2, ba, lnw, lnb] + consts
    return pl.pallas_call(
        functools.partial(_gla_kernel, n_chunks=ts // CHUNK),
        grid=(bsz, s // ts),
        in_specs=[pl.BlockSpec((1, ts, GLA_SLAB), lambda b, j: (b, j, 0))] + [_full(a.shape) for a in small],
        out_specs=pl.BlockSpec((1, ts, LIN_WIDTH), lambda b, j: (b, j, 0)),
        out_shape=jax.ShapeDtypeStruct((bsz, s, LIN_WIDTH), F32),
        scratch_shapes=[pltpu.VMEM((HEAD_DIM, LIN_WIDTH), F32)],
        compiler_params=_params("parallel", "arbitrary"),
    )(hg, *small)


def _rwkv_kernel(*refs, n_chunks, has_vres):
    if has_vres:
        (h_ref, vf_ref, mu_ref, w0_ref, w2_ref, a0_ref, a2_ref, g2_ref, kk_ref, ka_ref, rk_ref,
         lnw_ref, lnb_ref, v0_ref, v2_ref, tril_ref, ones_ref, same_ref, strict_ref, incl_ref, eye_ref,
         o_ref, st_ref, prev_ref) = refs
    else:
        (h_ref, mu_ref, w0_ref, w2_ref, a0_ref, a2_ref, g2_ref, kk_ref, ka_ref, rk_ref,
         lnw_ref, lnb_ref, tril_ref, ones_ref, same_ref, strict_ref, incl_ref, eye_ref,
         o_ref, vf_ref, st_ref, prev_ref) = refs

    @pl.when(pl.program_id(1) == 0)
    def _():
        st_ref[...] = jnp.zeros_like(st_ref)
        prev_ref[...] = jnp.zeros_like(prev_ref)

    row = lax.broadcasted_iota(jnp.int32, (CHUNK, RWKV_SLAB), 0)
    heads = range(N_LIN_HEADS)

    def tile_rows(m):
        return jnp.concatenate([m] * N_LIN_HEADS, axis=0)

    def own_head(m):
        return tile_rows(m) * same_ref[...]

    def stack(m):
        return jnp.concatenate([m[:, HEAD_DIM * h:HEAD_DIM * (h + 1)] for h in heads], axis=0)

    def unstack(m):
        return jnp.concatenate([m[CHUNK * h:CHUNK * (h + 1), :] for h in heads], axis=1)

    def each(f, *lists):
        return [f(*args) for args in zip(*lists)]

    def prepare(ps, prev_rows, vfs):
        prev = each(lambda p, pr: jnp.where(row == 0, pr, pltpu.roll(p, 1, 0)), ps, prev_rows)
        xm = each(lambda p, pv: p + (pv - p) * mu_ref[...], ps, prev)
        r = each(lambda x: x[:, 0:256], xm)
        k = each(lambda x: x[:, 256:512], xm)
        v = each(lambda x: x[:, 512:768], xm)
        v_shift = v
        w_log = each(lambda x: -_softplus(-(w0_ref[...] + _dot(jnp.tanh(x[:, RW_WLR:RW_ALR]), w2_ref[...], hi=True)))
                     - 0.5, xm)
        lw = each(lambda w: -jnp.exp(w), w_log)
        a = each(lambda x: _sigmoid(a0_ref[...] + _dot(x[:, RW_ALR:RW_GLR], a2_ref[...], hi=True)), xm)
        g = each(lambda x: _dot(_sigmoid(x[:, RW_GLR:RW_VLR]), g2_ref[...], hi=True), xm)
        if has_vres:
            mixv = each(lambda x: _sigmoid(v0_ref[...] + _dot(x[:, RW_VLR:RWKV_SLAB], v2_ref[...], hi=True)), xm)
            v = each(lambda vv, vf, mx: vv + (vf - vv) * mx, v, vfs, mixv)
        kkv = each(lambda kk: kk * kk_ref[...], k)
        nrm = each(lambda kv: jnp.sqrt(_dot(kv * kv, ones_ref[...], hi=True)), kkv)
        kkn = each(lambda kv, n: kv / jnp.maximum(n, 1e-12), kkv, nrm)
        k2 = each(lambda kk, aa: kk * (1.0 + (aa - 1.0) * ka_ref[...]), k, a)
        bv = each(lambda kn, aa: kn * aa, kkn, a)
        c = each(lambda w: _dot(tril_ref[...], w, hi=True), lw)
        c_end = each(lambda cc: cc[CHUNK - 1:CHUNK, :], c)
        e_c = each(jnp.exp, c)
        e_nc = each(lambda cc: jnp.exp(-cc), c)
        e_rest = each(lambda ce, cc: jnp.exp(ce - cc), c_end, c)
        ar = each(lambda kn, cc, w, rr, ec: jnp.concatenate(
            [own_head(-kn * jnp.exp(cc - w)), own_head(rr * ec)], axis=0), kkn, c, lw, r, e_c)
        bk = each(lambda b_, kk, en: jnp.concatenate([tile_rows(b_ * en), tile_rows(kk * en)], axis=0), bv, k2, e_nc)
        gm = each(lambda x, y: _dot(x, y, _NT), ar, bk)
        n_ab = each(lambda m: m[0:256, 0:256] * strict_ref[...], gm)
        m_ak = each(lambda m: m[0:256, 256:512] * strict_ref[...], gm)
        r_bk = each(lambda m: m[256:512, :] * jnp.concatenate([incl_ref[...], incl_ref[...]], axis=1), gm)
        v_st = each(stack, v)
        mv = each(_dot, m_ak, v_st)
        inv = each(lambda n: eye_ref[...] + n, n_ab)
        pw = n_ab
        for _ in range(5):
            pw = each(lambda x: _dot(x, x), pw)
            inv = each(lambda i_, x: i_ + _dot(i_, x), inv, pw)
        bk_rest = each(lambda b_, kk, er: jnp.concatenate([own_head(b_ * er), own_head(kk * er)], axis=0),
                       bv, k2, e_rest)
        p_end = each(jnp.exp, c_end)
        bonus = each(lambda rr, kk, vv: _dot(rr * kk * rk_ref[...], ones_ref[...], hi=True) * vv, r, k2, v)
        return list(zip(ar, mv, inv, v_st, r_bk, bk_rest, p_end, bonus, g)), v_shift

    def advance(prep, s0):
        ar, mv, inv, v_st, r_bk, bk_rest, p_end, bonus, g = prep
        a_s = _dot(ar, s0, _NT)
        uv = jnp.concatenate([_dot(inv, a_s[0:256] + mv), v_st], axis=0)
        y = unstack(_head_norm(a_s[256:512] + _dot(r_bk, uv), 1.0, 0.0, RWKV_GN_EPS))
        s1 = s0 * p_end + _dot(uv, bk_rest, _TN)
        return (y * lnw_ref[...] + lnb_ref[...] + bonus) * g, s1

    def chunk_group(gi, carry):
        rows = [pl.ds(pl.multiple_of((gi * RWKV_GROUP + j) * CHUNK, CHUNK), CHUNK) for j in range(RWKV_GROUP)]
        ps = [h_ref[0, rw, :] for rw in rows]
        vfs = [vf_ref[0, rw, :] if has_vres else None for rw in rows]
        prev_rows = [prev_ref[...]] + [p[CHUNK - 1:CHUNK, :] for p in ps[:-1]]
        preps, shifted = prepare(ps, prev_rows, vfs)
        s = st_ref[...]
        for j, rw in enumerate(rows):
            out, s = advance(preps[j], s)
            o_ref[0, rw, :] = out
            if not has_vres:
                vf_ref[0, rw, :] = shifted[j]
        prev_ref[...] = ps[-1][CHUNK - 1:CHUNK, :]
        st_ref[...] = s
        return carry

    lax.fori_loop(0, n_chunks // RWKV_GROUP, chunk_group, 0)


def _rwkv(hr, v_first, small, ts=512):
    bsz, s, _ = hr.shape
    has_vres = v_first is not None
    r = np.arange(CHUNK)
    tril = (r[None, :] <= r[:, None]).astype(np.float32)
    i = np.arange(LIN_WIDTH)
    same = (i[:, None] // HEAD_DIM == i[None, :] // HEAD_DIM).astype(np.float32)
    t_in, s_in = i[:, None] % CHUNK, i[None, :] % CHUNK
    consts = [tril, same, same, same * (s_in < t_in), same * (s_in <= t_in), np.eye(LIN_WIDTH, dtype=np.float32)]
    small = list(small) + [jnp.asarray(c, F32) for c in consts]
    seq_spec = pl.BlockSpec((1, ts, LIN_WIDTH), lambda b, j: (b, j, 0))
    in_specs = [pl.BlockSpec((1, ts, RWKV_SLAB), lambda b, j: (b, j, 0))]
    args = [hr]
    if has_vres:
        in_specs.append(seq_spec)
        args.append(v_first)
    in_specs += [_full(a.shape) for a in small]
    out_sd = jax.ShapeDtypeStruct((bsz, s, LIN_WIDTH), F32)
    return pl.pallas_call(
        functools.partial(_rwkv_kernel, n_chunks=ts // CHUNK, has_vres=has_vres),
        grid=(bsz, s // ts),
        in_specs=in_specs,
        out_specs=seq_spec if has_vres else [seq_spec, seq_spec],
        out_shape=out_sd if has_vres else [out_sd, out_sd],
        scratch_shapes=[pltpu.VMEM((HEAD_DIM, LIN_WIDTH), F32),
                        pltpu.VMEM((1, RWKV_SLAB), F32)],
        compiler_params=_params("parallel", "arbitrary"),
    )(*args, *small)


def _nsa_prep_kernel(h_ref, cos_ref, sin_ref, rot_ref, q_ref, kvc_ref, kk_ref, vt_ref, gt_ref, *, n_qblk):
    cos = cos_ref[...]
    sin = sin_ref[...]
    rot = rot_ref[...]

    def rope(x):
        return x * cos + _dot(x, rot, hi=True) * sin

    def col(i):
        return h_ref[0, :, NSA_WIDTH + LANES * i:NSA_WIDTH + LANES * (i + 1)]

    for i in range(NSA_WIDTH // LANES):
        cs = slice(LANES * i, LANES * (i + 1))
        q_ref[0, :, cs] = rope(h_ref[0, :, cs]) * (HEAD_DIM ** -0.5)
    kvc_ref[0, :, 0:LANES] = rope(col(0))
    kvc_ref[0, :, LANES:2 * LANES] = col(1)
    kk_ref[0, :, 0:LANES] = rope(col(2)).astype(BF16)
    kk_ref[0, :, LANES:2 * LANES] = rope(col(4)).astype(BF16)
    pad = VT_ROWS - HEAD_DIM
    ones_rows = jnp.where(lax.broadcasted_iota(jnp.int32, (pad, QBLK), 0) == 0, 1.0, 0.0).astype(BF16)
    for i in range(n_qblk):
        rs = slice(QBLK * i, QBLK * (i + 1))
        for branch, c in enumerate((3, 5)):
            v_t = col(c)[rs, :].T.astype(BF16)
            for g in range(NSA_KV_HEADS):
                r0 = (branch * NSA_KV_HEADS + g) * VT_ROWS
                vt_ref[0, i, r0:r0 + HEAD_DIM, :] = v_t[HEAD_DIM * g:HEAD_DIM * (g + 1), :]
                vt_ref[0, i, r0 + HEAD_DIM:r0 + VT_ROWS, :] = ones_rows
    gt_ref[0] = _sigmoid(h_ref[0, :, NSA_WIDTH + 6 * LANES:NSA_SLAB])


def _nsa_prep(hn, cos2, sin2, rot, ts=512):
    bsz, s, _ = hn.shape
    n_qblk = ts // QBLK

    def seq(width):
        return pl.BlockSpec((1, ts, width), lambda b, j: (b, j, 0))

    return pl.pallas_call(
        functools.partial(_nsa_prep_kernel, n_qblk=n_qblk),
        grid=(bsz, s // ts),
        in_specs=[seq(NSA_SLAB),
                  pl.BlockSpec((ts, LANES), lambda b, j: (j, 0)),
                  pl.BlockSpec((ts, LANES), lambda b, j: (j, 0)),
                  _full(rot.shape)],
        out_specs=[seq(NSA_WIDTH), seq(2 * LANES), seq(2 * LANES),
                   pl.BlockSpec((1, n_qblk, 4 * VT_ROWS, QBLK), lambda b, j: (b, j, 0, 0)),
                   seq(LANES)],
        out_shape=[jax.ShapeDtypeStruct((bsz, s, NSA_WIDTH), F32),
                   jax.ShapeDtypeStruct((bsz, s, 2 * LANES), F32),
                   jax.ShapeDtypeStruct((bsz, s, 2 * LANES), BF16),
                   jax.ShapeDtypeStruct((bsz, s // QBLK, 4 * VT_ROWS, QBLK), BF16),
                   jax.ShapeDtypeStruct((bsz, s, LANES), F32)],
        compiler_params=_params("parallel", "parallel"),
    )(hn, cos2, sin2, rot)


def _nsa_cmp_kernel(kc_ref, vc_ref, pk_ref, pv_ref, wk1_ref, wk2_ref, wv1_ref, wv2_ref, ko_ref, vo_ref,
                    *, n_blk):
    half = NSA_CMP_BLOCK // 2

    def compress(t_ref, pos_ref, w1_ref, w2_ref, o_ref):
        for g in range(NSA_KV_HEADS):
            gs = slice(HEAD_DIM * g, HEAD_DIM * (g + 1))
            top = jnp.zeros((n_blk, NSA_CMP_HIDDEN), F32)
            bot = jnp.zeros((n_blk, NSA_CMP_HIDDEN), F32)
            for l in range(half):
                rows = t_ref[0, pl.ds(l, n_blk, stride=NSA_CMP_STRIDE), :][:, gs]
                top = top + _dot(rows + pos_ref[l:l + 1, :], w1_ref[l], hi=True)
                bot = bot + _dot(rows + pos_ref[half + l:half + l + 1, :], w1_ref[half + l], hi=True)
            hid = top + pltpu.roll(bot, n_blk - 1, 0)
            act = 0.5 * hid * (1.0 + jnp.tanh(0.7978845608028654 * (hid + 0.044715 * hid * hid * hid)))
            o_ref[0, :, gs] = _dot(act, w2_ref[...], hi=True)

    compress(kc_ref, pk_ref, wk1_ref, wk2_ref, ko_ref)
    compress(vc_ref, pv_ref, wv1_ref, wv2_ref, vo_ref)


def _nsa_cmp(kv, pos_k, pos_v, wk1, wk2, wv1, wv2):
    bsz, s, _ = kv.shape
    n_blk = s // NSA_CMP_STRIDE
    wk1 = wk1.reshape(NSA_CMP_BLOCK, HEAD_DIM, NSA_CMP_HIDDEN)
    wv1 = wv1.reshape(NSA_CMP_BLOCK, HEAD_DIM, NSA_CMP_HIDDEN)
    small = [pos_k, pos_v, wk1, wk2, wv1, wv2]
    out_spec = pl.BlockSpec((1, n_blk, LANES), lambda b: (b, 0, 0))
    out_sd = jax.ShapeDtypeStruct((bsz, n_blk, LANES), F32)
    return pl.pallas_call(
        functools.partial(_nsa_cmp_kernel, n_blk=n_blk),
        grid=(bsz,),
        in_specs=[pl.BlockSpec((1, s, LANES), lambda b: (b, 0, 0)),
                  pl.BlockSpec((1, s, LANES), lambda b: (b, 0, 1))] + [_full(a.shape) for a in small],
        out_specs=[out_spec, out_spec],
        out_shape=[out_sd, out_sd],
        compiler_params=_params("parallel"),
    )(kv, kv, *small)


def _nsa_attn_kernel(q_ref, gt_ref, kc_ref, vc_ref, ks_ref, kw_ref, vt_ref, ovt_ref, ext_ref,
                     o_ref, acc_ref, *, n_cmp):
    qi = pl.program_id(1)
    cols = NSA_HPG * QBLK
    n_sel_blk = ovt_ref.shape[0]
    key_off = lax.broadcasted_iota(jnp.int32, (QBLK, cols), 0)
    t_pos = qi * QBLK + (lax.broadcasted_iota(jnp.int32, (QBLK, cols), 1) & (QBLK - 1))
    gt_t = gt_ref[0].T
    zeros64 = jnp.zeros((QBLK, HEAD_DIM), F32)
    m0 = jnp.full((1, cols), MASK_NEG, F32)

    groups = list(range(NSA_KV_HEADS))

    def each(f, *lists):
        return [f(*args) for args in zip(*lists)]

    def stacked_q(g):
        pieces = []
        for hp in range(NSA_HPG):
            qh = q_ref[0, :, (g * NSA_HPG + hp) * HEAD_DIM:(g * NSA_HPG + hp + 1) * HEAD_DIM]
            pieces.append(jnp.concatenate([qh, zeros64] if g == 0 else [zeros64, qh], axis=1))
        return jnp.concatenate(pieces, axis=0)

    qs = each(stacked_q, groups)
    qs_b = each(lambda q: (q * LOG2E).astype(BF16), qs)

    valid_c = (key_off * NSA_CMP_STRIDE + (NSA_CMP_BLOCK - 1) <= t_pos) & (key_off < n_cmp)
    s_c = each(lambda q: jnp.where(valid_c, _dot(kc_ref[0], q, _NT, hi=True), MASK_NEG), qs)
    e_c = each(lambda x: jnp.exp(x - jnp.max(x, axis=0, keepdims=True)), s_c)
    p_c = each(lambda e: jnp.where(valid_c, e / jnp.sum(e, axis=0, keepdims=True), 0.0), e_c)
    o_cmp = each(lambda p, g: _dot(vc_ref[0], p, _TN)[HEAD_DIM * g:HEAD_DIM * (g + 1), :], p_c, groups)
    p_sum = each(lambda p: p[:, 0:QBLK] + p[:, QBLK:2 * QBLK] + p[:, 2 * QBLK:3 * QBLK] + p[:, 3 * QBLK:], p_c)
    imp = each(lambda p: _dot(ovt_ref[...], p, hi=True), p_sum)

    j = lax.broadcasted_iota(jnp.int32, (n_sel_blk, QBLK), 0)
    tq = qi * QBLK + lax.broadcasted_iota(jnp.int32, (n_sel_blk, QBLK), 1)
    cur = tq // NSA_SEL_BLOCK
    forced = (j == 0) | (j == cur) | (j == cur - 1)
    score = each(lambda x: jnp.where(forced, 1e9, jnp.where(j > cur, -1e9, x)), imp)
    rank = [jnp.zeros((n_sel_blk, QBLK), F32) for _ in groups]
    for i in range(n_sel_blk):
        rank = each(lambda rk, sc: rk + jnp.where(
            (sc[i:i + 1, :] > sc) | ((sc[i:i + 1, :] == sc) & (j > i)), 1.0, 0.0), rank, score)
    sel = each(lambda rk: jnp.where(rk < NSA_N_SELECT, 1.0, 0.0), rank)

    def attend(k_ref, slabs, kb0, n_kb, ms, bias=None, window=False, causal=True):
        k0 = pl.multiple_of(kb0 * QBLK, QBLK)
        k_blk = k_ref[0, pl.ds(k0, n_kb * QBLK), :]
        s = each(lambda q: lax.dot_general(k_blk, q, _NT, preferred_element_type=F32), qs_b)
        if bias is not None:
            s = each(lambda x, bb: x + bb, s, bias)
        if causal:
            key_pos = k0 + lax.broadcasted_iota(jnp.int32, (n_kb * QBLK, cols), 0)
            t_all = jnp.concatenate([t_pos] * n_kb, axis=0)
            valid = key_pos <= t_all
            if window:
                valid = valid & (key_pos > t_all - NSA_WINDOW)
            s = each(lambda x: jnp.where(valid, x, MASK_NEG), s)
        m_new = each(lambda m, x: jnp.maximum(m, jnp.max(x, axis=0, keepdims=True)), ms, s)
        p = each(lambda x, m: jnp.exp2(x - m).astype(BF16), s, m_new)
        pv = [jnp.zeros((VT_ROWS, cols), F32) for _ in groups]
        for i in range(n_kb):
            pv = each(lambda acc, pp, slab: acc + jnp.dot(
                vt_ref[0, kb0 + i, slab * VT_ROWS:(slab + 1) * VT_ROWS, :], pp[QBLK * i:QBLK * (i + 1), :],
                preferred_element_type=F32), pv, p, slabs)
        for slab, m, mn, x in zip(slabs, ms, m_new, pv):
            acc_ref[slab] = acc_ref[slab] * jnp.exp2(m - mn) + x
        return tuple(m_new)

    def normalised(slab):
        return acc_ref[slab, 0:HEAD_DIM, :] / acc_ref[slab, HEAD_DIM:HEAD_DIM + 1, :]

    acc_ref[...] = jnp.zeros_like(acc_ref)
    m0s = (m0,) * NSA_KV_HEADS

    def slc_step(i, ms, causal=False):
        chosen = each(lambda sl: _dot(ext_ref[i], sl), sel)
        bias = each(lambda ch: jnp.concatenate([jnp.where(ch > 0.5, 0.0, MASK_NEG)] * NSA_HPG, axis=1), chosen)
        return attend(ks_ref, groups, i * SLC_KB, SLC_KB, ms, bias=bias, causal=causal)

    ms = lax.fori_loop(0, qi // SLC_KB, slc_step, m0s)
    slc_step(qi // SLC_KB, ms, causal=True)

    n_win = NSA_WINDOW // QBLK + 1
    attend(kw_ref, [NSA_KV_HEADS + g for g in groups], jnp.maximum(qi - (n_win - 1), 0), n_win, m0s, window=True)

    o_groups = []
    for g in groups:
        def gate(branch):
            return jnp.concatenate(
                [gt_t[(g * NSA_HPG + hp) * 3 + branch:(g * NSA_HPG + hp) * 3 + branch + 1, :]
                 for hp in range(NSA_HPG)], axis=1)

        o_groups.append(gate(0) * o_cmp[g] + gate(1) * normalised(g) + gate(2) * normalised(NSA_KV_HEADS + g))

    o_all = jnp.concatenate(o_groups, axis=0)
    for hp in range(NSA_HPG):
        tile = o_all[:, QBLK * hp:QBLK * (hp + 1)].T
        for g in range(NSA_KV_HEADS):
            c0 = (g * NSA_HPG + hp) * HEAD_DIM
            o_ref[0, :, c0:c0 + HEAD_DIM] = tile[:, HEAD_DIM * g:HEAD_DIM * (g + 1)]


def _nsa_attn(qr, kk, vt, gt, kcmp, vcmp):
    bsz, s, _ = qr.shape
    n_blk = s // NSA_CMP_STRIDE
    n_cmp = n_blk - NSA_CMP_BLOCK // NSA_CMP_STRIDE + 1
    n_sel = s // NSA_SEL_BLOCK
    assert n_blk == QBLK, "compressed-score tiles are laid out one stride block per sublane row"
    c0 = np.arange(n_blk) * NSA_CMP_STRIDE
    s0 = np.arange(n_sel) * NSA_SEL_BLOCK
    lo = np.maximum(c0[:, None], s0[None, :])
    hi = np.minimum(c0[:, None] + NSA_CMP_BLOCK, s0[None, :] + NSA_SEL_BLOCK)
    overlap = (np.maximum(hi - lo, 0) / NSA_CMP_STRIDE).astype(np.float32)
    overlap[n_cmp:] = 0.0
    key = np.arange(s)
    expand = (key[:, None] // NSA_SEL_BLOCK == np.arange(n_sel)[None, :]).astype(np.float32)
    expand = expand.reshape(s // (SLC_KB * QBLK), SLC_KB * QBLK, n_sel)
    consts = [jnp.asarray(overlap.T), jnp.asarray(expand, BF16)]

    cmp_spec = pl.BlockSpec((1, n_blk, LANES), lambda b, i: (b, 0, 0))
    return pl.pallas_call(
        functools.partial(_nsa_attn_kernel, n_cmp=n_cmp),
        grid=(bsz, s // QBLK),
        in_specs=[pl.BlockSpec((1, QBLK, NSA_WIDTH), lambda b, i: (b, i, 0)),
                  pl.BlockSpec((1, QBLK, LANES), lambda b, i: (b, i, 0)),
                  cmp_spec, cmp_spec,
                  pl.BlockSpec((1, s, LANES), lambda b, i: (b, 0, 0)),
                  pl.BlockSpec((1, s, LANES), lambda b, i: (b, 0, 1)),
                  pl.BlockSpec((1, s // QBLK, 4 * VT_ROWS, QBLK), lambda b, i: (b, 0, 0, 0))]
        + [_full(a.shape) for a in consts],
        out_specs=pl.BlockSpec((1, QBLK, NSA_WIDTH), lambda b, i: (b, i, 0)),
        out_shape=jax.ShapeDtypeStruct((bsz, s, NSA_WIDTH), F32),
        scratch_shapes=[pltpu.VMEM((2 * NSA_KV_HEADS, VT_ROWS, NSA_HPG * QBLK), F32)],
        compiler_params=_params("parallel", "arbitrary"),
    )(qr, gt, kcmp, vcmp, kk, kk, vt, *consts)


def _out_ln_kernel(x_ref, og_ref, or_ref, on_ref, w_ref, lw_ref, lb_ref, o_ref, *, alpha):
    mix = (_dot(og_ref[...], w_ref[0:LIN_WIDTH, :])
           + _dot(or_ref[...], w_ref[LIN_WIDTH:2 * LIN_WIDTH, :])
           + _dot(on_ref[...], w_ref[2 * LIN_WIDTH:, :]))
    o_ref[...] = _layer_norm(alpha * x_ref[...] + mix, lw_ref[...], lb_ref[...])


def _out_ln(x2, og, orw, on, w, lw, lb, alpha, tm=512):
    t = x2.shape[0]

    def rows(width):
        return pl.BlockSpec((tm, width), lambda i: (i, 0))

    return pl.pallas_call(
        functools.partial(_out_ln_kernel, alpha=alpha),
        grid=(t // tm,),
        in_specs=[rows(D_MODEL), rows(LIN_WIDTH), rows(LIN_WIDTH), rows(NSA_WIDTH),
                  _full(w.shape), _full(lw.shape), _full(lb.shape)],
        out_specs=rows(D_MODEL),
        out_shape=jax.ShapeDtypeStruct((t, D_MODEL), F32),
        compiler_params=_params("parallel"),
    )(x2, og, orw, on, w, lw, lb)


def _ffn_ln_kernel(x_ref, wg_ref, wu_ref, wd_ref, lw_ref, lb_ref, o_ref, acc_ref, *, alpha):
    j = pl.program_id(1)
    x = x_ref[...]
    xb = x.astype(BF16)
    gate = jnp.dot(xb, wg_ref[...], preferred_element_type=F32)
    up = jnp.dot(xb, wu_ref[...], preferred_element_type=F32)
    part = _dot(gate * _sigmoid(gate) * up, wd_ref[...])

    @pl.when(j == 0)
    def _():
        acc_ref[...] = part

    @pl.when(j > 0)
    def _():
        acc_ref[...] += part

    @pl.when(j == pl.num_programs(1) - 1)
    def _():
        o_ref[...] = _layer_norm(alpha * x + acc_ref[...], lw_ref[...], lb_ref[...])


def _ffn_ln(x2, wg, wu, wd, lw, lb, alpha, tm=512, th=1408):
    t = x2.shape[0]
    return pl.pallas_call(
        functools.partial(_ffn_ln_kernel, alpha=alpha),
        grid=(t // tm, FFN_HIDDEN // th),
        in_specs=[pl.BlockSpec((tm, D_MODEL), lambda i, j: (i, 0)),
                  pl.BlockSpec((D_MODEL, th), lambda i, j: (0, j)),
                  pl.BlockSpec((D_MODEL, th), lambda i, j: (0, j)),
                  pl.BlockSpec((th, D_MODEL), lambda i, j: (j, 0)),
                  _full(lw.shape), _full(lb.shape)],
        out_specs=pl.BlockSpec((tm, D_MODEL), lambda i, j: (i, 0)),
        out_shape=jax.ShapeDtypeStruct((t, D_MODEL), F32),
        scratch_shapes=[pltpu.VMEM((tm, D_MODEL), F32)],
        compiler_params=_params("parallel", "arbitrary"),
    )(x2, wg, wu, wd, lw, lb)


def _pad_cols(w, width):
    return jnp.pad(w, ((0, 0), (0, width - w.shape[1])))


def _pad_rows(w, height):
    return jnp.pad(w, ((0, height - w.shape[0]), (0, 0)))


def _rwkv_slab_cols(w, w_vres):
    o = 3 * LIN_WIDTH
    parts = [w[:, :o],
             _pad_cols(w[:, o:o + RWKV_DECAY_RANK], LANES),
             _pad_cols(w[:, o + RWKV_DECAY_RANK:o + RWKV_DECAY_RANK + RWKV_ICLR_RANK], LANES),
             _pad_cols(w[:, o + RWKV_DECAY_RANK + RWKV_ICLR_RANK:], 2 * LANES),
             _pad_cols(w_vres, LANES)]
    return jnp.concatenate(parts, axis=1)


def _rope_tables(s):
    half = HEAD_DIM // 2
    inv = ROPE_THETA ** (-jnp.arange(half, dtype=F32) / half)
    ang = jnp.arange(s, dtype=jnp.int32).astype(F32)[:, None] * inv
    cos, sin = jnp.cos(ang), jnp.sin(ang)
    cos2 = jnp.tile(jnp.concatenate([cos, cos], axis=1), (1, LANES // HEAD_DIM))
    sin2 = jnp.tile(jnp.concatenate([-sin, sin], axis=1), (1, LANES // HEAD_DIM))
    i = np.arange(LANES)
    src = np.where(i % HEAD_DIM < half, i + half, i - half)
    rot = np.zeros((LANES, LANES), np.float32)
    rot[src, i] = 1.0
    return cos2, sin2, jnp.asarray(rot)


def kernel(x, w_in, w_in_vres, gla_w_a2, gla_b_a, gla_ln_w, gla_ln_b, rwkv_mu, rwkv_mu_vres, rwkv_w0, rwkv_w2, rwkv_a0, rwkv_a2, rwkv_v0, rwkv_v2, rwkv_g2, rwkv_k_k, rwkv_k_a, rwkv_r_k, rwkv_ln_w, rwkv_ln_b, nsa_pos_k, nsa_pos_v, nsa_wk1, nsa_wk2, nsa_wv1, nsa_wv2, w_out, ln1_w, ln1_b, ffn_w_gate, ffn_w_up, ffn_w_down, ln2_w, ln2_b):
    bsz, s, d = x.shape
    depth = w_in.shape[0]
    alpha = float((2 * depth) ** 0.25)
    gla_cols = 4 * LIN_WIDTH + GLA_GATE_RANK
    rwkv_cols = 3 * LIN_WIDTH + RWKV_DECAY_RANK + RWKV_ICLR_RANK + RWKV_GATE_RANK
    cos2, sin2, rot = _rope_tables(s)
    row = lambda a: a.reshape(1, -1)

    x2 = x.reshape(bsz * s, d)
    v_first = None
    for l in range(depth):
        w = w_in[l]
        w_r = w[:, gla_cols:gla_cols + rwkv_cols]
        if l == 0:
            w_vres = jnp.zeros((d, RWKV_VRES_RANK), F32)
            mu_vres = jnp.zeros((1, RWKV_VRES_RANK), F32)
        else:
            w_vres = w_in_vres[l - 1]
            mu_vres = row(rwkv_mu_vres[l - 1])
        wg = _pad_cols(w[:, :gla_cols], GLA_SLAB).astype(BF16)
        wr = _rwkv_slab_cols(w_r, w_vres).astype(BF16)
        wn = _pad_cols(w[:, gla_cols + rwkv_cols:], NSA_SLAB).astype(BF16)
        hg, hr, hn = _proj(x2, wg, wr, wn)
        hg = hg.reshape(bsz, s, GLA_SLAB)
        hr = hr.reshape(bsz, s, RWKV_SLAB)
        hn = hn.reshape(bsz, s, NSA_SLAB)

        o_gla = _gla(hg, _pad_rows(gla_w_a2[l], LANES), row(gla_b_a[l]), row(gla_ln_w[l]), row(gla_ln_b[l]))

        mu = _rwkv_slab_cols(row(rwkv_mu[l]), mu_vres)
        small = [mu, row(rwkv_w0[l]), _pad_rows(rwkv_w2[l], LANES), row(rwkv_a0[l]),
                 _pad_rows(rwkv_a2[l], LANES), _pad_rows(rwkv_g2[l], 2 * LANES), row(rwkv_k_k[l]),
                 row(rwkv_k_a[l]), row(rwkv_r_k[l]), row(rwkv_ln_w[l]), row(rwkv_ln_b[l])]
        if l == 0:
            o_rwkv, v_first = _rwkv(hr, None, small)
        else:
            small += [row(rwkv_v0[l - 1]), _pad_rows(rwkv_v2[l - 1], LANES)]
            o_rwkv = _rwkv(hr, v_first, small)

        qr, kvc, kk, vt, gt = _nsa_prep(hn, cos2, sin2, rot)
        kcmp, vcmp = _nsa_cmp(kvc, nsa_pos_k[l], nsa_pos_v[l], nsa_wk1[l], nsa_wk2[l], nsa_wv1[l], nsa_wv2[l])
        o_nsa = _nsa_attn(qr, kk, vt, gt, kcmp, vcmp)

        x2 = _out_ln(x2, o_gla.reshape(bsz * s, LIN_WIDTH), o_rwkv.reshape(bsz * s, LIN_WIDTH),
                     o_nsa.reshape(bsz * s, NSA_WIDTH), w_out[l].astype(BF16), row(ln1_w[l]), row(ln1_b[l]), alpha)
        x2 = _ffn_ln(x2, ffn_w_gate[l].astype(BF16), ffn_w_up[l].astype(BF16), ffn_w_down[l].astype(BF16),
                     row(ln2_w[l]), row(ln2_b[l]), alpha)
    return x2.reshape(bsz, s, d)
```

```python
import functools

import numpy as np
import jax
import jax.numpy as jnp
from jax import lax
from jax.experimental import pallas as pl
from jax.experimental.pallas import tpu as pltpu

F32 = jnp.float32
BF16 = jnp.bfloat16
HIGHEST = lax.Precision.HIGHEST

D_MODEL = 1024
HEAD_DIM = 64
N_LIN_HEADS = 4
LIN_WIDTH = N_LIN_HEADS * HEAD_DIM
GLA_GATE_RANK = 16
GLA_TAU = 16.0
RWKV_DECAY_RANK = 64
RWKV_ICLR_RANK = 64
RWKV_VRES_RANK = 32
RWKV_GATE_RANK = 160
RWKV_GN_EPS = 64e-5
NSA_HEADS = 8
NSA_KV_HEADS = 2
NSA_HPG = NSA_HEADS // NSA_KV_HEADS
NSA_WIDTH = NSA_HEADS * HEAD_DIM
NSA_KV_WIDTH = NSA_KV_HEADS * HEAD_DIM
NSA_CMP_BLOCK = 32
NSA_CMP_STRIDE = 16
NSA_CMP_HIDDEN = 256
NSA_SEL_BLOCK = 64
NSA_N_SELECT = 16
NSA_WINDOW = 512
ROPE_THETA = 10000.0
FFN_HIDDEN = 2816
LN_EPS = 1e-5
MASK_NEG = -1e30

LANES = 128
CHUNK = 64
SUB = 16
RWKV_GROUP = 4
GLA_GROUP = 4
RWKV_HB = 2
RW_BW = RWKV_HB * HEAD_DIM
QBLK = 128
SLC_KB = 4
VT_ROWS = HEAD_DIM + 16
LOG2E = 1.4426950408889634
VMEM_LIMIT = 56 * 1024 * 1024

GLA_SLAB = 4 * LIN_WIDTH + LANES
RWKV_SLAB = 3 * LIN_WIDTH + LANES + LANES + 2 * LANES + LANES
NSA_SLAB = NSA_WIDTH + 6 * NSA_KV_WIDTH + LANES
RW_WLR, RW_ALR, RW_GLR, RW_VLR = 768, 896, 1024, 1280


def _dot(a, b, dims=None, hi=False):
    dims = dims or (((a.ndim - 1,), (0,)), ((), ()))
    if hi:
        return lax.dot_general(a, b, dims, precision=HIGHEST, preferred_element_type=F32)
    return lax.dot_general(a.astype(BF16), b.astype(BF16), dims, preferred_element_type=F32)


def _split_bf16(a):
    hi = a.astype(BF16)
    return hi, (a - hi.astype(F32)).astype(BF16)


def _dot3(a, b_hi, b_lo):
    a_hi, a_lo = _split_bf16(a)
    return (jnp.dot(a_hi, b_hi, preferred_element_type=F32)
            + (jnp.dot(a_lo, b_hi, preferred_element_type=F32) + jnp.dot(a_hi, b_lo, preferred_element_type=F32)))


_NT = (((1,), (1,)), ((), ()))
_TN = (((0,), (0,)), ((), ()))


def _sigmoid(x):
    return 1.0 / (1.0 + jnp.exp(-x))


def _softplus(x):
    return jnp.maximum(x, 0.0) + jnp.log(1.0 + jnp.exp(-jnp.abs(x)))


def _head_norm(x, w, b, eps):
    mu = jnp.mean(x, axis=-1, keepdims=True)
    xc = x - mu
    var = jnp.mean(xc * xc, axis=-1, keepdims=True)
    return xc * lax.rsqrt(var + eps) * w + b


def _layer_norm(x, w, b):
    mu = jnp.mean(x, axis=-1, keepdims=True)
    xc = x - mu
    var = jnp.mean(xc * xc, axis=-1, keepdims=True)
    return xc * lax.rsqrt(var + LN_EPS) * w + b


def _params(*sem):
    return pltpu.CompilerParams(dimension_semantics=sem, vmem_limit_bytes=VMEM_LIMIT)


def _full(shape):
    nd = len(shape)
    return pl.BlockSpec(shape, lambda *_: (0,) * nd)


def _proj_kernel(x_ref, wg_ref, wr_ref, wn_ref, og_ref, or_ref, on_ref):
    xb = x_ref[...].astype(BF16)
    og_ref[...] = jnp.dot(xb, wg_ref[...], preferred_element_type=F32)
    or_ref[...] = jnp.dot(xb, wr_ref[...], preferred_element_type=F32)
    on_ref[...] = jnp.dot(xb, wn_ref[...], preferred_element_type=F32)


def _proj(x2, wg, wr, wn, tm=256):
    t = x2.shape[0]
    return pl.pallas_call(
        _proj_kernel,
        grid=(t // tm,),
        in_specs=[pl.BlockSpec((tm, D_MODEL), lambda i: (i, 0)),
                  _full(wg.shape), _full(wr.shape), _full(wn.shape)],
        out_specs=[pl.BlockSpec((tm, GLA_SLAB), lambda i: (i, 0)),
                   pl.BlockSpec((tm, RWKV_SLAB), lambda i: (i, 0)),
                   pl.BlockSpec((tm, NSA_SLAB), lambda i: (i, 0))],
        out_shape=[jax.ShapeDtypeStruct((t, GLA_SLAB), F32),
                   jax.ShapeDtypeStruct((t, RWKV_SLAB), F32),
                   jax.ShapeDtypeStruct((t, NSA_SLAB), F32)],
        compiler_params=_params("parallel"),
    )(x2, wg, wr, wn)


def _group_rows(x, u):
    return jnp.concatenate(
        [jnp.broadcast_to(x[SUB * i + u:SUB * i + u + 1, :], (SUB, x.shape[1]))
         for i in range(CHUNK // SUB)], axis=0)


def _gla_kernel(h_ref, wa2_ref, ba_ref, lnw_ref, lnb_ref, tril_ref, blk_ref, ones_ref, dm_ref, same_ref,
                same_sub_ref, o_ref, st_ref, *, n_chunks):
    @pl.when(pl.program_id(1) == 0)
    def _():
        st_ref[...] = jnp.zeros_like(st_ref)

    row_in_sub = lax.broadcasted_iota(jnp.int32, (CHUNK, LIN_WIDTH), 0) % SUB
    dm = dm_ref[...]
    heads = range(N_LIN_HEADS)
    n_sub = CHUNK // SUB

    def own_head(m, mask_ref):
        return jnp.concatenate([m] * N_LIN_HEADS, axis=0) * mask_ref[...]

    def each(f, *lists):
        return [f(*args) for args in zip(*lists)]

    def chunk_group(gi, carry):
        rows = [pl.ds(pl.multiple_of((gi * GLA_GROUP + j) * CHUNK, CHUNK), CHUNK) for j in range(GLA_GROUP)]
        ps = [h_ref[0, rw, :] for rw in rows]
        q = each(lambda p: p[:, 0:256] * (HEAD_DIM ** -0.5), ps)
        k = each(lambda p: p[:, 256:512], ps)
        v = each(lambda p: p[:, 512:768], ps)
        g = each(lambda p: p[:, 768:1024], ps)
        z = each(lambda p: _dot(p[:, 1024:1152], wa2_ref[...], hi=True) + ba_ref[...], ps)
        log_a = each(lambda x: -_softplus(-x) * (1.0 / GLA_TAU), z)
        b = each(lambda x: _dot(tril_ref[...], x, hi=True), log_a)
        b_end = each(lambda x: _dot(blk_ref[...], x, hi=True), log_a)
        qe = each(lambda x, bb: x * jnp.exp(bb), q, b)
        kd = each(lambda x, be, bb: x * jnp.exp(be - bb), k, b_end, b)
        p_end = each(jnp.exp, b_end)
        v_st = each(lambda x: jnp.concatenate([x[:, HEAD_DIM * h:HEAD_DIM * (h + 1)] for h in heads], axis=0), v)

        at = [jnp.zeros((CHUNK, LIN_WIDTH), F32) for _ in ps]
        for u0 in range(0, SUB, N_LIN_HEADS):
            us = range(u0, u0 + N_LIN_HEADS)
            es = each(lambda bb, qq, kk: jnp.concatenate(
                [jnp.exp(jnp.where(row_in_sub <= u, _group_rows(bb, u) - bb, MASK_NEG)) * (_group_rows(qq, u) * kk)
                 for u in us], axis=0), b, q, k)
            a_rep = each(lambda e: _dot(e, ones_ref[...]), es)
            for i, u in enumerate(us):
                at = each(lambda acc, ar: acc + jnp.where(dm == u, ar[CHUNK * i:CHUNK * (i + 1), :], 0.0), at, a_rep)
        o_st = each(lambda x, vs: _dot(own_head(x, same_ref), vs, _TN), at, v_st)

        upd, q_own = [], []
        for i in range(n_sub):
            rs = slice(SUB * i, SUB * (i + 1))
            upd.append(each(lambda vs, kk: _dot(
                jnp.concatenate([vs[CHUNK * h + SUB * i:CHUNK * h + SUB * (i + 1), :] for h in heads], axis=0),
                own_head(kk[rs], same_sub_ref), _TN), v_st, kd))
            q_own.append(each(lambda x: own_head(x[rs], same_sub_ref), qe))
        st = st_ref[...]
        for c, rw in enumerate(rows):
            inter = []
            for i in range(n_sub):
                inter.append(_dot(q_own[i][c], st, _NT))
                st = st * p_end[c][SUB * i:SUB * i + 1, :] + upd[i][c]
            o_c = o_st[c] + jnp.concatenate(
                [inter[i][SUB * h:SUB * (h + 1), :] for h in heads for i in range(n_sub)], axis=0)
            o_c = _head_norm(o_c, 1.0, 0.0, LN_EPS)
            o = jnp.concatenate([o_c[CHUNK * h:CHUNK * (h + 1), :] for h in heads], axis=1)
            o_ref[0, rw, :] = (o * lnw_ref[...] + lnb_ref[...]) * (g[c] * _sigmoid(g[c]))
        st_ref[...] = st
        return carry

    lax.fori_loop(0, n_chunks // GLA_GROUP, chunk_group, 0)


def _gla(hg, wa2, ba, lnw, lnb, ts=512):
    bsz, s, _ = hg.shape
    r = np.arange(CHUNK)
    tril = ((r[:, None] // SUB == r[None, :] // SUB) & (r[None, :] <= r[:, None])).astype(np.float32)
    blk = (r[:, None] // SUB == r[None, :] // SUB).astype(np.float32)
    c = np.arange(LIN_WIDTH)
    same = (c[:, None] // HEAD_DIM == c[None, :] // HEAD_DIM).astype(np.float32)
    same_sub = (r[:, None] // SUB == c[None, :] // HEAD_DIM).astype(np.float32)
    dm = ((c[None, :] % HEAD_DIM) - SUB * (r[:, None] // SUB)).astype(np.int32)
    consts = [jnp.asarray(tril), jnp.asarray(blk), jnp.asarray(same, BF16), jnp.asarray(dm), jnp.asarray(same),
              jnp.asarray(same_sub)]
    small = [wa2, ba, lnw, lnb] + consts
    return pl.pallas_call(
        functools.partial(_gla_kernel, n_chunks=ts // CHUNK),
        grid=(bsz, s // ts),
        in_specs=[pl.BlockSpec((1, ts, GLA_SLAB), lambda b, j: (b, j, 0))] + [_full(a.shape) for a in small],
        out_specs=pl.BlockSpec((1, ts, LIN_WIDTH), lambda b, j: (b, j, 0)),
        out_shape=jax.ShapeDtypeStruct((bsz, s, LIN_WIDTH), F32),
        scratch_shapes=[pltpu.VMEM((HEAD_DIM, LIN_WIDTH), F32)],
        compiler_params=_params("parallel", "arbitrary"),
    )(hg, *small)


def _rwkv_kernel(*refs, n_chunks, has_vres):
    if has_vres:
        (h_ref, vf_ref, mu_ref, w0_ref, w2_ref, a0_ref, a2_ref, g2_ref, kk_ref, ka_ref, rk_ref,
         lnw_ref, lnb_ref, v0_ref, v2_ref, tril_ref, ones_ref, same_ref, strict_ref, incl_ref, eye_ref,
         o_ref, st_ref, prev_ref) = refs
    else:
        (h_ref, mu_ref, w0_ref, w2_ref, a0_ref, a2_ref, g2_ref, kk_ref, ka_ref, rk_ref,
         lnw_ref, lnb_ref, tril_ref, ones_ref, same_ref, strict_ref, incl_ref, eye_ref,
         o_ref, vf_ref, st_ref, prev_ref) = refs

    @pl.when(pl.program_id(1) == 0)
    def _():
        st_ref[...] = jnp.zeros_like(st_ref)
        prev_ref[...] = jnp.zeros_like(prev_ref)

    row = lax.broadcasted_iota(jnp.int32, (CHUNK, RWKV_SLAB), 0)
    blocks = range(N_LIN_HEADS // RWKV_HB)
    in_block = range(RWKV_HB)
    nr = RWKV_HB * CHUNK

    def tile_rows(m):
        return jnp.concatenate([m] * RWKV_HB, axis=0)

    def own_head(m):
        return tile_rows(m) * same_ref[...]

    def stack(m):
        return jnp.concatenate([m[:, HEAD_DIM * h:HEAD_DIM * (h + 1)] for h in in_block], axis=0)

    def unstack(m):
        return jnp.concatenate([m[CHUNK * h:CHUNK * (h + 1), :] for h in in_block], axis=1)

    def each(f, *lists):
        return [f(*args) for args in zip(*lists)]

    def per_block(xs):
        return [x[:, RW_BW * bi:RW_BW * (bi + 1)] for x in xs for bi in blocks]

    def prepare(ps, prev_rows, vfs):
        prev = each(lambda p, pr: jnp.where(row == 0, pr, pltpu.roll(p, 1, 0)), ps, prev_rows)
        xm = each(lambda p, pv: p + (pv - p) * mu_ref[...], ps, prev)
        r = each(lambda x: x[:, 0:256], xm)
        k = each(lambda x: x[:, 256:512], xm)
        v = each(lambda x: x[:, 512:768], xm)
        v_shift = v
        w_log = each(lambda x: -_softplus(-(w0_ref[...] + _dot(jnp.tanh(x[:, RW_WLR:RW_ALR]), w2_ref[...], hi=True)))
                     - 0.5, xm)
        lw = each(lambda w: -jnp.exp(w), w_log)
        a = each(lambda x: _sigmoid(a0_ref[...] + _dot(x[:, RW_ALR:RW_GLR], a2_ref[...], hi=True)), xm)
        g = each(lambda x: _dot(_sigmoid(x[:, RW_GLR:RW_VLR]), g2_ref[...], hi=True), xm)
        if has_vres:
            mixv = each(lambda x: _sigmoid(v0_ref[...] + _dot(x[:, RW_VLR:RWKV_SLAB], v2_ref[...], hi=True)), xm)
            v = each(lambda vv, vf, mx: vv + (vf - vv) * mx, v, vfs, mixv)
        kkv = each(lambda kk: kk * kk_ref[...], k)
        nrm = each(lambda kv: jnp.sqrt(_dot(kv * kv, ones_ref[...], hi=True)), kkv)
        kkn = each(lambda kv, n: kv / jnp.maximum(n, 1e-12), kkv, nrm)
        k2 = each(lambda kk, aa: kk * (1.0 + (aa - 1.0) * ka_ref[...]), k, a)
        bv = each(lambda kn, aa: kn * aa, kkn, a)
        c = each(lambda w: _dot(tril_ref[...], w, hi=True), lw)
        c_end = each(lambda cc: cc[CHUNK - 1:CHUNK, :], c)
        bonus = each(lambda rr, kk, vv: _dot(rr * kk * rk_ref[...], ones_ref[...], hi=True) * vv, r, k2, v)
        at = each(lambda kn, cc, w: -kn * jnp.exp(cc - w), kkn, c, lw)
        rt = each(lambda rr, cc: rr * jnp.exp(cc), r, c)
        bt = each(lambda b_, cc: b_ * jnp.exp(-cc), bv, c)
        kt = each(lambda kk, cc: kk * jnp.exp(-cc), k2, c)
        b_rest = each(lambda b_, ce, cc: b_ * jnp.exp(ce - cc), bv, c_end, c)
        k_rest = each(lambda kk, ce, cc: kk * jnp.exp(ce - cc), k2, c_end, c)
        p_end = each(jnp.exp, c_end)

        ar = each(lambda x, y: jnp.concatenate([own_head(x), own_head(y)], axis=0), per_block(at), per_block(rt))
        bk = each(lambda x, y: jnp.concatenate([tile_rows(x), tile_rows(y)], axis=0), per_block(bt), per_block(kt))
        gm = each(lambda x, y: _dot(x, y, _NT), ar, bk)
        n_ab = each(lambda m: m[0:nr, 0:nr] * strict_ref[...], gm)
        m_ak = each(lambda m: m[0:nr, nr:2 * nr] * strict_ref[...], gm)
        r_bk = each(lambda m: m[nr:2 * nr, :] * jnp.concatenate([incl_ref[...], incl_ref[...]], axis=1), gm)
        v_st = each(stack, per_block(v))
        mv = each(_dot, m_ak, v_st)
        inv = each(lambda n: eye_ref[...] + n, n_ab)
        pw = n_ab
        for _ in range(5):
            pw = each(lambda x: _dot(x, x), pw)
            inv = each(lambda i_, x: i_ + _dot(i_, x), inv, pw)
        bk_rest = each(lambda x, y: jnp.concatenate([own_head(x), own_head(y)], axis=0),
                       per_block(b_rest), per_block(k_rest))
        items = list(zip(ar, mv, inv, v_st, r_bk, bk_rest, per_block(p_end)))
        n_blk = len(blocks)
        return [(items[n_blk * j:n_blk * (j + 1)], bonus[j], g[j]) for j in range(len(ps))], v_shift

    def advance(prep, s0):
        items, bonus, g = prep
        ar, mv, inv, v_st, r_bk, bk_rest, p_end = (list(x) for x in zip(*items))
        s0b = per_block([s0])
        a_s = each(lambda x, s_: _dot(x, s_, _NT), ar, s0b)
        u = each(lambda i_, x, m: _dot(i_, x[0:nr] + m), inv, a_s, mv)
        uv = each(lambda x, vs: jnp.concatenate([x, vs], axis=0), u, v_st)
        y = each(lambda x, rb, z: unstack(_head_norm(x[nr:2 * nr] + _dot(rb, z), 1.0, 0.0, RWKV_GN_EPS)), a_s, r_bk, uv)
        s1 = each(lambda s_, pe, z, br: s_ * pe + _dot(z, br, _TN), s0b, p_end, uv, bk_rest)
        y = jnp.concatenate(y, axis=1)
        return (y * lnw_ref[...] + lnb_ref[...] + bonus) * g, jnp.concatenate(s1, axis=1)

    def chunk_group(gi, carry):
        rows = [pl.ds(pl.multiple_of((gi * RWKV_GROUP + j) * CHUNK, CHUNK), CHUNK) for j in range(RWKV_GROUP)]
        ps = [h_ref[0, rw, :] for rw in rows]
        vfs = [vf_ref[0, rw, :] if has_vres else None for rw in rows]
        prev_rows = [prev_ref[...]] + [p[CHUNK - 1:CHUNK, :] for p in ps[:-1]]
        preps, shifted = prepare(ps, prev_rows, vfs)
        s = st_ref[...]
        for j, rw in enumerate(rows):
            out, s = advance(preps[j], s)
            o_ref[0, rw, :] = out
            if not has_vres:
                vf_ref[0, rw, :] = shifted[j]
        prev_ref[...] = ps[-1][CHUNK - 1:CHUNK, :]
        st_ref[...] = s
        return carry

    lax.fori_loop(0, n_chunks // RWKV_GROUP, chunk_group, 0)


def _rwkv(hr, v_first, small, ts=512):
    bsz, s, _ = hr.shape
    has_vres = v_first is not None
    r = np.arange(CHUNK)
    tril = (r[None, :] <= r[:, None]).astype(np.float32)
    i = np.arange(LIN_WIDTH)
    ones = (i[:, None] // HEAD_DIM == i[None, :] // HEAD_DIM).astype(np.float32)
    i = np.arange(RW_BW)
    same = (i[:, None] // HEAD_DIM == i[None, :] // HEAD_DIM).astype(np.float32)
    t_in, s_in = i[:, None] % CHUNK, i[None, :] % CHUNK
    consts = [tril, ones, same, same * (s_in < t_in), same * (s_in <= t_in), np.eye(RW_BW, dtype=np.float32)]
    small = list(small) + [jnp.asarray(c, F32) for c in consts]
    seq_spec = pl.BlockSpec((1, ts, LIN_WIDTH), lambda b, j: (b, j, 0))
    in_specs = [pl.BlockSpec((1, ts, RWKV_SLAB), lambda b, j: (b, j, 0))]
    args = [hr]
    if has_vres:
        in_specs.append(seq_spec)
        args.append(v_first)
    in_specs += [_full(a.shape) for a in small]
    out_sd = jax.ShapeDtypeStruct((bsz, s, LIN_WIDTH), F32)
    return pl.pallas_call(
        functools.partial(_rwkv_kernel, n_chunks=ts // CHUNK, has_vres=has_vres),
        grid=(bsz, s // ts),
        in_specs=in_specs,
        out_specs=seq_spec if has_vres else [seq_spec, seq_spec],
        out_shape=out_sd if has_vres else [out_sd, out_sd],
        scratch_shapes=[pltpu.VMEM((HEAD_DIM, LIN_WIDTH), F32),
                        pltpu.VMEM((1, RWKV_SLAB), F32)],
        compiler_params=_params("parallel", "arbitrary"),
    )(*args, *small)


def _nsa_prep_kernel(h_ref, cos_ref, sin_ref, q_ref, kvc_ref, kk_ref, vt_ref, gt_ref, *, n_qblk):
    cos = cos_ref[...]
    sin = sin_ref[...]
    half = HEAD_DIM // 2
    first_half = (lax.broadcasted_iota(jnp.int32, cos.shape, 1) & (HEAD_DIM - 1)) < half

    def rope(x):
        swapped = jnp.where(first_half, pltpu.roll(x, LANES - half, 1), pltpu.roll(x, half, 1))
        return x * cos + swapped * sin

    def col(i):
        return h_ref[0, :, NSA_WIDTH + LANES * i:NSA_WIDTH + LANES * (i + 1)]

    for i in range(NSA_WIDTH // LANES):
        cs = slice(LANES * i, LANES * (i + 1))
        q_ref[0, :, cs] = rope(h_ref[0, :, cs]) * (HEAD_DIM ** -0.5)
    kvc_ref[0, :, 0:LANES] = rope(col(0))
    kvc_ref[0, :, LANES:2 * LANES] = col(1)
    kk_ref[0, :, 0:LANES] = rope(col(2)).astype(BF16)
    kk_ref[0, :, LANES:2 * LANES] = rope(col(4)).astype(BF16)
    pad = VT_ROWS - HEAD_DIM
    ones_rows = jnp.where(lax.broadcasted_iota(jnp.int32, (pad, QBLK), 0) == 0, 1.0, 0.0).astype(BF16)
    for i in range(n_qblk):
        rs = slice(QBLK * i, QBLK * (i + 1))
        for branch, c in enumerate((3, 5)):
            v_t = col(c)[rs, :].T.astype(BF16)
            for g in range(NSA_KV_HEADS):
                r0 = (branch * NSA_KV_HEADS + g) * VT_ROWS
                vt_ref[0, i, r0:r0 + HEAD_DIM, :] = v_t[HEAD_DIM * g:HEAD_DIM * (g + 1), :]
                vt_ref[0, i, r0 + HEAD_DIM:r0 + VT_ROWS, :] = ones_rows
    gt_ref[0] = _sigmoid(h_ref[0, :, NSA_WIDTH + 6 * LANES:NSA_SLAB])


def _nsa_prep(hn, cos2, sin2, ts=512):
    bsz, s, _ = hn.shape
    n_qblk = ts // QBLK

    def seq(width):
        return pl.BlockSpec((1, ts, width), lambda b, j: (b, j, 0))

    return pl.pallas_call(
        functools.partial(_nsa_prep_kernel, n_qblk=n_qblk),
        grid=(bsz, s // ts),
        in_specs=[seq(NSA_SLAB),
                  pl.BlockSpec((ts, LANES), lambda b, j: (j, 0)),
                  pl.BlockSpec((ts, LANES), lambda b, j: (j, 0))],
        out_specs=[seq(NSA_WIDTH), seq(2 * LANES), seq(2 * LANES),
                   pl.BlockSpec((1, n_qblk, 4 * VT_ROWS, QBLK), lambda b, j: (b, j, 0, 0)),
                   seq(LANES)],
        out_shape=[jax.ShapeDtypeStruct((bsz, s, NSA_WIDTH), F32),
                   jax.ShapeDtypeStruct((bsz, s, 2 * LANES), F32),
                   jax.ShapeDtypeStruct((bsz, s, 2 * LANES), BF16),
                   jax.ShapeDtypeStruct((bsz, s // QBLK, 4 * VT_ROWS, QBLK), BF16),
                   jax.ShapeDtypeStruct((bsz, s, LANES), F32)],
        compiler_params=_params("parallel", "parallel"),
    )(hn, cos2, sin2)


def _nsa_cmp_kernel(kc_ref, vc_ref, pk_ref, pv_ref, wk1h_ref, wk1l_ref, wk2h_ref, wk2l_ref,
                    wv1h_ref, wv1l_ref, wv2h_ref, wv2l_ref, ko_ref, vo_ref, *, n_blk):
    half = NSA_CMP_BLOCK // 2

    def compress(t_ref, pos_ref, w1h_ref, w1l_ref, w2h_ref, w2l_ref, o_ref):
        for g in range(NSA_KV_HEADS):
            gs = slice(HEAD_DIM * g, HEAD_DIM * (g + 1))
            top = jnp.zeros((n_blk, NSA_CMP_HIDDEN), F32)
            bot = jnp.zeros((n_blk, NSA_CMP_HIDDEN), F32)
            for l in range(half):
                rows = t_ref[0, pl.ds(l, n_blk, stride=NSA_CMP_STRIDE), :][:, gs]
                top = top + _dot3(rows + pos_ref[l:l + 1, :], w1h_ref[l], w1l_ref[l])
                bot = bot + _dot3(rows + pos_ref[half + l:half + l + 1, :], w1h_ref[half + l], w1l_ref[half + l])
            hid = top + pltpu.roll(bot, n_blk - 1, 0)
            act = 0.5 * hid * (1.0 + jnp.tanh(0.7978845608028654 * (hid + 0.044715 * hid * hid * hid)))
            o_ref[0, :, gs] = _dot3(act, w2h_ref[...], w2l_ref[...])

    compress(kc_ref, pk_ref, wk1h_ref, wk1l_ref, wk2h_ref, wk2l_ref, ko_ref)
    compress(vc_ref, pv_ref, wv1h_ref, wv1l_ref, wv2h_ref, wv2l_ref, vo_ref)


def _nsa_cmp(kv, pos_k, pos_v, wk1, wk2, wv1, wv2):
    bsz, s, _ = kv.shape
    n_blk = s // NSA_CMP_STRIDE
    wk1 = wk1.reshape(NSA_CMP_BLOCK, HEAD_DIM, NSA_CMP_HIDDEN)
    wv1 = wv1.reshape(NSA_CMP_BLOCK, HEAD_DIM, NSA_CMP_HIDDEN)
    small = [pos_k, pos_v, *_split_bf16(wk1), *_split_bf16(wk2), *_split_bf16(wv1), *_split_bf16(wv2)]
    out_spec = pl.BlockSpec((1, n_blk, LANES), lambda b: (b, 0, 0))
    out_sd = jax.ShapeDtypeStruct((bsz, n_blk, LANES), F32)
    return pl.pallas_call(
        functools.partial(_nsa_cmp_kernel, n_blk=n_blk),
        grid=(bsz,),
        in_specs=[pl.BlockSpec((1, s, LANES), lambda b: (b, 0, 0)),
                  pl.BlockSpec((1, s, LANES), lambda b: (b, 0, 1))] + [_full(a.shape) for a in small],
        out_specs=[out_spec, out_spec],
        out_shape=[out_sd, out_sd],
        compiler_params=_params("parallel"),
    )(kv, kv, *small)


def _nsa_attn_kernel(q_ref, gt_ref, kc_ref, vc_ref, ks_ref, kw_ref, vt_ref, ovt_ref, ext_ref,
                     o_ref, acc_ref, *, n_cmp):
    qi = pl.program_id(1)
    cols = NSA_HPG * QBLK
    n_sel_blk = ovt_ref.shape[0]
    key_off = lax.broadcasted_iota(jnp.int32, (QBLK, cols), 0)
    t_pos = qi * QBLK + (lax.broadcasted_iota(jnp.int32, (QBLK, cols), 1) & (QBLK - 1))
    gt_t = gt_ref[0].T
    zeros64 = jnp.zeros((QBLK, HEAD_DIM), F32)
    m0 = jnp.full((1, cols), MASK_NEG, F32)

    groups = list(range(NSA_KV_HEADS))

    def each(f, *lists):
        return [f(*args) for args in zip(*lists)]

    def stacked_q(g):
        pieces = []
        for hp in range(NSA_HPG):
            qh = q_ref[0, :, (g * NSA_HPG + hp) * HEAD_DIM:(g * NSA_HPG + hp + 1) * HEAD_DIM]
            pieces.append(jnp.concatenate([qh, zeros64] if g == 0 else [zeros64, qh], axis=1))
        return jnp.concatenate(pieces, axis=0)

    qs = each(stacked_q, groups)
    qs_b = each(lambda q: (q * LOG2E).astype(BF16), qs)

    valid_c = (key_off * NSA_CMP_STRIDE + (NSA_CMP_BLOCK - 1) <= t_pos) & (key_off < n_cmp)
    s_c = each(lambda q: jnp.where(valid_c, _dot(kc_ref[0], q, _NT, hi=True), MASK_NEG), qs)
    e_c = each(lambda x: jnp.exp(x - jnp.max(x, axis=0, keepdims=True)), s_c)
    p_c = each(lambda e: jnp.where(valid_c, e / jnp.sum(e, axis=0, keepdims=True), 0.0), e_c)
    o_cmp = each(lambda p, g: _dot(vc_ref[0], p, _TN)[HEAD_DIM * g:HEAD_DIM * (g + 1), :], p_c, groups)
    p_sum = each(lambda p: p[:, 0:QBLK] + p[:, QBLK:2 * QBLK] + p[:, 2 * QBLK:3 * QBLK] + p[:, 3 * QBLK:], p_c)
    imp = each(lambda p: _dot(ovt_ref[...], p, hi=True), p_sum)

    j = lax.broadcasted_iota(jnp.int32, (n_sel_blk, QBLK), 0)
    tq = qi * QBLK + lax.broadcasted_iota(jnp.int32, (n_sel_blk, QBLK), 1)
    cur = tq // NSA_SEL_BLOCK
    forced = (j == 0) | (j == cur) | (j == cur - 1)
    score = each(lambda x: jnp.where(forced, 1e9, jnp.where(j > cur, -1e9, x)), imp)
    rank = [jnp.zeros((n_sel_blk, QBLK), F32) for _ in groups]
    for i in range(n_sel_blk):
        rank = each(lambda rk, sc: rk + jnp.where(
            (sc[i:i + 1, :] > sc) | ((sc[i:i + 1, :] == sc) & (j > i)), 1.0, 0.0), rank, score)
    sel = each(lambda rk: jnp.where(rk < NSA_N_SELECT, 1.0, 0.0), rank)

    def attend(chains):
        def scores(k_ref, g, slab, kb0, n_kb, m, bias, mask):
            k0 = pl.multiple_of(kb0 * QBLK, QBLK)
            s = lax.dot_general(k_ref[0, pl.ds(k0, n_kb * QBLK), :], qs_b[g], _NT, preferred_element_type=F32)
            if bias is not None:
                s = s + bias
            if mask is not None:
                key_pos = k0 + lax.broadcasted_iota(jnp.int32, (n_kb * QBLK, cols), 0)
                t_all = jnp.concatenate([t_pos] * n_kb, axis=0)
                valid = key_pos <= t_all
                if mask == "window":
                    valid = valid & (key_pos > t_all - NSA_WINDOW)
                s = jnp.where(valid, s, MASK_NEG)
            return s

        s = [scores(*ch) for ch in chains]
        ms = [ch[5] for ch in chains]
        m_new = each(lambda m, x: jnp.maximum(m, jnp.max(x, axis=0, keepdims=True)), ms, s)
        p = each(lambda x, m: jnp.exp2(x - m).astype(BF16), s, m_new)
        pv = [jnp.zeros((VT_ROWS, cols), F32) for _ in chains]
        for i in range(max(ch[4] for ch in chains)):
            pv = [acc + jnp.dot(vt_ref[0, ch[3] + i, ch[2] * VT_ROWS:(ch[2] + 1) * VT_ROWS, :],
                                pp[QBLK * i:QBLK * (i + 1), :], preferred_element_type=F32) if i < ch[4] else acc
                  for acc, pp, ch in zip(pv, p, chains)]
        for ch, m, mn, x in zip(chains, ms, m_new, pv):
            acc_ref[ch[2]] = acc_ref[ch[2]] * jnp.exp2(m - mn) + x
        return tuple(m_new)

    def normalised(slab):
        return acc_ref[slab, 0:HEAD_DIM, :] / acc_ref[slab, HEAD_DIM:HEAD_DIM + 1, :]

    acc_ref[...] = jnp.zeros_like(acc_ref)

    def slc_chains(i, ms, mask):
        chosen = each(lambda sl: _dot(ext_ref[i], sl), sel)
        bias = each(lambda ch: jnp.concatenate([jnp.where(ch > 0.5, 0.0, MASK_NEG)] * NSA_HPG, axis=1), chosen)
        return [(ks_ref, g, g, i * SLC_KB, SLC_KB, ms[g], bias[g], mask) for g in groups]

    ms = lax.fori_loop(0, qi // SLC_KB, lambda i, ms: attend(slc_chains(i, ms, None)), (m0,) * NSA_KV_HEADS)

    n_win = NSA_WINDOW // QBLK + 1
    kb_win = jnp.maximum(qi - (n_win - 1), 0)
    attend(slc_chains(qi // SLC_KB, ms, "causal")
           + [(kw_ref, g, NSA_KV_HEADS + g, kb_win, n_win, m0, None, "window") for g in groups])

    o_groups = []
    for g in groups:
        def gate(branch):
            return jnp.concatenate(
                [gt_t[(g * NSA_HPG + hp) * 3 + branch:(g * NSA_HPG + hp) * 3 + branch + 1, :]
                 for hp in range(NSA_HPG)], axis=1)

        o_groups.append(gate(0) * o_cmp[g] + gate(1) * normalised(g) + gate(2) * normalised(NSA_KV_HEADS + g))

    o_all = jnp.concatenate(o_groups, axis=0)
    for hp in range(NSA_HPG):
        tile = o_all[:, QBLK * hp:QBLK * (hp + 1)].T
        for g in range(NSA_KV_HEADS):
            c0 = (g * NSA_HPG + hp) * HEAD_DIM
            o_ref[0, :, c0:c0 + HEAD_DIM] = tile[:, HEAD_DIM * g:HEAD_DIM * (g + 1)]


def _nsa_attn(qr, kk, vt, gt, kcmp, vcmp):
    bsz, s, _ = qr.shape
    n_blk = s // NSA_CMP_STRIDE
    n_cmp = n_blk - NSA_CMP_BLOCK // NSA_CMP_STRIDE + 1
    n_sel = s // NSA_SEL_BLOCK
    assert n_blk == QBLK, "compressed-score tiles are laid out one stride block per sublane row"
    c0 = np.arange(n_blk) * NSA_CMP_STRIDE
    s0 = np.arange(n_sel) * NSA_SEL_BLOCK
    lo = np.maximum(c0[:, None], s0[None, :])
    hi = np.minimum(c0[:, None] + NSA_CMP_BLOCK, s0[None, :] + NSA_SEL_BLOCK)
    overlap = (np.maximum(hi - lo, 0) / NSA_CMP_STRIDE).astype(np.float32)
    overlap[n_cmp:] = 0.0
    key = np.arange(s)
    expand = (key[:, None] // NSA_SEL_BLOCK == np.arange(n_sel)[None, :]).astype(np.float32)
    expand = expand.reshape(s // (SLC_KB * QBLK), SLC_KB * QBLK, n_sel)
    consts = [jnp.asarray(overlap.T), jnp.asarray(expand, BF16)]

    cmp_spec = pl.BlockSpec((1, n_blk, LANES), lambda b, i: (b, 0, 0))
    return pl.pallas_call(
        functools.partial(_nsa_attn_kernel, n_cmp=n_cmp),
        grid=(bsz, s // QBLK),
        in_specs=[pl.BlockSpec((1, QBLK, NSA_WIDTH), lambda b, i: (b, i, 0)),
                  pl.BlockSpec((1, QBLK, LANES), lambda b, i: (b, i, 0)),
                  cmp_spec, cmp_spec,
                  pl.BlockSpec((1, s, LANES), lambda b, i: (b, 0, 0)),
                  pl.BlockSpec((1, s, LANES), lambda b, i: (b, 0, 1)),
                  pl.BlockSpec((1, s // QBLK, 4 * VT_ROWS, QBLK), lambda b, i: (b, 0, 0, 0))]
        + [_full(a.shape) for a in consts],
        out_specs=pl.BlockSpec((1, QBLK, NSA_WIDTH), lambda b, i: (b, i, 0)),
        out_shape=jax.ShapeDtypeStruct((bsz, s, NSA_WIDTH), F32),
        scratch_shapes=[pltpu.VMEM((2 * NSA_KV_HEADS, VT_ROWS, NSA_HPG * QBLK), F32)],
        compiler_params=_params("parallel", "arbitrary"),
    )(qr, gt, kcmp, vcmp, kk, kk, vt, *consts)


def _out_ln_kernel(x_ref, og_ref, or_ref, on_ref, w_ref, lw_ref, lb_ref, o_ref, *, alpha):
    mix = (_dot(og_ref[...], w_ref[0:LIN_WIDTH, :])
           + _dot(or_ref[...], w_ref[LIN_WIDTH:2 * LIN_WIDTH, :])
           + _dot(on_ref[...], w_ref[2 * LIN_WIDTH:, :]))
    o_ref[...] = _layer_norm(alpha * x_ref[...] + mix, lw_ref[...], lb_ref[...])


def _out_ln(x2, og, orw, on, w, lw, lb, alpha, tm=512):
    t = x2.shape[0]

    def rows(width):
        return pl.BlockSpec((tm, width), lambda i: (i, 0))

    return pl.pallas_call(
        functools.partial(_out_ln_kernel, alpha=alpha),
        grid=(t // tm,),
        in_specs=[rows(D_MODEL), rows(LIN_WIDTH), rows(LIN_WIDTH), rows(NSA_WIDTH),
                  _full(w.shape), _full(lw.shape), _full(lb.shape)],
        out_specs=rows(D_MODEL),
        out_shape=jax.ShapeDtypeStruct((t, D_MODEL), F32),
        compiler_params=_params("parallel"),
    )(x2, og, orw, on, w, lw, lb)


def _ffn_ln_kernel(x_ref, wg_ref, wu_ref, wd_ref, lw_ref, lb_ref, o_ref, acc_ref, *, alpha):
    j = pl.program_id(1)
    x = x_ref[...]
    xb = x.astype(BF16)
    gate = jnp.dot(xb, wg_ref[...], preferred_element_type=F32)
    up = jnp.dot(xb, wu_ref[...], preferred_element_type=F32)
    part = _dot(gate * _sigmoid(gate) * up, wd_ref[...])

    @pl.when(j == 0)
    def _():
        acc_ref[...] = part

    @pl.when(j > 0)
    def _():
        acc_ref[...] += part

    @pl.when(j == pl.num_programs(1) - 1)
    def _():
        o_ref[...] = _layer_norm(alpha * x + acc_ref[...], lw_ref[...], lb_ref[...])


def _ffn_ln(x2, wg, wu, wd, lw, lb, alpha, tm=512, th=1408):
    t = x2.shape[0]
    return pl.pallas_call(
        functools.partial(_ffn_ln_kernel, alpha=alpha),
        grid=(t // tm, FFN_HIDDEN // th),
        in_specs=[pl.BlockSpec((tm, D_MODEL), lambda i, j: (i, 0)),
                  pl.BlockSpec((D_MODEL, th), lambda i, j: (0, j)),
                  pl.BlockSpec((D_MODEL, th), lambda i, j: (0, j)),
                  pl.BlockSpec((th, D_MODEL), lambda i, j: (j, 0)),
                  _full(lw.shape), _full(lb.shape)],
        out_specs=pl.BlockSpec((tm, D_MODEL), lambda i, j: (i, 0)),
        out_shape=jax.ShapeDtypeStruct((t, D_MODEL), F32),
        scratch_shapes=[pltpu.VMEM((tm, D_MODEL), F32)],
        compiler_params=_params("parallel", "arbitrary"),
    )(x2, wg, wu, wd, lw, lb)


def _pad_cols(w, width):
    return jnp.pad(w, ((0, 0), (0, width - w.shape[1])))


def _pad_rows(w, height):
    return jnp.pad(w, ((0, height - w.shape[0]), (0, 0)))


def _rwkv_slab_cols(w, w_vres):
    o = 3 * LIN_WIDTH
    parts = [w[:, :o],
             _pad_cols(w[:, o:o + RWKV_DECAY_RANK], LANES),
             _pad_cols(w[:, o + RWKV_DECAY_RANK:o + RWKV_DECAY_RANK + RWKV_ICLR_RANK], LANES),
             _pad_cols(w[:, o + RWKV_DECAY_RANK + RWKV_ICLR_RANK:], 2 * LANES),
             _pad_cols(w_vres, LANES)]
    return jnp.concatenate(parts, axis=1)


def _rope_tables(s):
    half = HEAD_DIM // 2
    inv = ROPE_THETA ** (-jnp.arange(half, dtype=F32) / half)
    ang = jnp.arange(s, dtype=jnp.int32).astype(F32)[:, None] * inv
    cos, sin = jnp.cos(ang), jnp.sin(ang)
    cos2 = jnp.tile(jnp.concatenate([cos, cos], axis=1), (1, LANES // HEAD_DIM))
    sin2 = jnp.tile(jnp.concatenate([-sin, sin], axis=1), (1, LANES // HEAD_DIM))
    return cos2, sin2


def kernel(x, w_in, w_in_vres, gla_w_a2, gla_b_a, gla_ln_w, gla_ln_b, rwkv_mu, rwkv_mu_vres, rwkv_w0, rwkv_w2, rwkv_a0, rwkv_a2, rwkv_v0, rwkv_v2, rwkv_g2, rwkv_k_k, rwkv_k_a, rwkv_r_k, rwkv_ln_w, rwkv_ln_b, nsa_pos_k, nsa_pos_v, nsa_wk1, nsa_wk2, nsa_wv1, nsa_wv2, w_out, ln1_w, ln1_b, ffn_w_gate, ffn_w_up, ffn_w_down, ln2_w, ln2_b):
    bsz, s, d = x.shape
    depth = w_in.shape[0]
    alpha = float((2 * depth) ** 0.25)
    gla_cols = 4 * LIN_WIDTH + GLA_GATE_RANK
    rwkv_cols = 3 * LIN_WIDTH + RWKV_DECAY_RANK + RWKV_ICLR_RANK + RWKV_GATE_RANK
    cos2, sin2 = _rope_tables(s)
    row = lambda a: a.reshape(1, -1)

    x2 = x.reshape(bsz * s, d)
    v_first = None
    for l in range(depth):
        w = w_in[l]
        w_r = w[:, gla_cols:gla_cols + rwkv_cols]
        if l == 0:
            w_vres = jnp.zeros((d, RWKV_VRES_RANK), F32)
            mu_vres = jnp.zeros((1, RWKV_VRES_RANK), F32)
        else:
            w_vres = w_in_vres[l - 1]
            mu_vres = row(rwkv_mu_vres[l - 1])
        wg = _pad_cols(w[:, :gla_cols], GLA_SLAB).astype(BF16)
        wr = _rwkv_slab_cols(w_r, w_vres).astype(BF16)
        wn = _pad_cols(w[:, gla_cols + rwkv_cols:], NSA_SLAB).astype(BF16)
        hg, hr, hn = _proj(x2, wg, wr, wn)
        hg = hg.reshape(bsz, s, GLA_SLAB)
        hr = hr.reshape(bsz, s, RWKV_SLAB)
        hn = hn.reshape(bsz, s, NSA_SLAB)

        o_gla = _gla(hg, _pad_rows(gla_w_a2[l], LANES), row(gla_b_a[l]), row(gla_ln_w[l]), row(gla_ln_b[l]))

        mu = _rwkv_slab_cols(row(rwkv_mu[l]), mu_vres)
        small = [mu, row(rwkv_w0[l]), _pad_rows(rwkv_w2[l], LANES), row(rwkv_a0[l]),
                 _pad_rows(rwkv_a2[l], LANES), _pad_rows(rwkv_g2[l], 2 * LANES), row(rwkv_k_k[l]),
                 row(rwkv_k_a[l]), row(rwkv_r_k[l]), row(rwkv_ln_w[l]), row(rwkv_ln_b[l])]
        if l == 0:
            o_rwkv, v_first = _rwkv(hr, None, small)
        else:
            small += [row(rwkv_v0[l - 1]), _pad_rows(rwkv_v2[l - 1], LANES)]
            o_rwkv = _rwkv(hr, v_first, small)

        qr, kvc, kk, vt, gt = _nsa_prep(hn, cos2, sin2)
        kcmp, vcmp = _nsa_cmp(kvc, nsa_pos_k[l], nsa_pos_v[l], nsa_wk1[l], nsa_wk2[l], nsa_wv1[l], nsa_wv2[l])
        o_nsa = _nsa_attn(qr, kk, vt, gt, kcmp, vcmp)

        x2 = _out_ln(x2, o_gla.reshape(bsz * s, LIN_WIDTH), o_rwkv.reshape(bsz * s, LIN_WIDTH),
                     o_nsa.reshape(bsz * s, NSA_WIDTH), w_out[l].astype(BF16), row(ln1_w[l]), row(ln1_b[l]), alpha)
        x2 = _ffn_ln(x2, ffn_w_gate[l].astype(BF16), ffn_w_up[l].astype(BF16), ffn_w_down[l].astype(BF16),
                     row(ln2_w[l]), row(ln2_b[l]), alpha)
    return x2.reshape(bsz, s, d)
```

```python
import functools

import numpy as np
import jax
import jax.numpy as jnp
from jax import lax
from jax.experimental import pallas as pl
from jax.experimental.pallas import tpu as pltpu

F32 = jnp.float32
BF16 = jnp.bfloat16
HIGHEST = lax.Precision.HIGHEST

D_MODEL = 1024
HEAD_DIM = 64
N_LIN_HEADS = 4
LIN_WIDTH = N_LIN_HEADS * HEAD_DIM
GLA_GATE_RANK = 16
GLA_TAU = 16.0
RWKV_DECAY_RANK = 64
RWKV_ICLR_RANK = 64
RWKV_VRES_RANK = 32
RWKV_GATE_RANK = 160
RWKV_GN_EPS = 64e-5
NSA_HEADS = 8
NSA_KV_HEADS = 2
NSA_HPG = NSA_HEADS // NSA_KV_HEADS
NSA_WIDTH = NSA_HEADS * HEAD_DIM
NSA_KV_WIDTH = NSA_KV_HEADS * HEAD_DIM
NSA_CMP_BLOCK = 32
NSA_CMP_STRIDE = 16
NSA_CMP_HIDDEN = 256
NSA_SEL_BLOCK = 64
NSA_N_SELECT = 16
NSA_WINDOW = 512
ROPE_THETA = 10000.0
FFN_HIDDEN = 2816
FFN_SPLITS = ((0, 1536), (1536, FFN_HIDDEN))
LN_EPS = 1e-5
MASK_NEG = -1e30

LANES = 128
CHUNK = 64
SUB = 16
RWKV_GROUP = 4
GLA_GROUP = 4
RWKV_HB = 2
RW_BW = RWKV_HB * HEAD_DIM
QBLK = 128
SLC_KB = 4
VT_ROWS = HEAD_DIM + 16
LOG2E = 1.4426950408889634
VMEM_LIMIT = 56 * 1024 * 1024

GLA_SLAB = 4 * LIN_WIDTH + LANES
RWKV_SLAB = 3 * LIN_WIDTH + LANES + LANES + 2 * LANES + LANES
NSA_SLAB = NSA_WIDTH + 6 * NSA_KV_WIDTH + LANES
RW_WLR, RW_ALR, RW_GLR, RW_VLR = 768, 896, 1024, 1280


def _dot(a, b, dims=None, hi=False):
    dims = dims or (((a.ndim - 1,), (0,)), ((), ()))
    if hi:
        return lax.dot_general(a, b, dims, precision=HIGHEST, preferred_element_type=F32)
    return lax.dot_general(a.astype(BF16), b.astype(BF16), dims, preferred_element_type=F32)


def _split_bf16(a):
    hi = a.astype(BF16)
    return hi, (a - hi.astype(F32)).astype(BF16)


def _dot3(a, b_hi, b_lo):
    a_hi, a_lo = _split_bf16(a)
    return (jnp.dot(a_hi, b_hi, preferred_element_type=F32)
            + (jnp.dot(a_lo, b_hi, preferred_element_type=F32) + jnp.dot(a_hi, b_lo, preferred_element_type=F32)))


_NT = (((1,), (1,)), ((), ()))
_TN = (((0,), (0,)), ((), ()))


def _sigmoid(x):
    return 1.0 / (1.0 + jnp.exp(-x))


def _softplus(x):
    return jnp.maximum(x, 0.0) + jnp.log(1.0 + jnp.exp(-jnp.abs(x)))


def _head_norm(x, w, b, eps):
    mu = jnp.mean(x, axis=-1, keepdims=True)
    xc = x - mu
    var = jnp.mean(xc * xc, axis=-1, keepdims=True)
    return xc * lax.rsqrt(var + eps) * w + b


def _layer_norm(x, w, b):
    mu = jnp.mean(x, axis=-1, keepdims=True)
    xc = x - mu
    var = jnp.mean(xc * xc, axis=-1, keepdims=True)
    return xc * lax.rsqrt(var + LN_EPS) * w + b


def _params(*sem):
    return pltpu.CompilerParams(dimension_semantics=sem, vmem_limit_bytes=VMEM_LIMIT)


def _full(shape):
    nd = len(shape)
    return pl.BlockSpec(shape, lambda *_: (0,) * nd)


def _proj_kernel(x_ref, wg_ref, wr_ref, wn_ref, cos_ref, sin_ref,
                 og_ref, or_ref, q_ref, kvc_ref, kk_ref, vt_ref, gt_ref, *, n_qblk):
    xb = x_ref[...].astype(BF16)
    og_ref[...] = jnp.dot(xb, wg_ref[...], preferred_element_type=F32)
    or_ref[...] = jnp.dot(xb, wr_ref[...], preferred_element_type=F32)
    hn = jnp.dot(xb, wn_ref[...], preferred_element_type=F32)

    cos = cos_ref[...]
    sin = sin_ref[...]
    half = HEAD_DIM // 2
    first_half = (lax.broadcasted_iota(jnp.int32, cos.shape, 1) & (HEAD_DIM - 1)) < half

    def rope(x):
        swapped = jnp.where(first_half, pltpu.roll(x, LANES - half, 1), pltpu.roll(x, half, 1))
        return x * cos + swapped * sin

    def col(i):
        return hn[:, NSA_WIDTH + LANES * i:NSA_WIDTH + LANES * (i + 1)]

    for i in range(NSA_WIDTH // LANES):
        cs = slice(LANES * i, LANES * (i + 1))
        q_ref[:, cs] = rope(hn[:, cs]) * (HEAD_DIM ** -0.5)
    kvc_ref[:, 0:LANES] = rope(col(0))
    kvc_ref[:, LANES:2 * LANES] = col(1)
    kk_ref[:, 0:LANES] = rope(col(2)).astype(BF16)
    kk_ref[:, LANES:2 * LANES] = rope(col(4)).astype(BF16)
    pad = VT_ROWS - HEAD_DIM
    ones_rows = jnp.where(lax.broadcasted_iota(jnp.int32, (pad, QBLK), 0) == 0, 1.0, 0.0).astype(BF16)
    for i in range(n_qblk):
        rs = slice(QBLK * i, QBLK * (i + 1))
        for branch, c in enumerate((3, 5)):
            v_t = col(c)[rs, :].T.astype(BF16)
            for g in range(NSA_KV_HEADS):
                r0 = (branch * NSA_KV_HEADS + g) * VT_ROWS
                vt_ref[i, r0:r0 + HEAD_DIM, :] = v_t[HEAD_DIM * g:HEAD_DIM * (g + 1), :]
                vt_ref[i, r0 + HEAD_DIM:r0 + VT_ROWS, :] = ones_rows
    gt_ref[...] = _sigmoid(hn[:, NSA_WIDTH + 6 * LANES:NSA_SLAB])


def _proj(x2, wg, wr, wn, cos2, sin2, tm=512):
    t = x2.shape[0]
    s = cos2.shape[0]
    n_qblk = tm // QBLK

    def rows(width):
        return pl.BlockSpec((tm, width), lambda i: (i, 0))

    def resident(a):
        return pl.BlockSpec(a.shape, lambda i: (0,) * a.ndim, pipeline_mode=pl.Buffered(1))

    pos_rows = pl.BlockSpec((tm, LANES), lambda i: (i % (s // tm), 0))
    return pl.pallas_call(
        functools.partial(_proj_kernel, n_qblk=n_qblk),
        grid=(t // tm,),
        in_specs=[rows(D_MODEL), resident(wg), resident(wr), resident(wn), pos_rows, pos_rows],
        out_specs=[rows(GLA_SLAB), rows(RWKV_SLAB), rows(NSA_WIDTH), rows(2 * LANES), rows(2 * LANES),
                   pl.BlockSpec((n_qblk, 4 * VT_ROWS, QBLK), lambda i: (i, 0, 0)), rows(LANES)],
        out_shape=[jax.ShapeDtypeStruct((t, GLA_SLAB), F32),
                   jax.ShapeDtypeStruct((t, RWKV_SLAB), F32),
                   jax.ShapeDtypeStruct((t, NSA_WIDTH), F32),
                   jax.ShapeDtypeStruct((t, 2 * LANES), F32),
                   jax.ShapeDtypeStruct((t, 2 * LANES), BF16),
                   jax.ShapeDtypeStruct((t // QBLK, 4 * VT_ROWS, QBLK), BF16),
                   jax.ShapeDtypeStruct((t, LANES), F32)],
        compiler_params=_params("parallel"),
    )(x2, wg, wr, wn, cos2, sin2)


def _group_rows(x, u):
    return jnp.concatenate(
        [jnp.broadcast_to(x[SUB * i + u:SUB * i + u + 1, :], (SUB, x.shape[1]))
         for i in range(CHUNK // SUB)], axis=0)


def _gla_kernel(h_ref, wa2_ref, ba_ref, lnw_ref, lnb_ref, tril_ref, blk_ref, ones_ref, dm_ref, same_ref,
                same_sub_ref, o_ref, st_ref, *, n_chunks):
    @pl.when(pl.program_id(1) == 0)
    def _():
        st_ref[...] = jnp.zeros_like(st_ref)

    row_in_sub = lax.broadcasted_iota(jnp.int32, (CHUNK, LIN_WIDTH), 0) % SUB
    dm = dm_ref[...]
    heads = range(N_LIN_HEADS)
    n_sub = CHUNK // SUB

    def own_head(m, mask_ref):
        return jnp.concatenate([m] * N_LIN_HEADS, axis=0) * mask_ref[...]

    def each(f, *lists):
        return [f(*args) for args in zip(*lists)]

    def chunk_group(gi, carry):
        rows = [pl.ds(pl.multiple_of((gi * GLA_GROUP + j) * CHUNK, CHUNK), CHUNK) for j in range(GLA_GROUP)]
        ps = [h_ref[0, rw, :] for rw in rows]
        q = each(lambda p: p[:, 0:256] * (HEAD_DIM ** -0.5), ps)
        k = each(lambda p: p[:, 256:512], ps)
        v = each(lambda p: p[:, 512:768], ps)
        g = each(lambda p: p[:, 768:1024], ps)
        z = each(lambda p: _dot(p[:, 1024:1152], wa2_ref[...], hi=True) + ba_ref[...], ps)
        log_a = each(lambda x: -_softplus(-x) * (1.0 / GLA_TAU), z)
        b = each(lambda x: _dot(tril_ref[...], x, hi=True), log_a)
        b_end = each(lambda x: _dot(blk_ref[...], x, hi=True), log_a)
        qe = each(lambda x, bb: x * jnp.exp(bb), q, b)
        kd = each(lambda x, be, bb: x * jnp.exp(be - bb), k, b_end, b)
        p_end = each(jnp.exp, b_end)
        v_st = each(lambda x: jnp.concatenate([x[:, HEAD_DIM * h:HEAD_DIM * (h + 1)] for h in heads], axis=0), v)

        at = [jnp.zeros((CHUNK, LIN_WIDTH), F32) for _ in ps]
        for u0 in range(0, SUB, N_LIN_HEADS):
            us = range(u0, u0 + N_LIN_HEADS)
            es = each(lambda bb, qq, kk: jnp.concatenate(
                [jnp.exp(jnp.where(row_in_sub <= u, _group_rows(bb, u) - bb, MASK_NEG)) * (_group_rows(qq, u) * kk)
                 for u in us], axis=0), b, q, k)
            a_rep = each(lambda e: _dot(e, ones_ref[...]), es)
            for i, u in enumerate(us):
                at = each(lambda acc, ar: acc + jnp.where(dm == u, ar[CHUNK * i:CHUNK * (i + 1), :], 0.0), at, a_rep)
        o_st = each(lambda x, vs: _dot(own_head(x, same_ref), vs, _TN), at, v_st)

        upd, q_own = [], []
        for i in range(n_sub):
            rs = slice(SUB * i, SUB * (i + 1))
            upd.append(each(lambda vs, kk: _dot(
                jnp.concatenate([vs[CHUNK * h + SUB * i:CHUNK * h + SUB * (i + 1), :] for h in heads], axis=0),
                own_head(kk[rs], same_sub_ref), _TN), v_st, kd))
            q_own.append(each(lambda x: own_head(x[rs], same_sub_ref), qe))
        st = st_ref[...]
        for c, rw in enumerate(rows):
            inter = []
            for i in range(n_sub):
                inter.append(_dot(q_own[i][c], st, _NT))
                st = st * p_end[c][SUB * i:SUB * i + 1, :] + upd[i][c]
            o_c = o_st[c] + jnp.concatenate(
                [inter[i][SUB * h:SUB * (h + 1), :] for h in heads for i in range(n_sub)], axis=0)
            o_c = _head_norm(o_c, 1.0, 0.0, LN_EPS)
            o = jnp.concatenate([o_c[CHUNK * h:CHUNK * (h + 1), :] for h in heads], axis=1)
            o_ref[0, rw, :] = (o * lnw_ref[...] + lnb_ref[...]) * (g[c] * _sigmoid(g[c]))
        st_ref[...] = st
        return carry

    lax.fori_loop(0, n_chunks // GLA_GROUP, chunk_group, 0)


def _gla(hg, wa2, ba, lnw, lnb, ts=512):
    bsz, s, _ = hg.shape
    r = np.arange(CHUNK)
    tril = ((r[:, None] // SUB == r[None, :] // SUB) & (r[None, :] <= r[:, None])).astype(np.float32)
    blk = (r[:, None] // SUB == r[None, :] // SUB).astype(np.float32)
    c = np.arange(LIN_WIDTH)
    same = (c[:, None] // HEAD_DIM == c[None, :] // HEAD_DIM).astype(np.float32)
    same_sub = (r[:, None] // SUB == c[None, :] // HEAD_DIM).astype(np.float32)
    dm = ((c[None, :] % HEAD_DIM) - SUB * (r[:, None] // SUB)).astype(np.int32)
    consts = [jnp.asarray(tril), jnp.asarray(blk), jnp.asarray(same, BF16), jnp.asarray(dm), jnp.asarray(same),
              jnp.asarray(same_sub)]
    small = [wa2, ba, lnw, lnb] + consts
    return pl.pallas_call(
        functools.partial(_gla_kernel, n_chunks=ts // CHUNK),
        grid=(bsz, s // ts),
        in_specs=[pl.BlockSpec((1, ts, GLA_SLAB), lambda b, j: (b, j, 0))] + [_full(a.shape) for a in small],
        out_specs=pl.BlockSpec((1, ts, LIN_WIDTH), lambda b, j: (b, j, 0)),
        out_shape=jax.ShapeDtypeStruct((bsz, s, LIN_WIDTH), F32),
        scratch_shapes=[pltpu.VMEM((HEAD_DIM, LIN_WIDTH), F32)],
        compiler_params=_params("parallel", "arbitrary"),
    )(hg, *small)


def _rwkv_kernel(*refs, n_chunks, has_vres):
    if has_vres:
        (h_ref, vf_ref, mu_ref, w0_ref, w2_ref, a0_ref, a2_ref, g2_ref, kk_ref, ka_ref, rk_ref,
         lnw_ref, lnb_ref, v0_ref, v2_ref, tril_ref, ones_ref, same_ref, strict_ref, incl_ref, eye_ref,
         o_ref, st_ref, prev_ref) = refs
    else:
        (h_ref, mu_ref, w0_ref, w2_ref, a0_ref, a2_ref, g2_ref, kk_ref, ka_ref, rk_ref,
         lnw_ref, lnb_ref, tril_ref, ones_ref, same_ref, strict_ref, incl_ref, eye_ref,
         o_ref, vf_ref, st_ref, prev_ref) = refs

    @pl.when(pl.program_id(1) == 0)
    def _():
        st_ref[...] = jnp.zeros_like(st_ref)
        prev_ref[...] = jnp.zeros_like(prev_ref)

    row = lax.broadcasted_iota(jnp.int32, (CHUNK, RWKV_SLAB), 0)
    blocks = range(N_LIN_HEADS // RWKV_HB)
    in_block = range(RWKV_HB)
    nr = RWKV_HB * CHUNK

    def tile_rows(m):
        return jnp.concatenate([m] * RWKV_HB, axis=0)

    def own_head(m):
        return tile_rows(m) * same_ref[...]

    def stack(m):
        return jnp.concatenate([m[:, HEAD_DIM * h:HEAD_DIM * (h + 1)] for h in in_block], axis=0)

    def unstack(m):
        return jnp.concatenate([m[CHUNK * h:CHUNK * (h + 1), :] for h in in_block], axis=1)

    def each(f, *lists):
        return [f(*args) for args in zip(*lists)]

    def per_block(xs):
        return [x[:, RW_BW * bi:RW_BW * (bi + 1)] for x in xs for bi in blocks]

    def prepare(ps, prev_rows, vfs):
        prev = each(lambda p, pr: jnp.where(row == 0, pr, pltpu.roll(p, 1, 0)), ps, prev_rows)
        xm = each(lambda p, pv: p + (pv - p) * mu_ref[...], ps, prev)
        r = each(lambda x: x[:, 0:256], xm)
        k = each(lambda x: x[:, 256:512], xm)
        v = each(lambda x: x[:, 512:768], xm)
        v_shift = v
        w_log = each(lambda x: -_softplus(-(w0_ref[...] + _dot(jnp.tanh(x[:, RW_WLR:RW_ALR]), w2_ref[...], hi=True)))
                     - 0.5, xm)
        lw = each(lambda w: -jnp.exp(w), w_log)
        a = each(lambda x: _sigmoid(a0_ref[...] + _dot(x[:, RW_ALR:RW_GLR], a2_ref[...], hi=True)), xm)
        g = each(lambda x: _dot(_sigmoid(x[:, RW_GLR:RW_VLR]), g2_ref[...], hi=True), xm)
        if has_vres:
            mixv = each(lambda x: _sigmoid(v0_ref[...] + _dot(x[:, RW_VLR:RWKV_SLAB], v2_ref[...], hi=True)), xm)
            v = each(lambda vv, vf, mx: vv + (vf - vv) * mx, v, vfs, mixv)
        kkv = each(lambda kk: kk * kk_ref[...], k)
        nrm = each(lambda kv: jnp.sqrt(_dot(kv * kv, ones_ref[...], hi=True)), kkv)
        kkn = each(lambda kv, n: kv / jnp.maximum(n, 1e-12), kkv, nrm)
        k2 = each(lambda kk, aa: kk * (1.0 + (aa - 1.0) * ka_ref[...]), k, a)
        bv = each(lambda kn, aa: kn * aa, kkn, a)
        c = each(lambda w: _dot(tril_ref[...], w, hi=True), lw)
        c_end = each(lambda cc: cc[CHUNK - 1:CHUNK, :], c)
        bonus = each(lambda rr, kk, vv: _dot(rr * kk * rk_ref[...], ones_ref[...], hi=True) * vv, r, k2, v)
        at = each(lambda kn, cc, w: -kn * jnp.exp(cc - w), kkn, c, lw)
        rt = each(lambda rr, cc: rr * jnp.exp(cc), r, c)
        bt = each(lambda b_, cc: b_ * jnp.exp(-cc), bv, c)
        kt = each(lambda kk, cc: kk * jnp.exp(-cc), k2, c)
        b_rest = each(lambda b_, ce, cc: b_ * jnp.exp(ce - cc), bv, c_end, c)
        k_rest = each(lambda kk, ce, cc: kk * jnp.exp(ce - cc), k2, c_end, c)
        p_end = each(jnp.exp, c_end)

        ar = each(lambda x, y: jnp.concatenate([own_head(x), own_head(y)], axis=0), per_block(at), per_block(rt))
        bk = each(lambda x, y: jnp.concatenate([tile_rows(x), tile_rows(y)], axis=0), per_block(bt), per_block(kt))
        gm = each(lambda x, y: _dot(x, y, _NT), ar, bk)
        n_ab = each(lambda m: m[0:nr, 0:nr] * strict_ref[...], gm)
        m_ak = each(lambda m: m[0:nr, nr:2 * nr] * strict_ref[...], gm)
        r_bk = each(lambda m: m[nr:2 * nr, :] * jnp.concatenate([incl_ref[...], incl_ref[...]], axis=1), gm)
        v_st = each(stack, per_block(v))
        mv = each(_dot, m_ak, v_st)
        inv = each(lambda n: eye_ref[...] + n, n_ab)
        pw = n_ab
        for _ in range(5):
            pw = each(lambda x: _dot(x, x), pw)
            inv = each(lambda i_, x: i_ + _dot(i_, x), inv, pw)
        bk_rest = each(lambda x, y: jnp.concatenate([own_head(x), own_head(y)], axis=0),
                       per_block(b_rest), per_block(k_rest))
        a_bar = each(lambda i_, x: _dot(i_, x[0:nr]), inv, ar)
        uv0 = each(lambda i_, m, vs: jnp.concatenate([_dot(i_, m), vs], axis=0), inv, mv, v_st)
        r_bar = each(lambda x, rb, ab: x[nr:2 * nr] + _dot(rb[:, 0:nr], ab), ar, r_bk, a_bar)
        y0 = each(_dot, r_bk, uv0)
        g_mat = each(lambda ab, br: _dot(ab, br[0:nr], _TN), a_bar, bk_rest)
        sc = each(lambda z, br: _dot(z, br, _TN), uv0, bk_rest)
        items = list(zip(r_bar, y0, g_mat, sc, per_block(p_end)))
        n_blk = len(blocks)
        return [(items[n_blk * j:n_blk * (j + 1)], bonus[j], g[j]) for j in range(len(ps))], v_shift

    def advance(prep, s0):
        items, bonus, g = prep
        r_bar, y0, g_mat, sc, p_end = (list(x) for x in zip(*items))
        s0b = per_block([s0])
        y = each(lambda rb, s_, y_: unstack(_head_norm(_dot(rb, s_, _NT) + y_, 1.0, 0.0, RWKV_GN_EPS)), r_bar, s0b, y0)
        s1 = each(lambda s_, pe, gm_, c_: s_ * pe + _dot(s_, gm_) + c_, s0b, p_end, g_mat, sc)
        y = jnp.concatenate(y, axis=1)
        return (y * lnw_ref[...] + lnb_ref[...] + bonus) * g, jnp.concatenate(s1, axis=1)

    def chunk_group(gi, carry):
        rows = [pl.ds(pl.multiple_of((gi * RWKV_GROUP + j) * CHUNK, CHUNK), CHUNK) for j in range(RWKV_GROUP)]
        ps = [h_ref[0, rw, :] for rw in rows]
        vfs = [vf_ref[0, rw, :] if has_vres else None for rw in rows]
        prev_rows = [prev_ref[...]] + [p[CHUNK - 1:CHUNK, :] for p in ps[:-1]]
        preps, shifted = prepare(ps, prev_rows, vfs)
        s = st_ref[...]
        for j, rw in enumerate(rows):
            out, s = advance(preps[j], s)
            o_ref[0, rw, :] = out
            if not has_vres:
                vf_ref[0, rw, :] = shifted[j]
        prev_ref[...] = ps[-1][CHUNK - 1:CHUNK, :]
        st_ref[...] = s
        return carry

    lax.fori_loop(0, n_chunks // RWKV_GROUP, chunk_group, 0)


def _rwkv(hr, v_first, small, ts=512):
    bsz, s, _ = hr.shape
    has_vres = v_first is not None
    r = np.arange(CHUNK)
    tril = (r[None, :] <= r[:, None]).astype(np.float32)
    i = np.arange(LIN_WIDTH)
    ones = (i[:, None] // HEAD_DIM == i[None, :] // HEAD_DIM).astype(np.float32)
    i = np.arange(RW_BW)
    same = (i[:, None] // HEAD_DIM == i[None, :] // HEAD_DIM).astype(np.float32)
    t_in, s_in = i[:, None] % CHUNK, i[None, :] % CHUNK
    consts = [tril, ones, same, same * (s_in < t_in), same * (s_in <= t_in), np.eye(RW_BW, dtype=np.float32)]
    small = list(small) + [jnp.asarray(c, F32) for c in consts]
    seq_spec = pl.BlockSpec((1, ts, LIN_WIDTH), lambda b, j: (b, j, 0))
    in_specs = [pl.BlockSpec((1, ts, RWKV_SLAB), lambda b, j: (b, j, 0))]
    args = [hr]
    if has_vres:
        in_specs.append(seq_spec)
        args.append(v_first)
    in_specs += [_full(a.shape) for a in small]
    out_sd = jax.ShapeDtypeStruct((bsz, s, LIN_WIDTH), F32)
    return pl.pallas_call(
        functools.partial(_rwkv_kernel, n_chunks=ts // CHUNK, has_vres=has_vres),
        grid=(bsz, s // ts),
        in_specs=in_specs,
        out_specs=seq_spec if has_vres else [seq_spec, seq_spec],
        out_shape=out_sd if has_vres else [out_sd, out_sd],
        scratch_shapes=[pltpu.VMEM((HEAD_DIM, LIN_WIDTH), F32),
                        pltpu.VMEM((1, RWKV_SLAB), F32)],
        compiler_params=_params("parallel", "arbitrary"),
    )(*args, *small)


def _nsa_cmp_kernel(kc_ref, vc_ref, pk_ref, pv_ref, wk1h_ref, wk1l_ref, wk2h_ref, wk2l_ref,
                    wv1h_ref, wv1l_ref, wv2h_ref, wv2l_ref, ko_ref, vo_ref, *, n_blk):
    half = NSA_CMP_BLOCK // 2

    def compress(t_ref, pos_ref, w1h_ref, w1l_ref, w2h_ref, w2l_ref, o_ref):
        for g in range(NSA_KV_HEADS):
            gs = slice(HEAD_DIM * g, HEAD_DIM * (g + 1))
            top = jnp.zeros((n_blk, NSA_CMP_HIDDEN), F32)
            bot = jnp.zeros((n_blk, NSA_CMP_HIDDEN), F32)
            for l in range(half):
                rows = t_ref[0, pl.ds(l, n_blk, stride=NSA_CMP_STRIDE), :][:, gs]
                top = top + _dot3(rows + pos_ref[l:l + 1, :], w1h_ref[l], w1l_ref[l])
                bot = bot + _dot3(rows + pos_ref[half + l:half + l + 1, :], w1h_ref[half + l], w1l_ref[half + l])
            hid = top + pltpu.roll(bot, n_blk - 1, 0)
            act = 0.5 * hid * (1.0 + jnp.tanh(0.7978845608028654 * (hid + 0.044715 * hid * hid * hid)))
            o_ref[0, :, gs] = _dot3(act, w2h_ref[...], w2l_ref[...])

    compress(kc_ref, pk_ref, wk1h_ref, wk1l_ref, wk2h_ref, wk2l_ref, ko_ref)
    compress(vc_ref, pv_ref, wv1h_ref, wv1l_ref, wv2h_ref, wv2l_ref, vo_ref)


def _nsa_cmp(kv, pos_k, pos_v, wk1, wk2, wv1, wv2):
    bsz, s, _ = kv.shape
    n_blk = s // NSA_CMP_STRIDE
    wk1 = wk1.reshape(NSA_CMP_BLOCK, HEAD_DIM, NSA_CMP_HIDDEN)
    wv1 = wv1.reshape(NSA_CMP_BLOCK, HEAD_DIM, NSA_CMP_HIDDEN)
    small = [pos_k, pos_v, *_split_bf16(wk1), *_split_bf16(wk2), *_split_bf16(wv1), *_split_bf16(wv2)]
    out_spec = pl.BlockSpec((1, n_blk, LANES), lambda b: (b, 0, 0))
    out_sd = jax.ShapeDtypeStruct((bsz, n_blk, LANES), F32)
    return pl.pallas_call(
        functools.partial(_nsa_cmp_kernel, n_blk=n_blk),
        grid=(bsz,),
        in_specs=[pl.BlockSpec((1, s, LANES), lambda b: (b, 0, 0)),
                  pl.BlockSpec((1, s, LANES), lambda b: (b, 0, 1))] + [_full(a.shape) for a in small],
        out_specs=[out_spec, out_spec],
        out_shape=[out_sd, out_sd],
        compiler_params=_params("parallel"),
    )(kv, kv, *small)


def _nsa_attn_kernel(q_ref, gt_ref, kc_ref, vc_ref, ks_ref, kw_ref, vt_ref, ovt_ref, ext_ref,
                     o_ref, acc_ref, *, n_cmp):
    qi = pl.program_id(1)
    cols = NSA_HPG * QBLK
    n_sel_blk = ovt_ref.shape[0]
    key_off = lax.broadcasted_iota(jnp.int32, (QBLK, cols), 0)
    t_pos = qi * QBLK + (lax.broadcasted_iota(jnp.int32, (QBLK, cols), 1) & (QBLK - 1))
    gt_t = gt_ref[0].T
    zeros64 = jnp.zeros((QBLK, HEAD_DIM), F32)
    m0 = jnp.full((1, cols), MASK_NEG, F32)

    groups = list(range(NSA_KV_HEADS))

    def each(f, *lists):
        return [f(*args) for args in zip(*lists)]

    def stacked_q(g):
        pieces = []
        for hp in range(NSA_HPG):
            qh = q_ref[0, :, (g * NSA_HPG + hp) * HEAD_DIM:(g * NSA_HPG + hp + 1) * HEAD_DIM]
            pieces.append(jnp.concatenate([qh, zeros64] if g == 0 else [zeros64, qh], axis=1))
        return jnp.concatenate(pieces, axis=0)

    qs = each(stacked_q, groups)
    qs_b = each(lambda q: (q * LOG2E).astype(BF16), qs)

    valid_c = (key_off * NSA_CMP_STRIDE + (NSA_CMP_BLOCK - 1) <= t_pos) & (key_off < n_cmp)
    s_c = each(lambda q: jnp.where(valid_c, _dot(kc_ref[0], q, _NT, hi=True), MASK_NEG), qs)
    e_c = each(lambda x: jnp.exp(x - jnp.max(x, axis=0, keepdims=True)), s_c)
    p_c = each(lambda e: jnp.where(valid_c, e / jnp.sum(e, axis=0, keepdims=True), 0.0), e_c)
    o_cmp = each(lambda p, g: _dot(vc_ref[0], p, _TN)[HEAD_DIM * g:HEAD_DIM * (g + 1), :], p_c, groups)
    p_sum = each(lambda p: p[:, 0:QBLK] + p[:, QBLK:2 * QBLK] + p[:, 2 * QBLK:3 * QBLK] + p[:, 3 * QBLK:], p_c)
    imp = each(lambda p: _dot(ovt_ref[...], p, hi=True), p_sum)

    j = lax.broadcasted_iota(jnp.int32, (n_sel_blk, QBLK), 0)
    tq = qi * QBLK + lax.broadcasted_iota(jnp.int32, (n_sel_blk, QBLK), 1)
    cur = tq // NSA_SEL_BLOCK
    forced = (j == 0) | (j == cur) | (j == cur - 1)
    score = each(lambda x: jnp.where(forced, 1e9, jnp.where(j > cur, -1e9, x)), imp)
    rank = [jnp.zeros((n_sel_blk, QBLK), F32) for _ in groups]
    for i in range(n_sel_blk):
        rank = each(lambda rk, sc: rk + jnp.where(
            (sc[i:i + 1, :] > sc) | ((sc[i:i + 1, :] == sc) & (j > i)), 1.0, 0.0), rank, score)
    sel = each(lambda rk: jnp.where(rk < NSA_N_SELECT, 1.0, 0.0), rank)

    def attend(chains):
        def scores(k_ref, g, slab, kb0, n_kb, m, bias, mask):
            k0 = pl.multiple_of(kb0 * QBLK, QBLK)
            s = lax.dot_general(k_ref[0, pl.ds(k0, n_kb * QBLK), :], qs_b[g], _NT, preferred_element_type=F32)
            if bias is not None:
                s = s + bias
            if mask is not None:
                key_pos = k0 + lax.broadcasted_iota(jnp.int32, (n_kb * QBLK, cols), 0)
                t_all = jnp.concatenate([t_pos] * n_kb, axis=0)
                valid = key_pos <= t_all
                if mask == "window":
                    valid = valid & (key_pos > t_all - NSA_WINDOW)
                s = jnp.where(valid, s, MASK_NEG)
            return s

        s = [scores(*ch) for ch in chains]
        ms = [ch[5] for ch in chains]
        m_new = each(lambda m, x: jnp.maximum(m, jnp.max(x, axis=0, keepdims=True)), ms, s)
        p = each(lambda x, m: jnp.exp2(x - m).astype(BF16), s, m_new)
        pv = [jnp.zeros((VT_ROWS, cols), F32) for _ in chains]
        for i in range(max(ch[4] for ch in chains)):
            pv = [acc + jnp.dot(vt_ref[0, ch[3] + i, ch[2] * VT_ROWS:(ch[2] + 1) * VT_ROWS, :],
                                pp[QBLK * i:QBLK * (i + 1), :], preferred_element_type=F32) if i < ch[4] else acc
                  for acc, pp, ch in zip(pv, p, chains)]
        for ch, m, mn, x in zip(chains, ms, m_new, pv):
            acc_ref[ch[2]] = acc_ref[ch[2]] * jnp.exp2(m - mn) + x
        return tuple(m_new)

    def normalised(slab):
        return acc_ref[slab, 0:HEAD_DIM, :] / acc_ref[slab, HEAD_DIM:HEAD_DIM + 1, :]

    acc_ref[...] = jnp.zeros_like(acc_ref)

    def slc_chains(i, ms, mask):
        chosen = each(lambda sl: _dot(ext_ref[i], sl), sel)
        bias = each(lambda ch: jnp.concatenate([jnp.where(ch > 0.5, 0.0, MASK_NEG)] * NSA_HPG, axis=1), chosen)
        return [(ks_ref, g, g, i * SLC_KB, SLC_KB, ms[g], bias[g], mask) for g in groups]

    ms = lax.fori_loop(0, qi // SLC_KB, lambda i, ms: attend(slc_chains(i, ms, None)), (m0,) * NSA_KV_HEADS)

    n_win = NSA_WINDOW // QBLK + 1
    kb_win = jnp.maximum(qi - (n_win - 1), 0)
    attend(slc_chains(qi // SLC_KB, ms, "causal")
           + [(kw_ref, g, NSA_KV_HEADS + g, kb_win, n_win, m0, None, "window") for g in groups])

    o_groups = []
    for g in groups:
        def gate(branch):
            return jnp.concatenate(
                [gt_t[(g * NSA_HPG + hp) * 3 + branch:(g * NSA_HPG + hp) * 3 + branch + 1, :]
                 for hp in range(NSA_HPG)], axis=1)

        o_groups.append(gate(0) * o_cmp[g] + gate(1) * normalised(g) + gate(2) * normalised(NSA_KV_HEADS + g))

    o_all = jnp.concatenate(o_groups, axis=0)
    for hp in range(NSA_HPG):
        tile = o_all[:, QBLK * hp:QBLK * (hp + 1)].T
        for g in range(NSA_KV_HEADS):
            c0 = (g * NSA_HPG + hp) * HEAD_DIM
            o_ref[0, :, c0:c0 + HEAD_DIM] = tile[:, HEAD_DIM * g:HEAD_DIM * (g + 1)]


def _nsa_attn(qr, kk, vt, gt, kcmp, vcmp):
    bsz, s, _ = qr.shape
    n_blk = s // NSA_CMP_STRIDE
    n_cmp = n_blk - NSA_CMP_BLOCK // NSA_CMP_STRIDE + 1
    n_sel = s // NSA_SEL_BLOCK
    assert n_blk == QBLK, "compressed-score tiles are laid out one stride block per sublane row"
    c0 = np.arange(n_blk) * NSA_CMP_STRIDE
    s0 = np.arange(n_sel) * NSA_SEL_BLOCK
    lo = np.maximum(c0[:, None], s0[None, :])
    hi = np.minimum(c0[:, None] + NSA_CMP_BLOCK, s0[None, :] + NSA_SEL_BLOCK)
    overlap = (np.maximum(hi - lo, 0) / NSA_CMP_STRIDE).astype(np.float32)
    overlap[n_cmp:] = 0.0
    key = np.arange(s)
    expand = (key[:, None] // NSA_SEL_BLOCK == np.arange(n_sel)[None, :]).astype(np.float32)
    expand = expand.reshape(s // (SLC_KB * QBLK), SLC_KB * QBLK, n_sel)
    consts = [jnp.asarray(overlap.T), jnp.asarray(expand, BF16)]

    cmp_spec = pl.BlockSpec((1, n_blk, LANES), lambda b, i: (b, 0, 0))
    return pl.pallas_call(
        functools.partial(_nsa_attn_kernel, n_cmp=n_cmp),
        grid=(bsz, s // QBLK),
        in_specs=[pl.BlockSpec((1, QBLK, NSA_WIDTH), lambda b, i: (b, i, 0)),
                  pl.BlockSpec((1, QBLK, LANES), lambda b, i: (b, i, 0)),
                  cmp_spec, cmp_spec,
                  pl.BlockSpec((1, s, LANES), lambda b, i: (b, 0, 0)),
                  pl.BlockSpec((1, s, LANES), lambda b, i: (b, 0, 1)),
                  pl.BlockSpec((1, s // QBLK, 4 * VT_ROWS, QBLK), lambda b, i: (b, 0, 0, 0))]
        + [_full(a.shape) for a in consts],
        out_specs=pl.BlockSpec((1, QBLK, NSA_WIDTH), lambda b, i: (b, i, 0)),
        out_shape=jax.ShapeDtypeStruct((bsz, s, NSA_WIDTH), F32),
        scratch_shapes=[pltpu.VMEM((2 * NSA_KV_HEADS, VT_ROWS, NSA_HPG * QBLK), F32)],
        compiler_params=_params("parallel", "arbitrary"),
    )(qr, gt, kcmp, vcmp, kk, kk, vt, *consts)


def _mix_ffn_kernel(x_ref, og_ref, or_ref, on_ref, wo_ref, l1w_ref, l1b_ref, wg_ref, wu_ref, wd_ref,
                    l2w_ref, l2b_ref, o_ref, *, alpha):
    mix = (_dot(og_ref[...], wo_ref[0:LIN_WIDTH, :])
           + _dot(or_ref[...], wo_ref[LIN_WIDTH:2 * LIN_WIDTH, :])
           + _dot(on_ref[...], wo_ref[2 * LIN_WIDTH:, :]))
    x1 = _layer_norm(alpha * x_ref[...] + mix, l1w_ref[...], l1b_ref[...])
    xb = x1.astype(BF16)
    ffn = jnp.zeros_like(x1)
    for c0, c1 in FFN_SPLITS:
        gate = jnp.dot(xb, wg_ref[:, c0:c1], preferred_element_type=F32)
        up = jnp.dot(xb, wu_ref[:, c0:c1], preferred_element_type=F32)
        ffn = ffn + _dot(gate * _sigmoid(gate) * up, wd_ref[c0:c1, :])
    o_ref[...] = _layer_norm(alpha * x1 + ffn, l2w_ref[...], l2b_ref[...])


def _mix_ffn(x2, og, orw, on, wo, l1w, l1b, wg, wu, wd, l2w, l2b, alpha, tm=512):
    t = x2.shape[0]

    def rows(width):
        return pl.BlockSpec((tm, width), lambda i: (i, 0))

    def resident(a):
        return pl.BlockSpec(a.shape, lambda i: (0,) * a.ndim, pipeline_mode=pl.Buffered(1))

    consts = [wo, l1w, l1b, wg, wu, wd, l2w, l2b]
    return pl.pallas_call(
        functools.partial(_mix_ffn_kernel, alpha=alpha),
        grid=(t // tm,),
        in_specs=[rows(D_MODEL), rows(LIN_WIDTH), rows(LIN_WIDTH), rows(NSA_WIDTH)] + [resident(a) for a in consts],
        out_specs=rows(D_MODEL),
        out_shape=jax.ShapeDtypeStruct((t, D_MODEL), F32),
        compiler_params=_params("parallel"),
    )(x2, og, orw, on, *consts)


def _pad_cols(w, width):
    return jnp.pad(w, ((0, 0), (0, width - w.shape[1])))


def _pad_rows(w, height):
    return jnp.pad(w, ((0, height - w.shape[0]), (0, 0)))


def _rwkv_slab_cols(w, w_vres):
    o = 3 * LIN_WIDTH
    parts = [w[:, :o],
             _pad_cols(w[:, o:o + RWKV_DECAY_RANK], LANES),
             _pad_cols(w[:, o + RWKV_DECAY_RANK:o + RWKV_DECAY_RANK + RWKV_ICLR_RANK], LANES),
             _pad_cols(w[:, o + RWKV_DECAY_RANK + RWKV_ICLR_RANK:], 2 * LANES),
             _pad_cols(w_vres, LANES)]
    return jnp.concatenate(parts, axis=1)


def _rope_tables(s):
    half = HEAD_DIM // 2
    inv = ROPE_THETA ** (-jnp.arange(half, dtype=F32) / half)
    ang = jnp.arange(s, dtype=jnp.int32).astype(F32)[:, None] * inv
    cos, sin = jnp.cos(ang), jnp.sin(ang)
    cos2 = jnp.tile(jnp.concatenate([cos, cos], axis=1), (1, LANES // HEAD_DIM))
    sin2 = jnp.tile(jnp.concatenate([-sin, sin], axis=1), (1, LANES // HEAD_DIM))
    return cos2, sin2


def kernel(x, w_in, w_in_vres, gla_w_a2, gla_b_a, gla_ln_w, gla_ln_b, rwkv_mu, rwkv_mu_vres, rwkv_w0, rwkv_w2, rwkv_a0, rwkv_a2, rwkv_v0, rwkv_v2, rwkv_g2, rwkv_k_k, rwkv_k_a, rwkv_r_k, rwkv_ln_w, rwkv_ln_b, nsa_pos_k, nsa_pos_v, nsa_wk1, nsa_wk2, nsa_wv1, nsa_wv2, w_out, ln1_w, ln1_b, ffn_w_gate, ffn_w_up, ffn_w_down, ln2_w, ln2_b):
    bsz, s, d = x.shape
    depth = w_in.shape[0]
    alpha = float((2 * depth) ** 0.25)
    gla_cols = 4 * LIN_WIDTH + GLA_GATE_RANK
    rwkv_cols = 3 * LIN_WIDTH + RWKV_DECAY_RANK + RWKV_ICLR_RANK + RWKV_GATE_RANK
    cos2, sin2 = _rope_tables(s)
    row = lambda a: a.reshape(1, -1)

    x2 = x.reshape(bsz * s, d)
    v_first = None
    for l in range(depth):
        w = w_in[l]
        w_r = w[:, gla_cols:gla_cols + rwkv_cols]
        if l == 0:
            w_vres = jnp.zeros((d, RWKV_VRES_RANK), F32)
            mu_vres = jnp.zeros((1, RWKV_VRES_RANK), F32)
        else:
            w_vres = w_in_vres[l - 1]
            mu_vres = row(rwkv_mu_vres[l - 1])
        wg = _pad_cols(w[:, :gla_cols], GLA_SLAB).astype(BF16)
        wr = _rwkv_slab_cols(w_r, w_vres).astype(BF16)
        wn = _pad_cols(w[:, gla_cols + rwkv_cols:], NSA_SLAB).astype(BF16)
        hg, hr, qr, kvc, kk, vt, gt = _proj(x2, wg, wr, wn, cos2, sin2)
        hg = hg.reshape(bsz, s, GLA_SLAB)
        hr = hr.reshape(bsz, s, RWKV_SLAB)
        qr = qr.reshape(bsz, s, NSA_WIDTH)
        kvc = kvc.reshape(bsz, s, 2 * LANES)
        kk = kk.reshape(bsz, s, 2 * LANES)
        vt = vt.reshape(bsz, s // QBLK, 4 * VT_ROWS, QBLK)
        gt = gt.reshape(bsz, s, LANES)

        o_gla = _gla(hg, _pad_rows(gla_w_a2[l], LANES), row(gla_b_a[l]), row(gla_ln_w[l]), row(gla_ln_b[l]))

        mu = _rwkv_slab_cols(row(rwkv_mu[l]), mu_vres)
        small = [mu, row(rwkv_w0[l]), _pad_rows(rwkv_w2[l], LANES), row(rwkv_a0[l]),
                 _pad_rows(rwkv_a2[l], LANES), _pad_rows(rwkv_g2[l], 2 * LANES), row(rwkv_k_k[l]),
                 row(rwkv_k_a[l]), row(rwkv_r_k[l]), row(rwkv_ln_w[l]), row(rwkv_ln_b[l])]
        if l == 0:
            o_rwkv, v_first = _rwkv(hr, None, small)
        else:
            small += [row(rwkv_v0[l - 1]), _pad_rows(rwkv_v2[l - 1], LANES)]
            o_rwkv = _rwkv(hr, v_first, small)

        kcmp, vcmp = _nsa_cmp(kvc, nsa_pos_k[l], nsa_pos_v[l], nsa_wk1[l], nsa_wk2[l], nsa_wv1[l], nsa_wv2[l])
        o_nsa = _nsa_attn(qr, kk, vt, gt, kcmp, vcmp)

        x2 = _mix_ffn(x2, o_gla.reshape(bsz * s, LIN_WIDTH), o_rwkv.reshape(bsz * s, LIN_WIDTH),
                      o_nsa.reshape(bsz * s, NSA_WIDTH), w_out[l].astype(BF16), row(ln1_w[l]), row(ln1_b[l]),
                      ffn_w_gate[l].astype(BF16), ffn_w_up[l].astype(BF16), ffn_w_down[l].astype(BF16),
                      row(ln2_w[l]), row(ln2_b[l]), alpha)
    return x2.reshape(bsz, s, d)
```

```python
import functools

import numpy as np
import jax
import jax.numpy as jnp
from jax import lax
from jax.experimental import pallas as pl
from jax.experimental.pallas import tpu as pltpu

F32 = jnp.float32
BF16 = jnp.bfloat16

D_MODEL = 1024
HEAD_DIM = 64
N_LIN_HEADS = 4
LIN_WIDTH = N_LIN_HEADS * HEAD_DIM
GLA_GATE_RANK = 16
GLA_TAU = 16.0
RWKV_DECAY_RANK = 64
RWKV_ICLR_RANK = 64
RWKV_VRES_RANK = 32
RWKV_GATE_RANK = 160
RWKV_GN_EPS = 64e-5
NSA_HEADS = 8
NSA_KV_HEADS = 2
NSA_HPG = NSA_HEADS // NSA_KV_HEADS
NSA_WIDTH = NSA_HEADS * HEAD_DIM
NSA_KV_WIDTH = NSA_KV_HEADS * HEAD_DIM
NSA_CMP_BLOCK = 32
NSA_CMP_STRIDE = 16
NSA_CMP_HIDDEN = 256
NSA_SEL_BLOCK = 64
NSA_N_SELECT = 16
NSA_WINDOW = 512
ROPE_THETA = 10000.0
FFN_HIDDEN = 2816
FFN_SPLITS = ((0, 1536), (1536, FFN_HIDDEN))
LN_EPS = 1e-5
MASK_NEG = -1e30

LANES = 128
CHUNK = 64
SUB = 16
RWKV_GROUP = 4
GLA_GROUP = 4
RWKV_HB = 2
RW_BW = RWKV_HB * HEAD_DIM
QBLK = 128
SLC_KB = 4
VT_ROWS = HEAD_DIM + 16
LOG2E = 1.4426950408889634
VMEM_LIMIT = 56 * 1024 * 1024

GLA_SLAB = 4 * LIN_WIDTH + LANES
RWKV_SLAB = 3 * LIN_WIDTH + LANES + LANES + 2 * LANES + LANES
NSA_SLAB = NSA_WIDTH + 6 * NSA_KV_WIDTH + LANES
RW_WLR, RW_ALR, RW_GLR, RW_VLR = 768, 896, 1024, 1280


def _dot(a, b, dims=None):
    dims = dims or (((a.ndim - 1,), (0,)), ((), ()))
    return lax.dot_general(a.astype(BF16), b.astype(BF16), dims, preferred_element_type=F32)


def _split_bf16(a):
    hi = a.astype(BF16)
    return hi, (a - hi.astype(F32)).astype(BF16)


def _dot3(a, b_hi, b_lo, dims=None):
    dims = dims or (((a.ndim - 1,), (0,)), ((), ()))
    a_hi, a_lo = _split_bf16(a)
    d = functools.partial(lax.dot_general, dimension_numbers=dims, preferred_element_type=F32)
    return d(a_hi, b_hi) + (d(a_lo, b_hi) + d(a_hi, b_lo))


def _dot_exact(a, b):
    if a.dtype == BF16:
        hi, lo = _split_bf16(b)
        return jnp.dot(a, hi, preferred_element_type=F32) + jnp.dot(a, lo, preferred_element_type=F32)
    hi, lo = _split_bf16(a)
    return jnp.dot(hi, b, preferred_element_type=F32) + jnp.dot(lo, b, preferred_element_type=F32)


def _hi_lo(w):
    return jnp.stack(_split_bf16(w))


_NT = (((1,), (1,)), ((), ()))
_TN = (((0,), (0,)), ((), ()))


def _sigmoid(x):
    return 1.0 / (1.0 + jnp.exp(-x))


def _softplus(x):
    return jnp.maximum(x, 0.0) + jnp.log(1.0 + jnp.exp(-jnp.abs(x)))


def _head_norm(x, w, b, eps):
    mu = jnp.mean(x, axis=-1, keepdims=True)
    xc = x - mu
    var = jnp.mean(xc * xc, axis=-1, keepdims=True)
    return xc * lax.rsqrt(var + eps) * w + b


def _layer_norm(x, w, b):
    mu = jnp.mean(x, axis=-1, keepdims=True)
    xc = x - mu
    var = jnp.mean(xc * xc, axis=-1, keepdims=True)
    return xc * lax.rsqrt(var + LN_EPS) * w + b


def _params(*sem):
    return pltpu.CompilerParams(dimension_semantics=sem, vmem_limit_bytes=VMEM_LIMIT)


def _full(shape):
    nd = len(shape)
    return pl.BlockSpec(shape, lambda *_: (0,) * nd)


def _proj_kernel(x_ref, wg_ref, wr_ref, wn_ref, cos_ref, sin_ref,
                 og_ref, or_ref, q_ref, kvc_ref, kk_ref, vt_ref, gt_ref, *, n_qblk):
    xb = x_ref[...].astype(BF16)
    og_ref[...] = jnp.dot(xb, wg_ref[...], preferred_element_type=F32)
    or_ref[...] = jnp.dot(xb, wr_ref[...], preferred_element_type=F32)
    hn = jnp.dot(xb, wn_ref[...], preferred_element_type=F32)

    cos = cos_ref[...]
    sin = sin_ref[...]
    half = HEAD_DIM // 2
    first_half = (lax.broadcasted_iota(jnp.int32, cos.shape, 1) & (HEAD_DIM - 1)) < half

    def rope(x):
        swapped = jnp.where(first_half, pltpu.roll(x, LANES - half, 1), pltpu.roll(x, half, 1))
        return x * cos + swapped * sin

    def col(i):
        return hn[:, NSA_WIDTH + LANES * i:NSA_WIDTH + LANES * (i + 1)]

    for i in range(NSA_WIDTH // LANES):
        cs = slice(LANES * i, LANES * (i + 1))
        q_ref[:, cs] = rope(hn[:, cs]) * (HEAD_DIM ** -0.5)
    kvc_ref[:, 0:LANES] = rope(col(0))
    kvc_ref[:, LANES:2 * LANES] = col(1)
    kk_ref[:, 0:LANES] = rope(col(2)).astype(BF16)
    kk_ref[:, LANES:2 * LANES] = rope(col(4)).astype(BF16)
    pad = VT_ROWS - HEAD_DIM
    ones_rows = jnp.where(lax.broadcasted_iota(jnp.int32, (pad, QBLK), 0) == 0, 1.0, 0.0).astype(BF16)
    for i in range(n_qblk):
        rs = slice(QBLK * i, QBLK * (i + 1))
        for branch, c in enumerate((3, 5)):
            v_t = col(c)[rs, :].T.astype(BF16)
            for g in range(NSA_KV_HEADS):
                r0 = (branch * NSA_KV_HEADS + g) * VT_ROWS
                vt_ref[i, r0:r0 + HEAD_DIM, :] = v_t[HEAD_DIM * g:HEAD_DIM * (g + 1), :]
                vt_ref[i, r0 + HEAD_DIM:r0 + VT_ROWS, :] = ones_rows
    gt_ref[...] = _sigmoid(hn[:, NSA_WIDTH + 6 * LANES:NSA_SLAB])


def _proj(x2, wg, wr, wn, cos2, sin2, tm=512):
    t = x2.shape[0]
    s = cos2.shape[0]
    n_qblk = tm // QBLK

    def rows(width):
        return pl.BlockSpec((tm, width), lambda i: (i, 0))

    def resident(a):
        return pl.BlockSpec(a.shape, lambda i: (0,) * a.ndim, pipeline_mode=pl.Buffered(1))

    pos_rows = pl.BlockSpec((tm, LANES), lambda i: (i % (s // tm), 0))
    return pl.pallas_call(
        functools.partial(_proj_kernel, n_qblk=n_qblk),
        grid=(t // tm,),
        in_specs=[rows(D_MODEL), resident(wg), resident(wr), resident(wn), pos_rows, pos_rows],
        out_specs=[rows(GLA_SLAB), rows(RWKV_SLAB), rows(NSA_WIDTH), rows(2 * LANES), rows(2 * LANES),
                   pl.BlockSpec((n_qblk, 4 * VT_ROWS, QBLK), lambda i: (i, 0, 0)), rows(LANES)],
        out_shape=[jax.ShapeDtypeStruct((t, GLA_SLAB), F32),
                   jax.ShapeDtypeStruct((t, RWKV_SLAB), F32),
                   jax.ShapeDtypeStruct((t, NSA_WIDTH), F32),
                   jax.ShapeDtypeStruct((t, 2 * LANES), F32),
                   jax.ShapeDtypeStruct((t, 2 * LANES), BF16),
                   jax.ShapeDtypeStruct((t // QBLK, 4 * VT_ROWS, QBLK), BF16),
                   jax.ShapeDtypeStruct((t, LANES), F32)],
        compiler_params=_params("parallel"),
    )(x2, wg, wr, wn, cos2, sin2)


def _group_rows(x, u):
    return jnp.concatenate(
        [jnp.broadcast_to(x[SUB * i + u:SUB * i + u + 1, :], (SUB, x.shape[1]))
         for i in range(CHUNK // SUB)], axis=0)


def _gla_kernel(h_ref, wa2_ref, ba_ref, lnw_ref, lnb_ref, tril_ref, blk_ref, ones_ref, dm_ref, same_ref,
                same_sub_ref, o_ref, st_ref, *, n_chunks):
    @pl.when(pl.program_id(1) == 0)
    def _():
        st_ref[...] = jnp.zeros_like(st_ref)

    row_in_sub = lax.broadcasted_iota(jnp.int32, (CHUNK, LIN_WIDTH), 0) % SUB
    dm = dm_ref[...]
    heads = range(N_LIN_HEADS)
    n_sub = CHUNK // SUB

    def own_head(m, mask_ref):
        return jnp.concatenate([m] * N_LIN_HEADS, axis=0) * mask_ref[...]

    def each(f, *lists):
        return [f(*args) for args in zip(*lists)]

    def chunk_group(gi, carry):
        rows = [pl.ds(pl.multiple_of((gi * GLA_GROUP + j) * CHUNK, CHUNK), CHUNK) for j in range(GLA_GROUP)]
        ps = [h_ref[0, rw, :] for rw in rows]
        q = each(lambda p: p[:, 0:256] * (HEAD_DIM ** -0.5), ps)
        k = each(lambda p: p[:, 256:512], ps)
        v = each(lambda p: p[:, 512:768], ps)
        g = each(lambda p: p[:, 768:1024], ps)
        z = each(lambda p: _dot3(p[:, 1024:1152], wa2_ref[0], wa2_ref[1]) + ba_ref[...], ps)
        log_a = each(lambda x: -_softplus(-x) * (1.0 / GLA_TAU), z)
        b = each(lambda x: _dot_exact(tril_ref[...], x), log_a)
        b_end = each(lambda x: _dot_exact(blk_ref[...], x), log_a)
        qe = each(lambda x, bb: x * jnp.exp(bb), q, b)
        kd = each(lambda x, be, bb: x * jnp.exp(be - bb), k, b_end, b)
        p_end = each(jnp.exp, b_end)
        v_st = each(lambda x: jnp.concatenate([x[:, HEAD_DIM * h:HEAD_DIM * (h + 1)] for h in heads], axis=0), v)

        at = [jnp.zeros((CHUNK, LIN_WIDTH), F32) for _ in ps]
        for u0 in range(0, SUB, N_LIN_HEADS):
            us = range(u0, u0 + N_LIN_HEADS)
            es = each(lambda bb, qq, kk: jnp.concatenate(
                [jnp.exp(jnp.where(row_in_sub <= u, _group_rows(bb, u) - bb, MASK_NEG)) * (_group_rows(qq, u) * kk)
                 for u in us], axis=0), b, q, k)
            a_rep = each(lambda e: _dot(e, ones_ref[...]), es)
            for i, u in enumerate(us):
                at = each(lambda acc, ar: acc + jnp.where(dm == u, ar[CHUNK * i:CHUNK * (i + 1), :], 0.0), at, a_rep)
        o_st = each(lambda x, vs: _dot(own_head(x, same_ref), vs, _TN), at, v_st)

        upd, q_own = [], []
        for i in range(n_sub):
            rs = slice(SUB * i, SUB * (i + 1))
            upd.append(each(lambda vs, kk: _dot(
                jnp.concatenate([vs[CHUNK * h + SUB * i:CHUNK * h + SUB * (i + 1), :] for h in heads], axis=0),
                own_head(kk[rs], same_sub_ref), _TN), v_st, kd))
            q_own.append(each(lambda x: own_head(x[rs], same_sub_ref), qe))
        st = st_ref[...]
        for c, rw in enumerate(rows):
            inter = []
            for i in range(n_sub):
                inter.append(_dot(q_own[i][c], st, _NT))
                st = st * p_end[c][SUB * i:SUB * i + 1, :] + upd[i][c]
            o_c = o_st[c] + jnp.concatenate(
                [inter[i][SUB * h:SUB * (h + 1), :] for h in heads for i in range(n_sub)], axis=0)
            o_c = _head_norm(o_c, 1.0, 0.0, LN_EPS)
            o = jnp.concatenate([o_c[CHUNK * h:CHUNK * (h + 1), :] for h in heads], axis=1)
            o_ref[0, rw, :] = (o * lnw_ref[...] + lnb_ref[...]) * (g[c] * _sigmoid(g[c]))
        st_ref[...] = st
        return carry

    lax.fori_loop(0, n_chunks // GLA_GROUP, chunk_group, 0)


def _gla(hg, wa2, ba, lnw, lnb, ts=512):
    bsz, s, _ = hg.shape
    r = np.arange(CHUNK)
    tril = ((r[:, None] // SUB == r[None, :] // SUB) & (r[None, :] <= r[:, None])).astype(np.float32)
    blk = (r[:, None] // SUB == r[None, :] // SUB).astype(np.float32)
    c = np.arange(LIN_WIDTH)
    same = (c[:, None] // HEAD_DIM == c[None, :] // HEAD_DIM).astype(np.float32)
    same_sub = (r[:, None] // SUB == c[None, :] // HEAD_DIM).astype(np.float32)
    dm = ((c[None, :] % HEAD_DIM) - SUB * (r[:, None] // SUB)).astype(np.int32)
    consts = [jnp.asarray(tril, BF16), jnp.asarray(blk, BF16), jnp.asarray(same, BF16), jnp.asarray(dm), jnp.asarray(same),
              jnp.asarray(same_sub)]
    small = [wa2, ba, lnw, lnb] + consts
    return pl.pallas_call(
        functools.partial(_gla_kernel, n_chunks=ts // CHUNK),
        grid=(bsz, s // ts),
        in_specs=[pl.BlockSpec((1, ts, GLA_SLAB), lambda b, j: (b, j, 0))] + [_full(a.shape) for a in small],
        out_specs=pl.BlockSpec((1, ts, LIN_WIDTH), lambda b, j: (b, j, 0)),
        out_shape=jax.ShapeDtypeStruct((bsz, s, LIN_WIDTH), F32),
        scratch_shapes=[pltpu.VMEM((HEAD_DIM, LIN_WIDTH), F32)],
        compiler_params=_params("parallel", "arbitrary"),
    )(hg, *small)


def _rwkv_kernel(*refs, n_chunks, has_vres):
    if has_vres:
        (h_ref, vf_ref, mu_ref, w0_ref, w2_ref, a0_ref, a2_ref, g2_ref, kk_ref, ka_ref, rk_ref,
         lnw_ref, lnb_ref, v0_ref, v2_ref, tril_ref, ones_ref, same_ref, strict_ref, incl_ref, eye_ref,
         o_ref, st_ref, prev_ref) = refs
    else:
        (h_ref, mu_ref, w0_ref, w2_ref, a0_ref, a2_ref, g2_ref, kk_ref, ka_ref, rk_ref,
         lnw_ref, lnb_ref, tril_ref, ones_ref, same_ref, strict_ref, incl_ref, eye_ref,
         o_ref, vf_ref, st_ref, prev_ref) = refs

    @pl.when(pl.program_id(1) == 0)
    def _():
        st_ref[...] = jnp.zeros_like(st_ref)
        prev_ref[...] = jnp.zeros_like(prev_ref)

    row = lax.broadcasted_iota(jnp.int32, (CHUNK, RWKV_SLAB), 0)
    blocks = range(N_LIN_HEADS // RWKV_HB)
    in_block = range(RWKV_HB)
    nr = RWKV_HB * CHUNK

    def tile_rows(m):
        return jnp.concatenate([m] * RWKV_HB, axis=0)

    def own_head(m):
        return tile_rows(m) * same_ref[...]

    def stack(m):
        return jnp.concatenate([m[:, HEAD_DIM * h:HEAD_DIM * (h + 1)] for h in in_block], axis=0)

    def unstack(m):
        return jnp.concatenate([m[CHUNK * h:CHUNK * (h + 1), :] for h in in_block], axis=1)

    def each(f, *lists):
        return [f(*args) for args in zip(*lists)]

    def per_block(xs):
        return [x[:, RW_BW * bi:RW_BW * (bi + 1)] for x in xs for bi in blocks]

    def prepare(ps, prev_rows, vfs):
        prev = each(lambda p, pr: jnp.where(row == 0, pr, pltpu.roll(p, 1, 0)), ps, prev_rows)
        xm = each(lambda p, pv: p + (pv - p) * mu_ref[...], ps, prev)
        r = each(lambda x: x[:, 0:256], xm)
        k = each(lambda x: x[:, 256:512], xm)
        v = each(lambda x: x[:, 512:768], xm)
        v_shift = v
        w_log = each(lambda x: -_softplus(-(w0_ref[...] + _dot3(jnp.tanh(x[:, RW_WLR:RW_ALR]), w2_ref[0], w2_ref[1])))
                     - 0.5, xm)
        lw = each(lambda w: -jnp.exp(w), w_log)
        a = each(lambda x: _sigmoid(a0_ref[...] + _dot3(x[:, RW_ALR:RW_GLR], a2_ref[0], a2_ref[1])), xm)
        g = each(lambda x: _dot3(_sigmoid(x[:, RW_GLR:RW_VLR]), g2_ref[0], g2_ref[1]), xm)
        if has_vres:
            mixv = each(lambda x: _sigmoid(v0_ref[...] + _dot3(x[:, RW_VLR:RWKV_SLAB], v2_ref[0], v2_ref[1])), xm)
            v = each(lambda vv, vf, mx: vv + (vf - vv) * mx, v, vfs, mixv)
        kkv = each(lambda kk: kk * kk_ref[...], k)
        nrm = each(lambda kv: jnp.sqrt(_dot_exact(kv * kv, ones_ref[...])), kkv)
        kkn = each(lambda kv, n: kv / jnp.maximum(n, 1e-12), kkv, nrm)
        k2 = each(lambda kk, aa: kk * (1.0 + (aa - 1.0) * ka_ref[...]), k, a)
        bv = each(lambda kn, aa: kn * aa, kkn, a)
        c = each(lambda w: _dot_exact(tril_ref[...], w), lw)
        c_end = each(lambda cc: cc[CHUNK - 1:CHUNK, :], c)
        bonus = each(lambda rr, kk, vv: _dot_exact(rr * kk * rk_ref[...], ones_ref[...]) * vv, r, k2, v)
        at = each(lambda kn, cc, w: -kn * jnp.exp(cc - w), kkn, c, lw)
        rt = each(lambda rr, cc: rr * jnp.exp(cc), r, c)
        bt = each(lambda b_, cc: b_ * jnp.exp(-cc), bv, c)
        kt = each(lambda kk, cc: kk * jnp.exp(-cc), k2, c)
        b_rest = each(lambda b_, ce, cc: b_ * jnp.exp(ce - cc), bv, c_end, c)
        k_rest = each(lambda kk, ce, cc: kk * jnp.exp(ce - cc), k2, c_end, c)
        p_end = each(jnp.exp, c_end)

        ar = each(lambda x, y: jnp.concatenate([own_head(x), own_head(y)], axis=0), per_block(at), per_block(rt))
        bk = each(lambda x, y: jnp.concatenate([tile_rows(x), tile_rows(y)], axis=0), per_block(bt), per_block(kt))
        gm = each(lambda x, y: _dot(x, y, _NT), ar, bk)
        n_ab = each(lambda m: m[0:nr, 0:nr] * strict_ref[...], gm)
        m_ak = each(lambda m: m[0:nr, nr:2 * nr] * strict_ref[...], gm)
        r_bk = each(lambda m: m[nr:2 * nr, :] * jnp.concatenate([incl_ref[...], incl_ref[...]], axis=1), gm)
        v_st = each(stack, per_block(v))
        mv = each(_dot, m_ak, v_st)
        inv = each(lambda n: eye_ref[...] + n, n_ab)
        pw = n_ab
        for _ in range(5):
            pw = each(lambda x: _dot(x, x), pw)
            inv = each(lambda i_, x: i_ + _dot(i_, x), inv, pw)
        bk_rest = each(lambda x, y: jnp.concatenate([own_head(x), own_head(y)], axis=0),
                       per_block(b_rest), per_block(k_rest))
        a_bar = each(lambda i_, x: _dot(i_, x[0:nr]), inv, ar)
        uv0 = each(lambda i_, m, vs: jnp.concatenate([_dot(i_, m), vs], axis=0), inv, mv, v_st)
        r_bar = each(lambda x, rb, ab: x[nr:2 * nr] + _dot(rb[:, 0:nr], ab), ar, r_bk, a_bar)
        y0 = each(_dot, r_bk, uv0)
        g_mat = each(lambda ab, br: _dot(ab, br[0:nr], _TN), a_bar, bk_rest)
        sc = each(lambda z, br: _dot(z, br, _TN), uv0, bk_rest)
        items = list(zip(r_bar, y0, g_mat, sc, per_block(p_end)))
        n_blk = len(blocks)
        return [(items[n_blk * j:n_blk * (j + 1)], bonus[j], g[j]) for j in range(len(ps))], v_shift

    def advance(prep, s0):
        items, bonus, g = prep
        r_bar, y0, g_mat, sc, p_end = (list(x) for x in zip(*items))
        s0b = per_block([s0])
        y = each(lambda rb, s_, y_: unstack(_head_norm(_dot(rb, s_, _NT) + y_, 1.0, 0.0, RWKV_GN_EPS)), r_bar, s0b, y0)
        s1 = each(lambda s_, pe, gm_, c_: s_ * pe + _dot(s_, gm_) + c_, s0b, p_end, g_mat, sc)
        y = jnp.concatenate(y, axis=1)
        return (y * lnw_ref[...] + lnb_ref[...] + bonus) * g, jnp.concatenate(s1, axis=1)

    def chunk_group(gi, carry):
        rows = [pl.ds(pl.multiple_of((gi * RWKV_GROUP + j) * CHUNK, CHUNK), CHUNK) for j in range(RWKV_GROUP)]
        ps = [h_ref[0, rw, :] for rw in rows]
        vfs = [vf_ref[0, rw, :] if has_vres else None for rw in rows]
        prev_rows = [prev_ref[...]] + [p[CHUNK - 1:CHUNK, :] for p in ps[:-1]]
        preps, shifted = prepare(ps, prev_rows, vfs)
        s = st_ref[...]
        for j, rw in enumerate(rows):
            out, s = advance(preps[j], s)
            o_ref[0, rw, :] = out
            if not has_vres:
                vf_ref[0, rw, :] = shifted[j]
        prev_ref[...] = ps[-1][CHUNK - 1:CHUNK, :]
        st_ref[...] = s
        return carry

    lax.fori_loop(0, n_chunks // RWKV_GROUP, chunk_group, 0)


def _rwkv(hr, v_first, small, ts=512):
    bsz, s, _ = hr.shape
    has_vres = v_first is not None
    r = np.arange(CHUNK)
    tril = (r[None, :] <= r[:, None]).astype(np.float32)
    i = np.arange(LIN_WIDTH)
    ones = (i[:, None] // HEAD_DIM == i[None, :] // HEAD_DIM).astype(np.float32)
    i = np.arange(RW_BW)
    same = (i[:, None] // HEAD_DIM == i[None, :] // HEAD_DIM).astype(np.float32)
    t_in, s_in = i[:, None] % CHUNK, i[None, :] % CHUNK
    consts = [tril, ones, same, same * (s_in < t_in), same * (s_in <= t_in), np.eye(RW_BW, dtype=np.float32)]
    small = list(small) + [jnp.asarray(c, BF16 if i < 2 else F32) for i, c in enumerate(consts)]
    seq_spec = pl.BlockSpec((1, ts, LIN_WIDTH), lambda b, j: (b, j, 0))
    in_specs = [pl.BlockSpec((1, ts, RWKV_SLAB), lambda b, j: (b, j, 0))]
    args = [hr]
    if has_vres:
        in_specs.append(seq_spec)
        args.append(v_first)
    in_specs += [_full(a.shape) for a in small]
    out_sd = jax.ShapeDtypeStruct((bsz, s, LIN_WIDTH), F32)
    return pl.pallas_call(
        functools.partial(_rwkv_kernel, n_chunks=ts // CHUNK, has_vres=has_vres),
        grid=(bsz, s // ts),
        in_specs=in_specs,
        out_specs=seq_spec if has_vres else [seq_spec, seq_spec],
        out_shape=out_sd if has_vres else [out_sd, out_sd],
        scratch_shapes=[pltpu.VMEM((HEAD_DIM, LIN_WIDTH), F32),
                        pltpu.VMEM((1, RWKV_SLAB), F32)],
        compiler_params=_params("parallel", "arbitrary"),
    )(*args, *small)


def _nsa_cmp_kernel(kc_ref, vc_ref, pk_ref, pv_ref, wk1h_ref, wk1l_ref, wk2h_ref, wk2l_ref,
                    wv1h_ref, wv1l_ref, wv2h_ref, wv2l_ref, ko_ref, vo_ref, *, n_blk):
    half = NSA_CMP_BLOCK // 2

    def compress(t_ref, pos_ref, w1h_ref, w1l_ref, w2h_ref, w2l_ref, o_ref):
        for g in range(NSA_KV_HEADS):
            gs = slice(HEAD_DIM * g, HEAD_DIM * (g + 1))
            top = jnp.zeros((n_blk, NSA_CMP_HIDDEN), F32)
            bot = jnp.zeros((n_blk, NSA_CMP_HIDDEN), F32)
            for l in range(half):
                rows = t_ref[0, pl.ds(l, n_blk, stride=NSA_CMP_STRIDE), :][:, gs]
                top = top + _dot3(rows + pos_ref[l:l + 1, :], w1h_ref[l], w1l_ref[l])
                bot = bot + _dot3(rows + pos_ref[half + l:half + l + 1, :], w1h_ref[half + l], w1l_ref[half + l])
            hid = top + pltpu.roll(bot, n_blk - 1, 0)
            act = 0.5 * hid * (1.0 + jnp.tanh(0.7978845608028654 * (hid + 0.044715 * hid * hid * hid)))
            o_ref[0, :, gs] = _dot3(act, w2h_ref[...], w2l_ref[...])

    compress(kc_ref, pk_ref, wk1h_ref, wk1l_ref, wk2h_ref, wk2l_ref, ko_ref)
    compress(vc_ref, pv_ref, wv1h_ref, wv1l_ref, wv2h_ref, wv2l_ref, vo_ref)


def _nsa_cmp(kv, pos_k, pos_v, wk1, wk2, wv1, wv2):
    bsz, s, _ = kv.shape
    n_blk = s // NSA_CMP_STRIDE
    wk1 = wk1.reshape(NSA_CMP_BLOCK, HEAD_DIM, NSA_CMP_HIDDEN)
    wv1 = wv1.reshape(NSA_CMP_BLOCK, HEAD_DIM, NSA_CMP_HIDDEN)
    small = [pos_k, pos_v, *_split_bf16(wk1), *_split_bf16(wk2), *_split_bf16(wv1), *_split_bf16(wv2)]
    out_spec = pl.BlockSpec((1, n_blk, LANES), lambda b: (b, 0, 0))
    out_sd = jax.ShapeDtypeStruct((bsz, n_blk, LANES), F32)
    return pl.pallas_call(
        functools.partial(_nsa_cmp_kernel, n_blk=n_blk),
        grid=(bsz,),
        in_specs=[pl.BlockSpec((1, s, LANES), lambda b: (b, 0, 0)),
                  pl.BlockSpec((1, s, LANES), lambda b: (b, 0, 1))] + [_full(a.shape) for a in small],
        out_specs=[out_spec, out_spec],
        out_shape=[out_sd, out_sd],
        compiler_params=_params("parallel"),
    )(kv, kv, *small)


def _nsa_attn_kernel(q_ref, gt_ref, kc_ref, vc_ref, ks_ref, kw_ref, vt_ref, ovt_ref, ext_ref,
                     o_ref, acc_ref, s_ref, cm_ref, m_ref, *, n_cmp):
    qi = pl.program_id(1)
    cols = NSA_HPG * QBLK
    n_sel_blk = ovt_ref.shape[0]
    key_off = lax.broadcasted_iota(jnp.int32, (QBLK, cols), 0)
    t_pos = qi * QBLK + (lax.broadcasted_iota(jnp.int32, (QBLK, cols), 1) & (QBLK - 1))
    gt_t = gt_ref[0].T
    zeros64 = jnp.zeros((QBLK, HEAD_DIM), F32)
    m0 = jnp.full((1, cols), MASK_NEG, F32)

    groups = list(range(NSA_KV_HEADS))

    def each(f, *lists):
        return [f(*args) for args in zip(*lists)]

    def stacked_q(g):
        pieces = []
        for hp in range(NSA_HPG):
            qh = q_ref[0, :, (g * NSA_HPG + hp) * HEAD_DIM:(g * NSA_HPG + hp + 1) * HEAD_DIM]
            pieces.append(jnp.concatenate([qh, zeros64] if g == 0 else [zeros64, qh], axis=1))
        return jnp.concatenate(pieces, axis=0)

    qs = each(stacked_q, groups)
    qs_b = each(lambda q: (q * LOG2E).astype(BF16), qs)

    def run_together(*tracers):
        out = [None] * len(tracers)
        live = dict(enumerate(tracers))
        while live:
            for i in list(live):
                try:
                    next(live[i])
                except StopIteration as stop:
                    out[i] = stop.value
                    del live[i]
        return out

    def select_steps():
        valid_c = (key_off * NSA_CMP_STRIDE + (NSA_CMP_BLOCK - 1) <= t_pos) & (key_off < n_cmp)
        s_c = each(lambda q: jnp.where(valid_c, _dot3(kc_ref[0], *_split_bf16(q), _NT), MASK_NEG), qs)
        yield
        e_c = each(lambda x: jnp.exp(x - jnp.max(x, axis=0, keepdims=True)), s_c)
        yield
        p_c = each(lambda e: jnp.where(valid_c, e / jnp.sum(e, axis=0, keepdims=True), 0.0), e_c)
        yield
        o_cmp = each(lambda p, g: _dot(vc_ref[0], p, _TN)[HEAD_DIM * g:HEAD_DIM * (g + 1), :], p_c, groups)
        yield
        p_sum = each(lambda p: p[:, 0:QBLK] + p[:, QBLK:2 * QBLK] + p[:, 2 * QBLK:3 * QBLK] + p[:, 3 * QBLK:], p_c)
        imp = each(lambda p: _dot_exact(ovt_ref[...], p), p_sum)
        yield
        j = lax.broadcasted_iota(jnp.int32, (n_sel_blk, QBLK), 0)
        tq = qi * QBLK + lax.broadcasted_iota(jnp.int32, (n_sel_blk, QBLK), 1)
        cur = tq // NSA_SEL_BLOCK
        forced = (j == 0) | (j == cur) | (j == cur - 1)
        score = each(lambda x: jnp.where(forced, 1e9, jnp.where(j > cur, -1e9, x)), imp)
        rank = [jnp.zeros((n_sel_blk, QBLK), F32) for _ in groups]
        for i in range(n_sel_blk):
            rank = each(lambda rk, sc: rk + jnp.where(
                (sc[i:i + 1, :] > sc) | ((sc[i:i + 1, :] == sc) & (j > i)), 1.0, 0.0), rank, score)
            if i % 4 == 3:
                yield
        return o_cmp, each(lambda rk: jnp.where(rk < NSA_N_SELECT, 1.0, 0.0), rank)

    def key_scores(k_ref, g, kb0, n_kb, bias, mask):
        k0 = pl.multiple_of(kb0 * QBLK, QBLK)
        s = lax.dot_general(k_ref[0, pl.ds(k0, n_kb * QBLK), :], qs_b[g], _NT, preferred_element_type=F32)
        if bias is not None:
            s = s + bias
        if mask is not None:
            key_pos = k0 + lax.broadcasted_iota(jnp.int32, (n_kb * QBLK, cols), 0)
            t_all = jnp.concatenate([t_pos] * n_kb, axis=0)
            valid = key_pos <= t_all
            if mask == "window":
                valid = valid & (key_pos > t_all - NSA_WINDOW)
            s = jnp.where(valid, s, MASK_NEG)
        return s

    def softmax_steps(slabs, kb0, n_kb, block_scores, col_max, m_old):
        m_new = each(jnp.maximum, m_old, col_max)
        pv = [jnp.zeros((VT_ROWS, cols), F32) for _ in slabs]
        for i in range(n_kb):
            pv = [acc + jnp.dot(vt_ref[0, kb0 + i, slab * VT_ROWS:(slab + 1) * VT_ROWS, :],
                                jnp.exp2(rd(i) - mn).astype(BF16), preferred_element_type=F32)
                  for acc, rd, mn, slab in zip(pv, block_scores, m_new, slabs)]
            yield
        for slab, m, mn, x in zip(slabs, m_old, m_new, pv):
            acc_ref[slab] = acc_ref[slab] * jnp.exp2(m - mn) + x
        return m_new

    def window_steps():
        n_win = NSA_WINDOW // QBLK + 1
        kb_win = jnp.maximum(qi - (n_win - 1), 0)
        s = []
        for g in groups:
            s.append(key_scores(kw_ref, g, kb_win, n_win, None, "window"))
            yield
        col_max = each(lambda x: jnp.max(x, axis=0, keepdims=True), s)
        yield
        blocks = [lambda i, x=x: x[QBLK * i:QBLK * (i + 1), :] for x in s]
        yield from softmax_steps([NSA_KV_HEADS + g for g in groups], kb_win, n_win, blocks, col_max, [m0] * len(s))

    def normalised(slab):
        return acc_ref[slab, 0:HEAD_DIM, :] / acc_ref[slab, HEAD_DIM:HEAD_DIM + 1, :]

    acc_ref[...] = jnp.zeros_like(acc_ref)
    for g in groups:
        m_ref[g] = m0

    (o_cmp, sel), _ = run_together(select_steps(), window_steps())

    def score_steps(i, mask):
        chosen = each(lambda sl: _dot(ext_ref[i], sl), sel)
        bias = each(lambda ch: jnp.concatenate([jnp.where(ch > 0.5, 0.0, MASK_NEG)] * NSA_HPG, axis=1), chosen)
        s = []
        for g in groups:
            s.append(key_scores(ks_ref, g, i * SLC_KB, SLC_KB, bias[g], mask))
            yield
        return s, each(lambda x: jnp.max(x, axis=0, keepdims=True), s)

    def park(scores_and_max):
        for g, (x, cm) in enumerate(zip(*scores_and_max)):
            s_ref[g] = x
            cm_ref[g] = cm

    def consume_steps(i):
        blocks = [lambda b, g=g: s_ref[g, QBLK * b:QBLK * (b + 1), :] for g in groups]
        m_new = yield from softmax_steps(groups, i * SLC_KB, SLC_KB, blocks, [cm_ref[g] for g in groups],
                                         [m_ref[g] for g in groups])
        for g in groups:
            m_ref[g] = m_new[g]

    n_full = qi // SLC_KB

    @pl.when(n_full >= 1)
    def _():
        park(run_together(score_steps(0, None))[0])

    def pipelined(i, carry):
        nxt, _ = run_together(score_steps(i + 1, None), consume_steps(i))
        park(nxt)
        return carry

    lax.fori_loop(0, n_full - 1, pipelined, 0)

    @pl.when(n_full >= 1)
    def _():
        nxt, _ = run_together(score_steps(n_full, "causal"), consume_steps(n_full - 1))
        park(nxt)

    @pl.when(n_full == 0)
    def _():
        park(run_together(score_steps(0, "causal"))[0])

    run_together(consume_steps(n_full))

    o_groups = []
    for g in groups:
        def gate(branch):
            return jnp.concatenate(
                [gt_t[(g * NSA_HPG + hp) * 3 + branch:(g * NSA_HPG + hp) * 3 + branch + 1, :]
                 for hp in range(NSA_HPG)], axis=1)

        o_groups.append(gate(0) * o_cmp[g] + gate(1) * normalised(g) + gate(2) * normalised(NSA_KV_HEADS + g))

    o_all = jnp.concatenate(o_groups, axis=0)
    for hp in range(NSA_HPG):
        tile = o_all[:, QBLK * hp:QBLK * (hp + 1)].T
        for g in range(NSA_KV_HEADS):
            c0 = (g * NSA_HPG + hp) * HEAD_DIM
            o_ref[0, :, c0:c0 + HEAD_DIM] = tile[:, HEAD_DIM * g:HEAD_DIM * (g + 1)]


def _nsa_attn(qr, kk, vt, gt, kcmp, vcmp):
    bsz, s, _ = qr.shape
    n_blk = s // NSA_CMP_STRIDE
    n_cmp = n_blk - NSA_CMP_BLOCK // NSA_CMP_STRIDE + 1
    n_sel = s // NSA_SEL_BLOCK
    assert n_blk == QBLK, "compressed-score tiles are laid out one stride block per sublane row"
    c0 = np.arange(n_blk) * NSA_CMP_STRIDE
    s0 = np.arange(n_sel) * NSA_SEL_BLOCK
    lo = np.maximum(c0[:, None], s0[None, :])
    hi = np.minimum(c0[:, None] + NSA_CMP_BLOCK, s0[None, :] + NSA_SEL_BLOCK)
    overlap = (np.maximum(hi - lo, 0) / NSA_CMP_STRIDE).astype(np.float32)
    overlap[n_cmp:] = 0.0
    key = np.arange(s)
    expand = (key[:, None] // NSA_SEL_BLOCK == np.arange(n_sel)[None, :]).astype(np.float32)
    expand = expand.reshape(s // (SLC_KB * QBLK), SLC_KB * QBLK, n_sel)
    consts = [jnp.asarray(overlap.T, BF16), jnp.asarray(expand, BF16)]

    cmp_spec = pl.BlockSpec((1, n_blk, LANES), lambda b, i: (b, 0, 0))
    return pl.pallas_call(
        functools.partial(_nsa_attn_kernel, n_cmp=n_cmp),
        grid=(bsz, s // QBLK),
        in_specs=[pl.BlockSpec((1, QBLK, NSA_WIDTH), lambda b, i: (b, i, 0)),
                  pl.BlockSpec((1, QBLK, LANES), lambda b, i: (b, i, 0)),
                  cmp_spec, cmp_spec,
                  pl.BlockSpec((1, s, LANES), lambda b, i: (b, 0, 0)),
                  pl.BlockSpec((1, s, LANES), lambda b, i: (b, 0, 1)),
                  pl.BlockSpec((1, s // QBLK, 4 * VT_ROWS, QBLK), lambda b, i: (b, 0, 0, 0))]
        + [_full(a.shape) for a in consts],
        out_specs=pl.BlockSpec((1, QBLK, NSA_WIDTH), lambda b, i: (b, i, 0)),
        out_shape=jax.ShapeDtypeStruct((bsz, s, NSA_WIDTH), F32),
        scratch_shapes=[pltpu.VMEM((2 * NSA_KV_HEADS, VT_ROWS, NSA_HPG * QBLK), F32),
                        pltpu.VMEM((NSA_KV_HEADS, SLC_KB * QBLK, NSA_HPG * QBLK), F32),
                        pltpu.VMEM((NSA_KV_HEADS, 1, NSA_HPG * QBLK), F32),
                        pltpu.VMEM((NSA_KV_HEADS, 1, NSA_HPG * QBLK), F32)],
        compiler_params=_params("parallel", "arbitrary"),
    )(qr, gt, kcmp, vcmp, kk, kk, vt, *consts)


def _mix_ffn_kernel(x_ref, og_ref, or_ref, on_ref, wo_ref, l1w_ref, l1b_ref, wg_ref, wu_ref, wd_ref,
                    l2w_ref, l2b_ref, o_ref, *, alpha):
    mix = (_dot(og_ref[...], wo_ref[0:LIN_WIDTH, :])
           + _dot(or_ref[...], wo_ref[LIN_WIDTH:2 * LIN_WIDTH, :])
           + _dot(on_ref[...], wo_ref[2 * LIN_WIDTH:, :]))
    x1 = _layer_norm(alpha * x_ref[...] + mix, l1w_ref[...], l1b_ref[...])
    xb = x1.astype(BF16)
    ffn = jnp.zeros_like(x1)
    for c0, c1 in FFN_SPLITS:
        gate = jnp.dot(xb, wg_ref[:, c0:c1], preferred_element_type=F32)
        up = jnp.dot(xb, wu_ref[:, c0:c1], preferred_element_type=F32)
        ffn = ffn + _dot(gate * _sigmoid(gate) * up, wd_ref[c0:c1, :])
    o_ref[...] = _layer_norm(alpha * x1 + ffn, l2w_ref[...], l2b_ref[...])


def _mix_ffn(x2, og, orw, on, wo, l1w, l1b, wg, wu, wd, l2w, l2b, alpha, tm=512):
    t = x2.shape[0]

    def rows(width):
        return pl.BlockSpec((tm, width), lambda i: (i, 0))

    def resident(a):
        return pl.BlockSpec(a.shape, lambda i: (0,) * a.ndim, pipeline_mode=pl.Buffered(1))

    consts = [wo, l1w, l1b, wg, wu, wd, l2w, l2b]
    return pl.pallas_call(
        functools.partial(_mix_ffn_kernel, alpha=alpha),
        grid=(t // tm,),
        in_specs=[rows(D_MODEL), rows(LIN_WIDTH), rows(LIN_WIDTH), rows(NSA_WIDTH)] + [resident(a) for a in consts],
        out_specs=rows(D_MODEL),
        out_shape=jax.ShapeDtypeStruct((t, D_MODEL), F32),
        compiler_params=_params("parallel"),
    )(x2, og, orw, on, *consts)


def _pad_cols(w, width):
    return jnp.pad(w, ((0, 0), (0, width - w.shape[1])))


def _pad_rows(w, height):
    return jnp.pad(w, ((0, height - w.shape[0]), (0, 0)))


def _rwkv_slab_cols(w, w_vres):
    o = 3 * LIN_WIDTH
    parts = [w[:, :o],
             _pad_cols(w[:, o:o + RWKV_DECAY_RANK], LANES),
             _pad_cols(w[:, o + RWKV_DECAY_RANK:o + RWKV_DECAY_RANK + RWKV_ICLR_RANK], LANES),
             _pad_cols(w[:, o + RWKV_DECAY_RANK + RWKV_ICLR_RANK:], 2 * LANES),
             _pad_cols(w_vres, LANES)]
    return jnp.concatenate(parts, axis=1)


def _rope_tables(s):
    half = HEAD_DIM // 2
    inv = ROPE_THETA ** (-jnp.arange(half, dtype=F32) / half)
    ang = jnp.arange(s, dtype=jnp.int32).astype(F32)[:, None] * inv
    cos, sin = jnp.cos(ang), jnp.sin(ang)
    cos2 = jnp.tile(jnp.concatenate([cos, cos], axis=1), (1, LANES // HEAD_DIM))
    sin2 = jnp.tile(jnp.concatenate([-sin, sin], axis=1), (1, LANES // HEAD_DIM))
    return cos2, sin2


def kernel(x, w_in, w_in_vres, gla_w_a2, gla_b_a, gla_ln_w, gla_ln_b, rwkv_mu, rwkv_mu_vres, rwkv_w0, rwkv_w2, rwkv_a0, rwkv_a2, rwkv_v0, rwkv_v2, rwkv_g2, rwkv_k_k, rwkv_k_a, rwkv_r_k, rwkv_ln_w, rwkv_ln_b, nsa_pos_k, nsa_pos_v, nsa_wk1, nsa_wk2, nsa_wv1, nsa_wv2, w_out, ln1_w, ln1_b, ffn_w_gate, ffn_w_up, ffn_w_down, ln2_w, ln2_b):
    bsz, s, d = x.shape
    depth = w_in.shape[0]
    alpha = float((2 * depth) ** 0.25)
    gla_cols = 4 * LIN_WIDTH + GLA_GATE_RANK
    rwkv_cols = 3 * LIN_WIDTH + RWKV_DECAY_RANK + RWKV_ICLR_RANK + RWKV_GATE_RANK
    cos2, sin2 = _rope_tables(s)
    row = lambda a: a.reshape(1, -1)

    x2 = x.reshape(bsz * s, d)
    v_first = None
    for l in range(depth):
        w = w_in[l]
        w_r = w[:, gla_cols:gla_cols + rwkv_cols]
        if l == 0:
            w_vres = jnp.zeros((d, RWKV_VRES_RANK), F32)
            mu_vres = jnp.zeros((1, RWKV_VRES_RANK), F32)
        else:
            w_vres = w_in_vres[l - 1]
            mu_vres = row(rwkv_mu_vres[l - 1])
        wg = _pad_cols(w[:, :gla_cols], GLA_SLAB).astype(BF16)
        wr = _rwkv_slab_cols(w_r, w_vres).astype(BF16)
        wn = _pad_cols(w[:, gla_cols + rwkv_cols:], NSA_SLAB).astype(BF16)
        hg, hr, qr, kvc, kk, vt, gt = _proj(x2, wg, wr, wn, cos2, sin2)
        hg = hg.reshape(bsz, s, GLA_SLAB)
        hr = hr.reshape(bsz, s, RWKV_SLAB)
        qr = qr.reshape(bsz, s, NSA_WIDTH)
        kvc = kvc.reshape(bsz, s, 2 * LANES)
        kk = kk.reshape(bsz, s, 2 * LANES)
        vt = vt.reshape(bsz, s // QBLK, 4 * VT_ROWS, QBLK)
        gt = gt.reshape(bsz, s, LANES)

        o_gla = _gla(hg, _hi_lo(_pad_rows(gla_w_a2[l], LANES)), row(gla_b_a[l]), row(gla_ln_w[l]), row(gla_ln_b[l]))

        mu = _rwkv_slab_cols(row(rwkv_mu[l]), mu_vres)
        small = [mu, row(rwkv_w0[l]), _hi_lo(_pad_rows(rwkv_w2[l], LANES)), row(rwkv_a0[l]),
                 _hi_lo(_pad_rows(rwkv_a2[l], LANES)), _hi_lo(_pad_rows(rwkv_g2[l], 2 * LANES)), row(rwkv_k_k[l]),
                 row(rwkv_k_a[l]), row(rwkv_r_k[l]), row(rwkv_ln_w[l]), row(rwkv_ln_b[l])]
        if l == 0:
            o_rwkv, v_first = _rwkv(hr, None, small)
        else:
            small += [row(rwkv_v0[l - 1]), _hi_lo(_pad_rows(rwkv_v2[l - 1], LANES))]
            o_rwkv = _rwkv(hr, v_first, small)

        kcmp, vcmp = _nsa_cmp(kvc, nsa_pos_k[l], nsa_pos_v[l], nsa_wk1[l], nsa_wk2[l], nsa_wv1[l], nsa_wv2[l])
        o_nsa = _nsa_attn(qr, kk, vt, gt, kcmp, vcmp)

        x2 = _mix_ffn(x2, o_gla.reshape(bsz * s, LIN_WIDTH), o_rwkv.reshape(bsz * s, LIN_WIDTH),
                      o_nsa.reshape(bsz * s, NSA_WIDTH), w_out[l].astype(BF16), row(ln1_w[l]), row(ln1_b[l]),
                      ffn_w_gate[l].astype(BF16), ffn_w_up[l].astype(BF16), ffn_w_down[l].astype(BF16),
                      row(ln2_w[l]), row(ln2_b[l]), alpha)
    return x2.reshape(bsz, s, d)
```

```python
import functools

import numpy as np
import jax
import jax.numpy as jnp
from jax import lax
from jax.experimental import pallas as pl
from jax.experimental.pallas import tpu as pltpu

F32 = jnp.float32
BF16 = jnp.bfloat16

D_MODEL = 1024
HEAD_DIM = 64
N_LIN_HEADS = 4
LIN_WIDTH = N_LIN_HEADS * HEAD_DIM
GLA_GATE_RANK = 16
GLA_TAU = 16.0
RWKV_DECAY_RANK = 64
RWKV_ICLR_RANK = 64
RWKV_VRES_RANK = 32
RWKV_GATE_RANK = 160
RWKV_GN_EPS = 64e-5
NSA_HEADS = 8
NSA_KV_HEADS = 2
NSA_HPG = NSA_HEADS // NSA_KV_HEADS
NSA_WIDTH = NSA_HEADS * HEAD_DIM
NSA_KV_WIDTH = NSA_KV_HEADS * HEAD_DIM
NSA_CMP_BLOCK = 32
NSA_CMP_STRIDE = 16
NSA_CMP_HIDDEN = 256
NSA_SEL_BLOCK = 64
NSA_N_SELECT = 16
NSA_WINDOW = 512
ROPE_THETA = 10000.0
FFN_HIDDEN = 2816
FFN_SPLITS = ((0, 1536), (1536, FFN_HIDDEN))
LN_EPS = 1e-5
MASK_NEG = -1e30

LANES = 128
CHUNK = 64
SUB = 16
RWKV_HB = 2
RW_BW = RWKV_HB * HEAD_DIM
QBLK = 128
SLC_KB = 4
VT_ROWS = HEAD_DIM + 16
LOG2E = 1.4426950408889634
VMEM_LIMIT = 56 * 1024 * 1024

GLA_SLAB = 4 * LIN_WIDTH + LANES
RWKV_SLAB = 3 * LIN_WIDTH + LANES + LANES + 2 * LANES + LANES
NSA_SLAB = NSA_WIDTH + 6 * NSA_KV_WIDTH + LANES
RW_WLR, RW_ALR, RW_GLR, RW_VLR = 768, 896, 1024, 1280


def _dot(a, b, dims=None):
    dims = dims or (((a.ndim - 1,), (0,)), ((), ()))
    return lax.dot_general(a.astype(BF16), b.astype(BF16), dims, preferred_element_type=F32)


def _split_bf16(a):
    hi = a.astype(BF16)
    return hi, (a - hi.astype(F32)).astype(BF16)


def _dot3(a, b_hi, b_lo, dims=None):
    dims = dims or (((a.ndim - 1,), (0,)), ((), ()))
    a_hi, a_lo = _split_bf16(a)
    d = functools.partial(lax.dot_general, dimension_numbers=dims, preferred_element_type=F32)
    return d(a_hi, b_hi) + (d(a_lo, b_hi) + d(a_hi, b_lo))


def _dot_exact(a, b):
    if a.dtype == BF16:
        hi, lo = _split_bf16(b)
        return jnp.dot(a, hi, preferred_element_type=F32) + jnp.dot(a, lo, preferred_element_type=F32)
    hi, lo = _split_bf16(a)
    return jnp.dot(hi, b, preferred_element_type=F32) + jnp.dot(lo, b, preferred_element_type=F32)


def _hi_lo(w):
    return jnp.stack(_split_bf16(w))


_NT = (((1,), (1,)), ((), ()))
_TN = (((0,), (0,)), ((), ()))


def _sigmoid(x):
    return 1.0 / (1.0 + jnp.exp(-x))


def _softplus(x):
    return jnp.maximum(x, 0.0) + jnp.log(1.0 + jnp.exp(-jnp.abs(x)))


def _head_norm(x, w, b, eps):
    mu = jnp.mean(x, axis=-1, keepdims=True)
    xc = x - mu
    var = jnp.mean(xc * xc, axis=-1, keepdims=True)
    return xc * lax.rsqrt(var + eps) * w + b


def _layer_norm(x, w, b):
    mu = jnp.mean(x, axis=-1, keepdims=True)
    xc = x - mu
    var = jnp.mean(xc * xc, axis=-1, keepdims=True)
    return xc * lax.rsqrt(var + LN_EPS) * w + b


def _params(*sem):
    return pltpu.CompilerParams(dimension_semantics=sem, vmem_limit_bytes=VMEM_LIMIT)


def _full(shape):
    nd = len(shape)
    return pl.BlockSpec(shape, lambda *_: (0,) * nd)


def _each(f, *lists):
    return [f(*args) for args in zip(*lists)]


def _run_together(*tracers):
    out = [None] * len(tracers)
    live = dict(enumerate(tracers))
    while live:
        for i in list(live):
            try:
                next(live[i])
            except StopIteration as stop:
                out[i] = stop.value
                del live[i]
    return out


def _proj_kernel(x_ref, wg_ref, wr_ref, wn_ref, cos_ref, sin_ref,
                 og_ref, or_ref, q_ref, kvc_ref, kk_ref, vt_ref, gt_ref, *, n_qblk):
    xb = x_ref[...].astype(BF16)
    og_ref[...] = jnp.dot(xb, wg_ref[...], preferred_element_type=F32)
    or_ref[...] = jnp.dot(xb, wr_ref[...], preferred_element_type=F32)
    hn = jnp.dot(xb, wn_ref[...], preferred_element_type=F32)

    cos = cos_ref[...]
    sin = sin_ref[...]
    half = HEAD_DIM // 2
    first_half = (lax.broadcasted_iota(jnp.int32, cos.shape, 1) & (HEAD_DIM - 1)) < half

    def rope(x):
        swapped = jnp.where(first_half, pltpu.roll(x, LANES - half, 1), pltpu.roll(x, half, 1))
        return x * cos + swapped * sin

    def col(i):
        return hn[:, NSA_WIDTH + LANES * i:NSA_WIDTH + LANES * (i + 1)]

    for i in range(NSA_WIDTH // LANES):
        cs = slice(LANES * i, LANES * (i + 1))
        q_ref[:, cs] = rope(hn[:, cs]) * (HEAD_DIM ** -0.5)
    kvc_ref[:, 0:LANES] = rope(col(0))
    kvc_ref[:, LANES:2 * LANES] = col(1)
    kk_ref[:, 0:LANES] = rope(col(2)).astype(BF16)
    kk_ref[:, LANES:2 * LANES] = rope(col(4)).astype(BF16)
    pad = VT_ROWS - HEAD_DIM
    ones_rows = jnp.where(lax.broadcasted_iota(jnp.int32, (pad, QBLK), 0) == 0, 1.0, 0.0).astype(BF16)
    for i in range(n_qblk):
        rs = slice(QBLK * i, QBLK * (i + 1))
        for branch, c in enumerate((3, 5)):
            v_t = col(c)[rs, :].T.astype(BF16)
            for g in range(NSA_KV_HEADS):
                r0 = (branch * NSA_KV_HEADS + g) * VT_ROWS
                vt_ref[i, r0:r0 + HEAD_DIM, :] = v_t[HEAD_DIM * g:HEAD_DIM * (g + 1), :]
                vt_ref[i, r0 + HEAD_DIM:r0 + VT_ROWS, :] = ones_rows
    gt_ref[...] = _sigmoid(hn[:, NSA_WIDTH + 6 * LANES:NSA_SLAB])


def _proj(x2, wg, wr, wn, cos2, sin2, tm=512):
    t = x2.shape[0]
    s = cos2.shape[0]
    n_qblk = tm // QBLK

    def rows(width):
        return pl.BlockSpec((tm, width), lambda i: (i, 0))

    def resident(a):
        return pl.BlockSpec(a.shape, lambda i: (0,) * a.ndim, pipeline_mode=pl.Buffered(1))

    pos_rows = pl.BlockSpec((tm, LANES), lambda i: (i % (s // tm), 0))
    return pl.pallas_call(
        functools.partial(_proj_kernel, n_qblk=n_qblk),
        grid=(t // tm,),
        in_specs=[rows(D_MODEL), resident(wg), resident(wr), resident(wn), pos_rows, pos_rows],
        out_specs=[rows(GLA_SLAB), rows(RWKV_SLAB), rows(NSA_WIDTH), rows(2 * LANES), rows(2 * LANES),
                   pl.BlockSpec((n_qblk, 4 * VT_ROWS, QBLK), lambda i: (i, 0, 0)), rows(LANES)],
        out_shape=[jax.ShapeDtypeStruct((t, GLA_SLAB), F32),
                   jax.ShapeDtypeStruct((t, RWKV_SLAB), F32),
                   jax.ShapeDtypeStruct((t, NSA_WIDTH), F32),
                   jax.ShapeDtypeStruct((t, 2 * LANES), F32),
                   jax.ShapeDtypeStruct((t, 2 * LANES), BF16),
                   jax.ShapeDtypeStruct((t // QBLK, 4 * VT_ROWS, QBLK), BF16),
                   jax.ShapeDtypeStruct((t, LANES), F32)],
        compiler_params=_params("parallel"),
    )(x2, wg, wr, wn, cos2, sin2)


def _group_rows(x, u):
    return jnp.concatenate(
        [jnp.broadcast_to(x[SUB * i + u:SUB * i + u + 1, :], (SUB, x.shape[1]))
         for i in range(CHUNK // SUB)], axis=0)


def _gla_steps(h_ref, wa2_ref, ba_ref, lnw_ref, lnb_ref, tril_ref, blk_ref, ones_ref, dm_ref, same_ref,
               same_sub_ref, o_ref, st_ref, n_chunks):
    each = _each
    row_in_sub = lax.broadcasted_iota(jnp.int32, (CHUNK, LIN_WIDTH), 0) % SUB
    dm = dm_ref[...]
    heads = range(N_LIN_HEADS)
    n_sub = CHUNK // SUB

    def own_head(m, mask_ref):
        return jnp.concatenate([m] * N_LIN_HEADS, axis=0) * mask_ref[...]

    rows = [pl.ds(j * CHUNK, CHUNK) for j in range(n_chunks)]
    ps = [h_ref[0, rw, :] for rw in rows]
    q = each(lambda p: p[:, 0:256] * (HEAD_DIM ** -0.5), ps)
    k = each(lambda p: p[:, 256:512], ps)
    v = each(lambda p: p[:, 512:768], ps)
    g = each(lambda p: p[:, 768:1024], ps)
    z = each(lambda p: _dot3(p[:, 1024:1152], wa2_ref[0], wa2_ref[1]) + ba_ref[...], ps)
    yield
    log_a = each(lambda x: -_softplus(-x) * (1.0 / GLA_TAU), z)
    yield
    b = each(lambda x: _dot_exact(tril_ref[...], x), log_a)
    yield
    b_end = each(lambda x: _dot_exact(blk_ref[...], x), log_a)
    yield
    qe = each(lambda x, bb: x * jnp.exp(bb), q, b)
    kd = each(lambda x, be, bb: x * jnp.exp(be - bb), k, b_end, b)
    p_end = each(jnp.exp, b_end)
    v_st = each(lambda x: jnp.concatenate([x[:, HEAD_DIM * h:HEAD_DIM * (h + 1)] for h in heads], axis=0), v)
    yield

    at = [jnp.zeros((CHUNK, LIN_WIDTH), F32) for _ in ps]
    for u0 in range(0, SUB, N_LIN_HEADS):
        us = range(u0, u0 + N_LIN_HEADS)
        es = each(lambda bb, qq, kk: jnp.concatenate(
            [jnp.exp(jnp.where(row_in_sub <= u, _group_rows(bb, u) - bb, MASK_NEG)) * (_group_rows(qq, u) * kk)
             for u in us], axis=0), b, q, k)
        yield
        a_rep = each(lambda e: _dot(e, ones_ref[...]), es)
        yield
        for i, u in enumerate(us):
            at = each(lambda acc, ar: acc + jnp.where(dm == u, ar[CHUNK * i:CHUNK * (i + 1), :], 0.0), at, a_rep)
        yield
    o_st = each(lambda x, vs: _dot(own_head(x, same_ref), vs, _TN), at, v_st)
    yield

    upd, q_own = [], []
    for i in range(n_sub):
        rs = slice(SUB * i, SUB * (i + 1))
        upd.append(each(lambda vs, kk: _dot(
            jnp.concatenate([vs[CHUNK * h + SUB * i:CHUNK * h + SUB * (i + 1), :] for h in heads], axis=0),
            own_head(kk[rs], same_sub_ref), _TN), v_st, kd))
        q_own.append(each(lambda x: own_head(x[rs], same_sub_ref), qe))
        yield
    st = st_ref[...]
    for c, rw in enumerate(rows):
        inter = []
        for i in range(n_sub):
            inter.append(_dot(q_own[i][c], st, _NT))
            st = st * p_end[c][SUB * i:SUB * i + 1, :] + upd[i][c]
        o_c = o_st[c] + jnp.concatenate(
            [inter[i][SUB * h:SUB * (h + 1), :] for h in heads for i in range(n_sub)], axis=0)
        o_c = _head_norm(o_c, 1.0, 0.0, LN_EPS)
        o = jnp.concatenate([o_c[CHUNK * h:CHUNK * (h + 1), :] for h in heads], axis=1)
        o_ref[0, rw, :] = (o * lnw_ref[...] + lnb_ref[...]) * (g[c] * _sigmoid(g[c]))
        yield
    st_ref[...] = st


def _rwkv_steps(h_ref, vf_ref, mu_ref, w0_ref, w2_ref, a0_ref, a2_ref, g2_ref, kk_ref, ka_ref, rk_ref,
                lnw_ref, lnb_ref, v0_ref, v2_ref, tril_ref, ones_ref, same_ref, strict_ref, incl_ref, eye_ref,
                o_ref, st_ref, prev_ref, n_chunks, has_vres):
    each = _each
    row = lax.broadcasted_iota(jnp.int32, (CHUNK, RWKV_SLAB), 0)
    blocks = range(N_LIN_HEADS // RWKV_HB)
    in_block = range(RWKV_HB)
    nr = RWKV_HB * CHUNK

    def tile_rows(m):
        return jnp.concatenate([m] * RWKV_HB, axis=0)

    def own_head(m):
        return tile_rows(m) * same_ref[...]

    def stack(m):
        return jnp.concatenate([m[:, HEAD_DIM * h:HEAD_DIM * (h + 1)] for h in in_block], axis=0)

    def unstack(m):
        return jnp.concatenate([m[CHUNK * h:CHUNK * (h + 1), :] for h in in_block], axis=1)

    def per_block(xs):
        return [x[:, RW_BW * bi:RW_BW * (bi + 1)] for x in xs for bi in blocks]

    rows = [pl.ds(j * CHUNK, CHUNK) for j in range(n_chunks)]
    ps = [h_ref[0, rw, :] for rw in rows]
    vfs = [vf_ref[0, rw, :] if has_vres else None for rw in rows]
    prev_rows = [prev_ref[...]] + [p[CHUNK - 1:CHUNK, :] for p in ps[:-1]]

    prev = each(lambda p, pr: jnp.where(row == 0, pr, pltpu.roll(p, 1, 0)), ps, prev_rows)
    xm = each(lambda p, pv: p + (pv - p) * mu_ref[...], ps, prev)
    yield
    r = each(lambda x: x[:, 0:256], xm)
    k = each(lambda x: x[:, 256:512], xm)
    v = each(lambda x: x[:, 512:768], xm)
    v_shift = v
    w_log = each(lambda x: -_softplus(-(w0_ref[...] + _dot3(jnp.tanh(x[:, RW_WLR:RW_ALR]), w2_ref[0], w2_ref[1])))
                 - 0.5, xm)
    yield
    lw = each(lambda w: -jnp.exp(w), w_log)
    a = each(lambda x: _sigmoid(a0_ref[...] + _dot3(x[:, RW_ALR:RW_GLR], a2_ref[0], a2_ref[1])), xm)
    yield
    g = each(lambda x: _dot3(_sigmoid(x[:, RW_GLR:RW_VLR]), g2_ref[0], g2_ref[1]), xm)
    yield
    if has_vres:
        mixv = each(lambda x: _sigmoid(v0_ref[...] + _dot3(x[:, RW_VLR:RWKV_SLAB], v2_ref[0], v2_ref[1])), xm)
        v = each(lambda vv, vf, mx: vv + (vf - vv) * mx, v, vfs, mixv)
        yield
    kkv = each(lambda kk: kk * kk_ref[...], k)
    nrm = each(lambda kv: jnp.sqrt(_dot_exact(kv * kv, ones_ref[...])), kkv)
    yield
    kkn = each(lambda kv, n: kv / jnp.maximum(n, 1e-12), kkv, nrm)
    k2 = each(lambda kk, aa: kk * (1.0 + (aa - 1.0) * ka_ref[...]), k, a)
    bv = each(lambda kn, aa: kn * aa, kkn, a)
    c = each(lambda w: _dot_exact(tril_ref[...], w), lw)
    yield
    c_end = each(lambda cc: cc[CHUNK - 1:CHUNK, :], c)
    bonus = each(lambda rr, kk, vv: _dot_exact(rr * kk * rk_ref[...], ones_ref[...]) * vv, r, k2, v)
    yield
    at = each(lambda kn, cc, w: -kn * jnp.exp(cc - w), kkn, c, lw)
    rt = each(lambda rr, cc: rr * jnp.exp(cc), r, c)
    yield
    bt = each(lambda b_, cc: b_ * jnp.exp(-cc), bv, c)
    kt = each(lambda kk, cc: kk * jnp.exp(-cc), k2, c)
    yield
    b_rest = each(lambda b_, ce, cc: b_ * jnp.exp(ce - cc), bv, c_end, c)
    k_rest = each(lambda kk, ce, cc: kk * jnp.exp(ce - cc), k2, c_end, c)
    p_end = per_block(each(jnp.exp, c_end))
    yield

    ar = each(lambda x, y: jnp.concatenate([own_head(x), own_head(y)], axis=0), per_block(at), per_block(rt))
    yield
    bk = each(lambda x, y: jnp.concatenate([tile_rows(x), tile_rows(y)], axis=0), per_block(bt), per_block(kt))
    yield
    gm = each(lambda x, y: _dot(x, y, _NT), ar, bk)
    yield
    n_ab = each(lambda m: m[0:nr, 0:nr] * strict_ref[...], gm)
    m_ak = each(lambda m: m[0:nr, nr:2 * nr] * strict_ref[...], gm)
    r_bk = each(lambda m: m[nr:2 * nr, :] * jnp.concatenate([incl_ref[...], incl_ref[...]], axis=1), gm)
    yield
    v_st = each(stack, per_block(v))
    mv = each(_dot, m_ak, v_st)
    yield
    inv = each(lambda n: eye_ref[...] + n, n_ab)
    pw = n_ab
    for _ in range(5):
        pw = each(lambda x: _dot(x, x), pw)
        yield
        inv = each(lambda i_, x: i_ + _dot(i_, x), inv, pw)
        yield
    bk_rest = each(lambda x, y: jnp.concatenate([own_head(x), own_head(y)], axis=0),
                   per_block(b_rest), per_block(k_rest))
    yield
    a_bar = each(lambda i_, x: _dot(i_, x[0:nr]), inv, ar)
    yield
    uv0 = each(lambda i_, m, vs: jnp.concatenate([_dot(i_, m), vs], axis=0), inv, mv, v_st)
    yield
    r_bar = each(lambda x, rb, ab: x[nr:2 * nr] + _dot(rb[:, 0:nr], ab), ar, r_bk, a_bar)
    yield
    y0 = each(_dot, r_bk, uv0)
    yield
    g_mat = each(lambda ab, br: _dot(ab, br[0:nr], _TN), a_bar, bk_rest)
    yield
    sc = each(lambda z, br: _dot(z, br, _TN), uv0, bk_rest)
    yield

    n_blk = len(blocks)
    s = per_block([st_ref[...]])
    for j, rw in enumerate(rows):
        it = slice(n_blk * j, n_blk * (j + 1))
        y = each(lambda rb, s_, y_: unstack(_head_norm(_dot(rb, s_, _NT) + y_, 1.0, 0.0, RWKV_GN_EPS)),
                 r_bar[it], s, y0[it])
        s = each(lambda s_, pe, gm_, c_: s_ * pe + _dot(s_, gm_) + c_, s, p_end[it], g_mat[it], sc[it])
        o_ref[0, rw, :] = (jnp.concatenate(y, axis=1) * lnw_ref[...] + lnb_ref[...] + bonus[j]) * g[j]
        if not has_vres:
            vf_ref[0, rw, :] = v_shift[j]
        yield
    prev_ref[...] = ps[-1][CHUNK - 1:CHUNK, :]
    st_ref[...] = jnp.concatenate(s, axis=1)


N_GLA_IN, N_RWKV_IN = 11, 18


def _linmix_kernel(*refs, n_chunks, has_vres):
    refs = list(refs)
    gla_in = refs[:N_GLA_IN]
    n_rw = N_RWKV_IN + (3 if has_vres else 0)
    rwkv_in = refs[N_GLA_IN:N_GLA_IN + n_rw]
    outs = refs[N_GLA_IN + n_rw:]
    if has_vres:
        (h_ref, vf_ref, mu, w0, w2, a0, a2, g2, kk, ka, rk, lnw, lnb, v0, v2, *consts) = rwkv_in
        o_gla_ref, o_rwkv_ref, gla_st, rwkv_st, prev_ref = outs
    else:
        (h_ref, mu, w0, w2, a0, a2, g2, kk, ka, rk, lnw, lnb, *consts) = rwkv_in
        v0 = v2 = None
        o_gla_ref, o_rwkv_ref, vf_ref, gla_st, rwkv_st, prev_ref = outs

    @pl.when(pl.program_id(1) == 0)
    def _():
        gla_st[...] = jnp.zeros_like(gla_st)
        rwkv_st[...] = jnp.zeros_like(rwkv_st)
        prev_ref[...] = jnp.zeros_like(prev_ref)

    _run_together(
        _gla_steps(*gla_in, o_gla_ref, gla_st, n_chunks),
        _rwkv_steps(h_ref, vf_ref, mu, w0, w2, a0, a2, g2, kk, ka, rk, lnw, lnb, v0, v2, *consts,
                    o_rwkv_ref, rwkv_st, prev_ref, n_chunks, has_vres))


def _linmix(hg, hr, v_first, gla_small, rwkv_small, ts=512):
    bsz, s, _ = hg.shape
    has_vres = v_first is not None
    r = np.arange(CHUNK)
    c = np.arange(LIN_WIDTH)
    ones = (c[:, None] // HEAD_DIM == c[None, :] // HEAD_DIM).astype(np.float32)
    tril16 = ((r[:, None] // SUB == r[None, :] // SUB) & (r[None, :] <= r[:, None])).astype(np.float32)
    blk16 = (r[:, None] // SUB == r[None, :] // SUB).astype(np.float32)
    same_sub = (r[:, None] // SUB == c[None, :] // HEAD_DIM).astype(np.float32)
    dm = ((c[None, :] % HEAD_DIM) - SUB * (r[:, None] // SUB)).astype(np.int32)
    gla_consts = [jnp.asarray(tril16, BF16), jnp.asarray(blk16, BF16), jnp.asarray(ones, BF16), jnp.asarray(dm),
                  jnp.asarray(ones), jnp.asarray(same_sub)]
    tril = (r[None, :] <= r[:, None]).astype(np.float32)
    i = np.arange(RW_BW)
    same = (i[:, None] // HEAD_DIM == i[None, :] // HEAD_DIM).astype(np.float32)
    t_in, s_in = i[:, None] % CHUNK, i[None, :] % CHUNK
    rwkv_consts = [jnp.asarray(tril, BF16), jnp.asarray(ones, BF16), jnp.asarray(same),
                   jnp.asarray(same * (s_in < t_in)), jnp.asarray(same * (s_in <= t_in)),
                   jnp.asarray(np.eye(RW_BW, dtype=np.float32))]

    def seq(width):
        return pl.BlockSpec((1, ts, width), lambda b, j: (b, j, 0))

    gla_args = [hg] + list(gla_small) + gla_consts
    rwkv_args = [hr] + ([v_first] if has_vres else []) + list(rwkv_small) + rwkv_consts
    assert len(gla_args) == N_GLA_IN and len(rwkv_args) == N_RWKV_IN + (3 if has_vres else 0)
    in_specs = ([seq(GLA_SLAB)] + [_full(a.shape) for a in gla_args[1:]]
                + [seq(RWKV_SLAB)] + ([seq(LIN_WIDTH)] if has_vres else [])
                + [_full(a.shape) for a in rwkv_args[(2 if has_vres else 1):]])
    n_out = 2 if has_vres else 3
    out_sd = jax.ShapeDtypeStruct((bsz, s, LIN_WIDTH), F32)
    return pl.pallas_call(
        functools.partial(_linmix_kernel, n_chunks=ts // CHUNK, has_vres=has_vres),
        grid=(bsz, s // ts),
        in_specs=in_specs,
        out_specs=[seq(LIN_WIDTH)] * n_out,
        out_shape=[out_sd] * n_out,
        scratch_shapes=[pltpu.VMEM((HEAD_DIM, LIN_WIDTH), F32),
                        pltpu.VMEM((HEAD_DIM, LIN_WIDTH), F32),
                        pltpu.VMEM((1, RWKV_SLAB), F32)],
        compiler_params=_params("parallel", "arbitrary"),
    )(*gla_args, *rwkv_args)


def _nsa_cmp_kernel(kc_ref, vc_ref, pk_ref, pv_ref, wk1h_ref, wk1l_ref, wk2h_ref, wk2l_ref,
                    wv1h_ref, wv1l_ref, wv2h_ref, wv2l_ref, ko_ref, vo_ref, *, n_blk):
    half = NSA_CMP_BLOCK // 2

    def compress(t_ref, pos_ref, w1h_ref, w1l_ref, w2h_ref, w2l_ref, o_ref):
        for g in range(NSA_KV_HEADS):
            gs = slice(HEAD_DIM * g, HEAD_DIM * (g + 1))
            top = jnp.zeros((n_blk, NSA_CMP_HIDDEN), F32)
            bot = jnp.zeros((n_blk, NSA_CMP_HIDDEN), F32)
            for l in range(half):
                rows = t_ref[0, pl.ds(l, n_blk, stride=NSA_CMP_STRIDE), :][:, gs]
                top = top + _dot3(rows + pos_ref[l:l + 1, :], w1h_ref[l], w1l_ref[l])
                bot = bot + _dot3(rows + pos_ref[half + l:half + l + 1, :], w1h_ref[half + l], w1l_ref[half + l])
            hid = top + pltpu.roll(bot, n_blk - 1, 0)
            act = 0.5 * hid * (1.0 + jnp.tanh(0.7978845608028654 * (hid + 0.044715 * hid * hid * hid)))
            o_ref[0, :, gs] = _dot3(act, w2h_ref[...], w2l_ref[...])

    compress(kc_ref, pk_ref, wk1h_ref, wk1l_ref, wk2h_ref, wk2l_ref, ko_ref)
    compress(vc_ref, pv_ref, wv1h_ref, wv1l_ref, wv2h_ref, wv2l_ref, vo_ref)


def _nsa_cmp(kv, pos_k, pos_v, wk1, wk2, wv1, wv2):
    bsz, s, _ = kv.shape
    n_blk = s // NSA_CMP_STRIDE
    wk1 = wk1.reshape(NSA_CMP_BLOCK, HEAD_DIM, NSA_CMP_HIDDEN)
    wv1 = wv1.reshape(NSA_CMP_BLOCK, HEAD_DIM, NSA_CMP_HIDDEN)
    small = [pos_k, pos_v, *_split_bf16(wk1), *_split_bf16(wk2), *_split_bf16(wv1), *_split_bf16(wv2)]
    out_spec = pl.BlockSpec((1, n_blk, LANES), lambda b: (b, 0, 0))
    out_sd = jax.ShapeDtypeStruct((bsz, n_blk, LANES), F32)
    return pl.pallas_call(
        functools.partial(_nsa_cmp_kernel, n_blk=n_blk),
        grid=(bsz,),
        in_specs=[pl.BlockSpec((1, s, LANES), lambda b: (b, 0, 0)),
                  pl.BlockSpec((1, s, LANES), lambda b: (b, 0, 1))] + [_full(a.shape) for a in small],
        out_specs=[out_spec, out_spec],
        out_shape=[out_sd, out_sd],
        compiler_params=_params("parallel"),
    )(kv, kv, *small)


def _nsa_attn_kernel(q_ref, gt_ref, kc_ref, vc_ref, ks_ref, kw_ref, vt_ref, ovt_ref, ext_ref,
                     o_ref, acc_ref, s_ref, cm_ref, m_ref, *, n_cmp):
    qi = pl.program_id(1)
    cols = NSA_HPG * QBLK
    n_sel_blk = ovt_ref.shape[0]
    key_off = lax.broadcasted_iota(jnp.int32, (QBLK, cols), 0)
    t_pos = qi * QBLK + (lax.broadcasted_iota(jnp.int32, (QBLK, cols), 1) & (QBLK - 1))
    gt_t = gt_ref[0].T
    zeros64 = jnp.zeros((QBLK, HEAD_DIM), F32)
    m0 = jnp.full((1, cols), MASK_NEG, F32)

    groups = list(range(NSA_KV_HEADS))

    each, run_together = _each, _run_together

    def stacked_q(g):
        pieces = []
        for hp in range(NSA_HPG):
            qh = q_ref[0, :, (g * NSA_HPG + hp) * HEAD_DIM:(g * NSA_HPG + hp + 1) * HEAD_DIM]
            pieces.append(jnp.concatenate([qh, zeros64] if g == 0 else [zeros64, qh], axis=1))
        return jnp.concatenate(pieces, axis=0)

    qs = each(stacked_q, groups)
    qs_b = each(lambda q: (q * LOG2E).astype(BF16), qs)

    def select_steps():
        valid_c = (key_off * NSA_CMP_STRIDE + (NSA_CMP_BLOCK - 1) <= t_pos) & (key_off < n_cmp)
        s_c = each(lambda q: jnp.where(valid_c, _dot3(kc_ref[0], *_split_bf16(q), _NT), MASK_NEG), qs)
        yield
        e_c = each(lambda x: jnp.exp(x - jnp.max(x, axis=0, keepdims=True)), s_c)
        yield
        p_c = each(lambda e: jnp.where(valid_c, e / jnp.sum(e, axis=0, keepdims=True), 0.0), e_c)
        yield
        o_cmp = each(lambda p, g: _dot(vc_ref[0], p, _TN)[HEAD_DIM * g:HEAD_DIM * (g + 1), :], p_c, groups)
        yield
        p_sum = each(lambda p: p[:, 0:QBLK] + p[:, QBLK:2 * QBLK] + p[:, 2 * QBLK:3 * QBLK] + p[:, 3 * QBLK:], p_c)
        imp = each(lambda p: _dot_exact(ovt_ref[...], p), p_sum)
        yield
        j = lax.broadcasted_iota(jnp.int32, (n_sel_blk, QBLK), 0)
        tq = qi * QBLK + lax.broadcasted_iota(jnp.int32, (n_sel_blk, QBLK), 1)
        cur = tq // NSA_SEL_BLOCK
        forced = (j == 0) | (j == cur) | (j == cur - 1)
        score = each(lambda x: jnp.where(forced, 1e9, jnp.where(j > cur, -1e9, x)), imp)
        rank = [jnp.zeros((n_sel_blk, QBLK), F32) for _ in groups]
        for i in range(n_sel_blk):
            rank = each(lambda rk, sc: rk + jnp.where(
                (sc[i:i + 1, :] > sc) | ((sc[i:i + 1, :] == sc) & (j > i)), 1.0, 0.0), rank, score)
            if i % 4 == 3:
                yield
        return o_cmp, each(lambda rk: jnp.where(rk < NSA_N_SELECT, 1.0, 0.0), rank)

    def key_scores(k_ref, g, kb0, n_kb, bias, mask):
        k0 = pl.multiple_of(kb0 * QBLK, QBLK)
        s = lax.dot_general(k_ref[0, pl.ds(k0, n_kb * QBLK), :], qs_b[g], _NT, preferred_element_type=F32)
        if bias is not None:
            s = s + bias
        if mask is not None:
            key_pos = k0 + lax.broadcasted_iota(jnp.int32, (n_kb * QBLK, cols), 0)
            t_all = jnp.concatenate([t_pos] * n_kb, axis=0)
            valid = key_pos <= t_all
            if mask == "window":
                valid = valid & (key_pos > t_all - NSA_WINDOW)
            s = jnp.where(valid, s, MASK_NEG)
        return s

    def softmax_steps(slabs, kb0, n_kb, block_scores, col_max, m_old):
        m_new = each(jnp.maximum, m_old, col_max)
        pv = [jnp.zeros((VT_ROWS, cols), F32) for _ in slabs]
        for i in range(n_kb):
            pv = [acc + jnp.dot(vt_ref[0, kb0 + i, slab * VT_ROWS:(slab + 1) * VT_ROWS, :],
                                jnp.exp2(rd(i) - mn).astype(BF16), preferred_element_type=F32)
                  for acc, rd, mn, slab in zip(pv, block_scores, m_new, slabs)]
            yield
        for slab, m, mn, x in zip(slabs, m_old, m_new, pv):
            acc_ref[slab] = acc_ref[slab] * jnp.exp2(m - mn) + x
        return m_new

    def window_steps():
        n_win = NSA_WINDOW // QBLK + 1
        kb_win = jnp.maximum(qi - (n_win - 1), 0)
        s = []
        for g in groups:
            s.append(key_scores(kw_ref, g, kb_win, n_win, None, "window"))
            yield
        col_max = each(lambda x: jnp.max(x, axis=0, keepdims=True), s)
        yield
        blocks = [lambda i, x=x: x[QBLK * i:QBLK * (i + 1), :] for x in s]
        yield from softmax_steps([NSA_KV_HEADS + g for g in groups], kb_win, n_win, blocks, col_max, [m0] * len(s))

    def normalised(slab):
        return acc_ref[slab, 0:HEAD_DIM, :] / acc_ref[slab, HEAD_DIM:HEAD_DIM + 1, :]

    acc_ref[...] = jnp.zeros_like(acc_ref)
    for g in groups:
        m_ref[g] = m0

    (o_cmp, sel), _ = run_together(select_steps(), window_steps())

    def score_steps(i, mask):
        chosen = each(lambda sl: _dot(ext_ref[i], sl), sel)
        bias = each(lambda ch: jnp.concatenate([jnp.where(ch > 0.5, 0.0, MASK_NEG)] * NSA_HPG, axis=1), chosen)
        s = []
        for g in groups:
            s.append(key_scores(ks_ref, g, i * SLC_KB, SLC_KB, bias[g], mask))
            yield
        return s, each(lambda x: jnp.max(x, axis=0, keepdims=True), s)

    def park(scores_and_max):
        for g, (x, cm) in enumerate(zip(*scores_and_max)):
            s_ref[g] = x
            cm_ref[g] = cm

    def consume_steps(i):
        blocks = [lambda b, g=g: s_ref[g, QBLK * b:QBLK * (b + 1), :] for g in groups]
        m_new = yield from softmax_steps(groups, i * SLC_KB, SLC_KB, blocks, [cm_ref[g] for g in groups],
                                         [m_ref[g] for g in groups])
        for g in groups:
            m_ref[g] = m_new[g]

    n_full = qi // SLC_KB

    @pl.when(n_full >= 1)
    def _():
        park(run_together(score_steps(0, None))[0])

    def pipelined(i, carry):
        nxt, _ = run_together(score_steps(i + 1, None), consume_steps(i))
        park(nxt)
        return carry

    lax.fori_loop(0, n_full - 1, pipelined, 0)

    @pl.when(n_full >= 1)
    def _():
        nxt, _ = run_together(score_steps(n_full, "causal"), consume_steps(n_full - 1))
        park(nxt)

    @pl.when(n_full == 0)
    def _():
        park(run_together(score_steps(0, "causal"))[0])

    run_together(consume_steps(n_full))

    o_groups = []
    for g in groups:
        def gate(branch):
            return jnp.concatenate(
                [gt_t[(g * NSA_HPG + hp) * 3 + branch:(g * NSA_HPG + hp) * 3 + branch + 1, :]
                 for hp in range(NSA_HPG)], axis=1)

        o_groups.append(gate(0) * o_cmp[g] + gate(1) * normalised(g) + gate(2) * normalised(NSA_KV_HEADS + g))

    o_all = jnp.concatenate(o_groups, axis=0)
    for hp in range(NSA_HPG):
        tile = o_all[:, QBLK * hp:QBLK * (hp + 1)].T
        for g in range(NSA_KV_HEADS):
            c0 = (g * NSA_HPG + hp) * HEAD_DIM
            o_ref[0, :, c0:c0 + HEAD_DIM] = tile[:, HEAD_DIM * g:HEAD_DIM * (g + 1)]


def _nsa_attn(qr, kk, vt, gt, kcmp, vcmp):
    bsz, s, _ = qr.shape
    n_blk = s // NSA_CMP_STRIDE
    n_cmp = n_blk - NSA_CMP_BLOCK // NSA_CMP_STRIDE + 1
    n_sel = s // NSA_SEL_BLOCK
    assert n_blk == QBLK, "compressed-score tiles are laid out one stride block per sublane row"
    c0 = np.arange(n_blk) * NSA_CMP_STRIDE
    s0 = np.arange(n_sel) * NSA_SEL_BLOCK
    lo = np.maximum(c0[:, None], s0[None, :])
    hi = np.minimum(c0[:, None] + NSA_CMP_BLOCK, s0[None, :] + NSA_SEL_BLOCK)
    overlap = (np.maximum(hi - lo, 0) / NSA_CMP_STRIDE).astype(np.float32)
    overlap[n_cmp:] = 0.0
    key = np.arange(s)
    expand = (key[:, None] // NSA_SEL_BLOCK == np.arange(n_sel)[None, :]).astype(np.float32)
    expand = expand.reshape(s // (SLC_KB * QBLK), SLC_KB * QBLK, n_sel)
    consts = [jnp.asarray(overlap.T, BF16), jnp.asarray(expand, BF16)]

    cmp_spec = pl.BlockSpec((1, n_blk, LANES), lambda b, i: (b, 0, 0))
    return pl.pallas_call(
        functools.partial(_nsa_attn_kernel, n_cmp=n_cmp),
        grid=(bsz, s // QBLK),
        in_specs=[pl.BlockSpec((1, QBLK, NSA_WIDTH), lambda b, i: (b, i, 0)),
                  pl.BlockSpec((1, QBLK, LANES), lambda b, i: (b, i, 0)),
                  cmp_spec, cmp_spec,
                  pl.BlockSpec((1, s, LANES), lambda b, i: (b, 0, 0)),
                  pl.BlockSpec((1, s, LANES), lambda b, i: (b, 0, 1)),
                  pl.BlockSpec((1, s // QBLK, 4 * VT_ROWS, QBLK), lambda b, i: (b, 0, 0, 0))]
        + [_full(a.shape) for a in consts],
        out_specs=pl.BlockSpec((1, QBLK, NSA_WIDTH), lambda b, i: (b, i, 0)),
        out_shape=jax.ShapeDtypeStruct((bsz, s, NSA_WIDTH), F32),
        scratch_shapes=[pltpu.VMEM((2 * NSA_KV_HEADS, VT_ROWS, NSA_HPG * QBLK), F32),
                        pltpu.VMEM((NSA_KV_HEADS, SLC_KB * QBLK, NSA_HPG * QBLK), F32),
                        pltpu.VMEM((NSA_KV_HEADS, 1, NSA_HPG * QBLK), F32),
                        pltpu.VMEM((NSA_KV_HEADS, 1, NSA_HPG * QBLK), F32)],
        compiler_params=_params("parallel", "arbitrary"),
    )(qr, gt, kcmp, vcmp, kk, kk, vt, *consts)


def _mix_ffn_kernel(x_ref, og_ref, or_ref, on_ref, wo_ref, l1w_ref, l1b_ref, wg_ref, wu_ref, wd_ref,
                    l2w_ref, l2b_ref, o_ref, *, alpha):
    mix = (_dot(og_ref[...], wo_ref[0:LIN_WIDTH, :])
           + _dot(or_ref[...], wo_ref[LIN_WIDTH:2 * LIN_WIDTH, :])
           + _dot(on_ref[...], wo_ref[2 * LIN_WIDTH:, :]))
    x1 = _layer_norm(alpha * x_ref[...] + mix, l1w_ref[...], l1b_ref[...])
    xb = x1.astype(BF16)
    ffn = jnp.zeros_like(x1)
    for c0, c1 in FFN_SPLITS:
        gate = jnp.dot(xb, wg_ref[:, c0:c1], preferred_element_type=F32)
        up = jnp.dot(xb, wu_ref[:, c0:c1], preferred_element_type=F32)
        ffn = ffn + _dot(gate * _sigmoid(gate) * up, wd_ref[c0:c1, :])
    o_ref[...] = _layer_norm(alpha * x1 + ffn, l2w_ref[...], l2b_ref[...])


def _mix_ffn(x2, og, orw, on, wo, l1w, l1b, wg, wu, wd, l2w, l2b, alpha, tm=512):
    t = x2.shape[0]

    def rows(width):
        return pl.BlockSpec((tm, width), lambda i: (i, 0))

    def resident(a):
        return pl.BlockSpec(a.shape, lambda i: (0,) * a.ndim, pipeline_mode=pl.Buffered(1))

    consts = [wo, l1w, l1b, wg, wu, wd, l2w, l2b]
    return pl.pallas_call(
        functools.partial(_mix_ffn_kernel, alpha=alpha),
        grid=(t // tm,),
        in_specs=[rows(D_MODEL), rows(LIN_WIDTH), rows(LIN_WIDTH), rows(NSA_WIDTH)] + [resident(a) for a in consts],
        out_specs=rows(D_MODEL),
        out_shape=jax.ShapeDtypeStruct((t, D_MODEL), F32),
        compiler_params=_params("parallel"),
    )(x2, og, orw, on, *consts)


def _pad_cols(w, width):
    return jnp.pad(w, ((0, 0), (0, width - w.shape[1])))


def _pad_rows(w, height):
    return jnp.pad(w, ((0, height - w.shape[0]), (0, 0)))


def _rwkv_slab_cols(w, w_vres):
    o = 3 * LIN_WIDTH
    parts = [w[:, :o],
             _pad_cols(w[:, o:o + RWKV_DECAY_RANK], LANES),
             _pad_cols(w[:, o + RWKV_DECAY_RANK:o + RWKV_DECAY_RANK + RWKV_ICLR_RANK], LANES),
             _pad_cols(w[:, o + RWKV_DECAY_RANK + RWKV_ICLR_RANK:], 2 * LANES),
             _pad_cols(w_vres, LANES)]
    return jnp.concatenate(parts, axis=1)


def _rope_tables(s):
    half = HEAD_DIM // 2
    inv = ROPE_THETA ** (-jnp.arange(half, dtype=F32) / half)
    ang = jnp.arange(s, dtype=jnp.int32).astype(F32)[:, None] * inv
    cos, sin = jnp.cos(ang), jnp.sin(ang)
    cos2 = jnp.tile(jnp.concatenate([cos, cos], axis=1), (1, LANES // HEAD_DIM))
    sin2 = jnp.tile(jnp.concatenate([-sin, sin], axis=1), (1, LANES // HEAD_DIM))
    return cos2, sin2


def kernel(x, w_in, w_in_vres, gla_w_a2, gla_b_a, gla_ln_w, gla_ln_b, rwkv_mu, rwkv_mu_vres, rwkv_w0, rwkv_w2, rwkv_a0, rwkv_a2, rwkv_v0, rwkv_v2, rwkv_g2, rwkv_k_k, rwkv_k_a, rwkv_r_k, rwkv_ln_w, rwkv_ln_b, nsa_pos_k, nsa_pos_v, nsa_wk1, nsa_wk2, nsa_wv1, nsa_wv2, w_out, ln1_w, ln1_b, ffn_w_gate, ffn_w_up, ffn_w_down, ln2_w, ln2_b):
    bsz, s, d = x.shape
    depth = w_in.shape[0]
    alpha = float((2 * depth) ** 0.25)
    gla_cols = 4 * LIN_WIDTH + GLA_GATE_RANK
    rwkv_cols = 3 * LIN_WIDTH + RWKV_DECAY_RANK + RWKV_ICLR_RANK + RWKV_GATE_RANK
    cos2, sin2 = _rope_tables(s)
    row = lambda a: a.reshape(1, -1)

    x2 = x.reshape(bsz * s, d)
    v_first = None
    for l in range(depth):
        w = w_in[l]
        w_r = w[:, gla_cols:gla_cols + rwkv_cols]
        if l == 0:
            w_vres = jnp.zeros((d, RWKV_VRES_RANK), F32)
            mu_vres = jnp.zeros((1, RWKV_VRES_RANK), F32)
        else:
            w_vres = w_in_vres[l - 1]
            mu_vres = row(rwkv_mu_vres[l - 1])
        wg = _pad_cols(w[:, :gla_cols], GLA_SLAB).astype(BF16)
        wr = _rwkv_slab_cols(w_r, w_vres).astype(BF16)
        wn = _pad_cols(w[:, gla_cols + rwkv_cols:], NSA_SLAB).astype(BF16)
        hg, hr, qr, kvc, kk, vt, gt = _proj(x2, wg, wr, wn, cos2, sin2)
        hg = hg.reshape(bsz, s, GLA_SLAB)
        hr = hr.reshape(bsz, s, RWKV_SLAB)
        qr = qr.reshape(bsz, s, NSA_WIDTH)
        kvc = kvc.reshape(bsz, s, 2 * LANES)
        kk = kk.reshape(bsz, s, 2 * LANES)
        vt = vt.reshape(bsz, s // QBLK, 4 * VT_ROWS, QBLK)
        gt = gt.reshape(bsz, s, LANES)

        gla_small = [_hi_lo(_pad_rows(gla_w_a2[l], LANES)), row(gla_b_a[l]), row(gla_ln_w[l]), row(gla_ln_b[l])]
        mu = _rwkv_slab_cols(row(rwkv_mu[l]), mu_vres)
        rwkv_small = [mu, row(rwkv_w0[l]), _hi_lo(_pad_rows(rwkv_w2[l], LANES)), row(rwkv_a0[l]),
                      _hi_lo(_pad_rows(rwkv_a2[l], LANES)), _hi_lo(_pad_rows(rwkv_g2[l], 2 * LANES)),
                      row(rwkv_k_k[l]), row(rwkv_k_a[l]), row(rwkv_r_k[l]), row(rwkv_ln_w[l]), row(rwkv_ln_b[l])]
        if l == 0:
            o_gla, o_rwkv, v_first = _linmix(hg, hr, None, gla_small, rwkv_small)
        else:
            rwkv_small += [row(rwkv_v0[l - 1]), _hi_lo(_pad_rows(rwkv_v2[l - 1], LANES))]
            o_gla, o_rwkv = _linmix(hg, hr, v_first, gla_small, rwkv_small)

        kcmp, vcmp = _nsa_cmp(kvc, nsa_pos_k[l], nsa_pos_v[l], nsa_wk1[l], nsa_wk2[l], nsa_wv1[l], nsa_wv2[l])
        o_nsa = _nsa_attn(qr, kk, vt, gt, kcmp, vcmp)

        x2 = _mix_ffn(x2, o_gla.reshape(bsz * s, LIN_WIDTH), o_rwkv.reshape(bsz * s, LIN_WIDTH),
                      o_nsa.reshape(bsz * s, NSA_WIDTH), w_out[l].astype(BF16), row(ln1_w[l]), row(ln1_b[l]),
                      ffn_w_gate[l].astype(BF16), ffn_w_up[l].astype(BF16), ffn_w_down[l].astype(BF16),
                      row(ln2_w[l]), row(ln2_b[l]), alpha)
    return x2.reshape(bsz, s, d)
```

```python
import functools

import numpy as np
import jax
import jax.numpy as jnp
from jax import lax
from jax.experimental import pallas as pl
from jax.experimental.pallas import tpu as pltpu

F32 = jnp.float32
BF16 = jnp.bfloat16

D_MODEL = 1024
HEAD_DIM = 64
N_LIN_HEADS = 4
LIN_WIDTH = N_LIN_HEADS * HEAD_DIM
GLA_GATE_RANK = 16
GLA_TAU = 16.0
RWKV_DECAY_RANK = 64
RWKV_ICLR_RANK = 64
RWKV_VRES_RANK = 32
RWKV_GATE_RANK = 160
RWKV_GN_EPS = 64e-5
NSA_HEADS = 8
NSA_KV_HEADS = 2
NSA_HPG = NSA_HEADS // NSA_KV_HEADS
NSA_WIDTH = NSA_HEADS * HEAD_DIM
NSA_KV_WIDTH = NSA_KV_HEADS * HEAD_DIM
NSA_CMP_BLOCK = 32
NSA_CMP_STRIDE = 16
NSA_CMP_HIDDEN = 256
NSA_SEL_BLOCK = 64
NSA_N_SELECT = 16
NSA_WINDOW = 512
ROPE_THETA = 10000.0
FFN_HIDDEN = 2816
FFN_SPLITS = ((0, 1536), (1536, FFN_HIDDEN))
LN_EPS = 1e-5
MASK_NEG = -1e30

LANES = 128
CHUNK = 64
SUB = 16
RWKV_HB = 2
RW_BW = RWKV_HB * HEAD_DIM
QBLK = 128
SLC_KB = 4
VT_ROWS = HEAD_DIM + 16
LOG2E = 1.4426950408889634
VMEM_LIMIT = 56 * 1024 * 1024

GLA_SLAB = 4 * LIN_WIDTH + LANES
RWKV_SLAB = 3 * LIN_WIDTH + LANES + LANES + 2 * LANES + LANES
NSA_SLAB = NSA_WIDTH + 6 * NSA_KV_WIDTH + LANES
RW_WLR, RW_ALR, RW_GLR, RW_VLR = 768, 896, 1024, 1280


def _dot(a, b, dims=None):
    dims = dims or (((a.ndim - 1,), (0,)), ((), ()))
    return lax.dot_general(a.astype(BF16), b.astype(BF16), dims, preferred_element_type=F32)


def _split_bf16(a):
    hi = a.astype(BF16)
    return hi, (a - hi.astype(F32)).astype(BF16)


def _dot3(a, b_hi, b_lo, dims=None):
    dims = dims or (((a.ndim - 1,), (0,)), ((), ()))
    a_hi, a_lo = _split_bf16(a)
    d = functools.partial(lax.dot_general, dimension_numbers=dims, preferred_element_type=F32)
    return d(a_hi, b_hi) + (d(a_lo, b_hi) + d(a_hi, b_lo))


def _dot_exact(a, b):
    if a.dtype == BF16:
        hi, lo = _split_bf16(b)
        return jnp.dot(a, hi, preferred_element_type=F32) + jnp.dot(a, lo, preferred_element_type=F32)
    hi, lo = _split_bf16(a)
    return jnp.dot(hi, b, preferred_element_type=F32) + jnp.dot(lo, b, preferred_element_type=F32)


def _hi_lo(w):
    return jnp.stack(_split_bf16(w))


_NT = (((1,), (1,)), ((), ()))
_TN = (((0,), (0,)), ((), ()))


def _sigmoid(x):
    return 1.0 / (1.0 + jnp.exp(-x))


def _softplus(x):
    return jnp.maximum(x, 0.0) + jnp.log(1.0 + jnp.exp(-jnp.abs(x)))


def _head_norm(x, w, b, eps):
    mu = jnp.mean(x, axis=-1, keepdims=True)
    xc = x - mu
    var = jnp.mean(xc * xc, axis=-1, keepdims=True)
    return xc * lax.rsqrt(var + eps) * w + b


def _layer_norm(x, w, b):
    mu = jnp.mean(x, axis=-1, keepdims=True)
    xc = x - mu
    var = jnp.mean(xc * xc, axis=-1, keepdims=True)
    return xc * lax.rsqrt(var + LN_EPS) * w + b


def _params(*sem):
    return pltpu.CompilerParams(dimension_semantics=sem, vmem_limit_bytes=VMEM_LIMIT)


def _full(shape):
    nd = len(shape)
    return pl.BlockSpec(shape, lambda *_: (0,) * nd)


def _each(f, *lists):
    return [f(*args) for args in zip(*lists)]


def _run_together(*tracers):
    out = [None] * len(tracers)
    live = dict(enumerate(tracers))
    while live:
        for i in list(live):
            try:
                next(live[i])
            except StopIteration as stop:
                out[i] = stop.value
                del live[i]
    return out


def _proj_kernel(x_ref, wg_ref, wr_ref, wn_ref, cos_ref, sin_ref,
                 og_ref, or_ref, q_ref, kvc_ref, kk_ref, vt_ref, gt_ref, *, n_qblk, seq_len):
    xb = x_ref[...].astype(BF16)
    og_ref[...] = jnp.dot(xb, wg_ref[...], preferred_element_type=F32)
    or_ref[...] = jnp.dot(xb, wr_ref[...], preferred_element_type=F32)
    hn = jnp.dot(xb, wn_ref[...], preferred_element_type=F32)

    cos = cos_ref[...]
    sin = sin_ref[...]
    half = HEAD_DIM // 2
    first_half = (lax.broadcasted_iota(jnp.int32, cos.shape, 1) & (HEAD_DIM - 1)) < half

    def rope(x):
        swapped = jnp.where(first_half, pltpu.roll(x, LANES - half, 1), pltpu.roll(x, half, 1))
        return x * cos + swapped * sin

    def col(i):
        return hn[:, NSA_WIDTH + LANES * i:NSA_WIDTH + LANES * (i + 1)]

    for i in range(NSA_WIDTH // LANES):
        cs = slice(LANES * i, LANES * (i + 1))
        q_ref[:, cs] = rope(hn[:, cs]) * (HEAD_DIM ** -0.5)
    kvc_ref[:, 0:LANES] = rope(col(0))
    kvc_ref[:, LANES:2 * LANES] = col(1)
    kk_ref[:, 0:LANES] = rope(col(2)).astype(BF16)
    pos = (pl.program_id(0) * x_ref.shape[0]) % seq_len + lax.broadcasted_iota(jnp.int32, cos.shape, 0)
    kk_ref[:, LANES:2 * LANES] = jnp.where(
        pos // NSA_SEL_BLOCK == lax.broadcasted_iota(jnp.int32, cos.shape, 1), 1.0, 0.0).astype(BF16)
    kk_ref[:, 2 * LANES:3 * LANES] = rope(col(4)).astype(BF16)
    pad = VT_ROWS - HEAD_DIM
    ones_rows = jnp.where(lax.broadcasted_iota(jnp.int32, (pad, QBLK), 0) == 0, 1.0, 0.0).astype(BF16)
    for i in range(n_qblk):
        rs = slice(QBLK * i, QBLK * (i + 1))
        for branch, c in enumerate((3, 5)):
            v_t = col(c)[rs, :].T.astype(BF16)
            for g in range(NSA_KV_HEADS):
                r0 = (branch * NSA_KV_HEADS + g) * VT_ROWS
                vt_ref[i, r0:r0 + HEAD_DIM, :] = v_t[HEAD_DIM * g:HEAD_DIM * (g + 1), :]
                vt_ref[i, r0 + HEAD_DIM:r0 + VT_ROWS, :] = ones_rows
    gt_ref[...] = _sigmoid(hn[:, NSA_WIDTH + 6 * LANES:NSA_SLAB])


def _proj(x2, wg, wr, wn, cos2, sin2, tm=512):
    t = x2.shape[0]
    s = cos2.shape[0]
    n_qblk = tm // QBLK

    def rows(width):
        return pl.BlockSpec((tm, width), lambda i: (i, 0))

    def resident(a):
        return pl.BlockSpec(a.shape, lambda i: (0,) * a.ndim, pipeline_mode=pl.Buffered(1))

    pos_rows = pl.BlockSpec((tm, LANES), lambda i: (i % (s // tm), 0))
    return pl.pallas_call(
        functools.partial(_proj_kernel, n_qblk=n_qblk, seq_len=s),
        grid=(t // tm,),
        in_specs=[rows(D_MODEL), resident(wg), resident(wr), resident(wn), pos_rows, pos_rows],
        out_specs=[rows(GLA_SLAB), rows(RWKV_SLAB), rows(NSA_WIDTH), rows(2 * LANES), rows(3 * LANES),
                   pl.BlockSpec((n_qblk, 4 * VT_ROWS, QBLK), lambda i: (i, 0, 0)), rows(LANES)],
        out_shape=[jax.ShapeDtypeStruct((t, GLA_SLAB), F32),
                   jax.ShapeDtypeStruct((t, RWKV_SLAB), F32),
                   jax.ShapeDtypeStruct((t, NSA_WIDTH), F32),
                   jax.ShapeDtypeStruct((t, 2 * LANES), F32),
                   jax.ShapeDtypeStruct((t, 3 * LANES), BF16),
                   jax.ShapeDtypeStruct((t // QBLK, 4 * VT_ROWS, QBLK), BF16),
                   jax.ShapeDtypeStruct((t, LANES), F32)],
        compiler_params=_params("parallel"),
    )(x2, wg, wr, wn, cos2, sin2)


def _group_rows(x, u):
    return jnp.concatenate(
        [jnp.broadcast_to(x[SUB * i + u:SUB * i + u + 1, :], (SUB, x.shape[1]))
         for i in range(CHUNK // SUB)], axis=0)


def _gla_steps(h_ref, wa2_ref, ba_ref, lnw_ref, lnb_ref, tril_ref, blk_ref, ones_ref, dm_ref, same_ref,
               same_sub_ref, o_ref, st_ref, n_chunks):
    each = _each
    row_in_sub = lax.broadcasted_iota(jnp.int32, (CHUNK, LIN_WIDTH), 0) % SUB
    dm = dm_ref[...]
    heads = range(N_LIN_HEADS)
    n_sub = CHUNK // SUB

    def own_head(m, mask_ref):
        return jnp.concatenate([m] * N_LIN_HEADS, axis=0) * mask_ref[...]

    rows = [pl.ds(j * CHUNK, CHUNK) for j in range(n_chunks)]
    ps = [h_ref[0, rw, :] for rw in rows]
    q = each(lambda p: p[:, 0:256] * (HEAD_DIM ** -0.5), ps)
    k = each(lambda p: p[:, 256:512], ps)
    v = each(lambda p: p[:, 512:768], ps)
    g = each(lambda p: p[:, 768:1024], ps)
    z = each(lambda p: _dot3(p[:, 1024:1152], wa2_ref[0], wa2_ref[1]) + ba_ref[...], ps)
    yield
    log_a = each(lambda x: -_softplus(-x) * (1.0 / GLA_TAU), z)
    yield
    b = each(lambda x: _dot_exact(tril_ref[...], x), log_a)
    yield
    b_end = each(lambda x: _dot_exact(blk_ref[...], x), log_a)
    yield
    qe = each(lambda x, bb: x * jnp.exp(bb), q, b)
    kd = each(lambda x, be, bb: x * jnp.exp(be - bb), k, b_end, b)
    p_end = each(jnp.exp, b_end)
    v_st = each(lambda x: jnp.concatenate([x[:, HEAD_DIM * h:HEAD_DIM * (h + 1)] for h in heads], axis=0), v)
    yield

    b2 = each(lambda x: x * LOG2E, b)
    at = [jnp.zeros((CHUNK, LIN_WIDTH), F32) for _ in ps]
    for u0 in range(0, SUB, N_LIN_HEADS):
        us = range(u0, u0 + N_LIN_HEADS)
        es = each(lambda bb, qq, kk: jnp.concatenate(
            [jnp.exp2(jnp.where(row_in_sub <= u, _group_rows(bb, u) - bb, MASK_NEG)) * (_group_rows(qq, u) * kk)
             for u in us], axis=0), b2, q, k)
        yield
        a_rep = each(lambda e: _dot(e, ones_ref[...]), es)
        yield
        for i, u in enumerate(us):
            at = each(lambda acc, ar: acc + jnp.where(dm == u, ar[CHUNK * i:CHUNK * (i + 1), :], 0.0), at, a_rep)
        yield
    o_st = each(lambda x, vs: _dot(own_head(x, same_ref), vs, _TN), at, v_st)
    yield

    upd, q_own = [], []
    for i in range(n_sub):
        rs = slice(SUB * i, SUB * (i + 1))
        upd.append(each(lambda vs, kk: _dot(
            jnp.concatenate([vs[CHUNK * h + SUB * i:CHUNK * h + SUB * (i + 1), :] for h in heads], axis=0),
            own_head(kk[rs], same_sub_ref), _TN), v_st, kd))
        q_own.append(each(lambda x: own_head(x[rs], same_sub_ref), qe))
        yield
    st = st_ref[...]
    for c, rw in enumerate(rows):
        inter = []
        for i in range(n_sub):
            inter.append(_dot(q_own[i][c], st, _NT))
            st = st * p_end[c][SUB * i:SUB * i + 1, :] + upd[i][c]
        o_c = o_st[c] + jnp.concatenate(
            [inter[i][SUB * h:SUB * (h + 1), :] for h in heads for i in range(n_sub)], axis=0)
        o_c = _head_norm(o_c, 1.0, 0.0, LN_EPS)
        o = jnp.concatenate([o_c[CHUNK * h:CHUNK * (h + 1), :] for h in heads], axis=1)
        o_ref[0, rw, :] = (o * lnw_ref[...] + lnb_ref[...]) * (g[c] * _sigmoid(g[c]))
        yield
    st_ref[...] = st


def _rwkv_steps(h_ref, vf_ref, mu_ref, w0_ref, w2_ref, a0_ref, a2_ref, g2_ref, kk_ref, ka_ref, rk_ref,
                lnw_ref, lnb_ref, v0_ref, v2_ref, tril_ref, ones_ref, same_ref, strict_ref, incl_ref, eye_ref,
                o_ref, st_ref, prev_ref, n_chunks, has_vres):
    each = _each
    row = lax.broadcasted_iota(jnp.int32, (CHUNK, RWKV_SLAB), 0)
    blocks = range(N_LIN_HEADS // RWKV_HB)
    in_block = range(RWKV_HB)
    nr = RWKV_HB * CHUNK

    def tile_rows(m):
        return jnp.concatenate([m] * RWKV_HB, axis=0)

    def own_head(m):
        return tile_rows(m) * same_ref[...]

    def stack(m):
        return jnp.concatenate([m[:, HEAD_DIM * h:HEAD_DIM * (h + 1)] for h in in_block], axis=0)

    def unstack(m):
        return jnp.concatenate([m[CHUNK * h:CHUNK * (h + 1), :] for h in in_block], axis=1)

    def per_block(xs):
        return [x[:, RW_BW * bi:RW_BW * (bi + 1)] for x in xs for bi in blocks]

    rows = [pl.ds(j * CHUNK, CHUNK) for j in range(n_chunks)]
    ps = [h_ref[0, rw, :] for rw in rows]
    vfs = [vf_ref[0, rw, :] if has_vres else None for rw in rows]
    prev_rows = [prev_ref[...]] + [p[CHUNK - 1:CHUNK, :] for p in ps[:-1]]

    prev = each(lambda p, pr: jnp.where(row == 0, pr, pltpu.roll(p, 1, 0)), ps, prev_rows)
    xm = each(lambda p, pv: p + (pv - p) * mu_ref[...], ps, prev)
    yield
    r = each(lambda x: x[:, 0:256], xm)
    k = each(lambda x: x[:, 256:512], xm)
    v = each(lambda x: x[:, 512:768], xm)
    v_shift = v
    w_log = each(lambda x: -_softplus(-(w0_ref[...] + _dot3(jnp.tanh(x[:, RW_WLR:RW_ALR]), w2_ref[0], w2_ref[1])))
                 - 0.5, xm)
    yield
    lw = each(lambda w: -jnp.exp(w), w_log)
    a = each(lambda x: _sigmoid(a0_ref[...] + _dot3(x[:, RW_ALR:RW_GLR], a2_ref[0], a2_ref[1])), xm)
    yield
    g = each(lambda x: _dot3(_sigmoid(x[:, RW_GLR:RW_VLR]), g2_ref[0], g2_ref[1]), xm)
    yield
    if has_vres:
        mixv = each(lambda x: _sigmoid(v0_ref[...] + _dot3(x[:, RW_VLR:RWKV_SLAB], v2_ref[0], v2_ref[1])), xm)
        v = each(lambda vv, vf, mx: vv + (vf - vv) * mx, v, vfs, mixv)
        yield
    kkv = each(lambda kk: kk * kk_ref[...], k)
    nrm = each(lambda kv: jnp.sqrt(_dot_exact(kv * kv, ones_ref[...])), kkv)
    yield
    kkn = each(lambda kv, n: kv / jnp.maximum(n, 1e-12), kkv, nrm)
    k2 = each(lambda kk, aa: kk * (1.0 + (aa - 1.0) * ka_ref[...]), k, a)
    bv = each(lambda kn, aa: kn * aa, kkn, a)
    c = each(lambda w: _dot_exact(tril_ref[...], w), lw)
    yield
    c_end = each(lambda cc: cc[CHUNK - 1:CHUNK, :], c)
    bonus = each(lambda rr, kk, vv: _dot_exact(rr * kk * rk_ref[...], ones_ref[...]) * vv, r, k2, v)
    yield
    at = each(lambda kn, cc, w: -kn * jnp.exp(cc - w), kkn, c, lw)
    rt = each(lambda rr, cc: rr * jnp.exp(cc), r, c)
    yield
    e_nc = each(lambda cc: jnp.exp(-cc), c)
    bt = each(lambda b_, e: b_ * e, bv, e_nc)
    kt = each(lambda kk, e: kk * e, k2, e_nc)
    yield
    e_rest = each(lambda ce, cc: jnp.exp(ce - cc), c_end, c)
    b_rest = each(lambda b_, e: b_ * e, bv, e_rest)
    k_rest = each(lambda kk, e: kk * e, k2, e_rest)
    p_end = per_block(each(jnp.exp, c_end))
    yield

    ar = each(lambda x, y: jnp.concatenate([own_head(x), own_head(y)], axis=0), per_block(at), per_block(rt))
    yield
    bk = each(lambda x, y: jnp.concatenate([tile_rows(x), tile_rows(y)], axis=0), per_block(bt), per_block(kt))
    yield
    gm = each(lambda x, y: _dot(x, y, _NT), ar, bk)
    yield
    n_ab = each(lambda m: m[0:nr, 0:nr] * strict_ref[...], gm)
    m_ak = each(lambda m: m[0:nr, nr:2 * nr] * strict_ref[...], gm)
    r_bk = each(lambda m: m[nr:2 * nr, :] * jnp.concatenate([incl_ref[...], incl_ref[...]], axis=1), gm)
    yield
    v_st = each(stack, per_block(v))
    mv = each(_dot, m_ak, v_st)
    yield
    inv = each(lambda n: eye_ref[...] + n, n_ab)
    pw = n_ab
    for _ in range(5):
        pw = each(lambda x: _dot(x, x), pw)
        yield
        inv = each(lambda i_, x: i_ + _dot(i_, x), inv, pw)
        yield
    bk_rest = each(lambda x, y: jnp.concatenate([own_head(x), own_head(y)], axis=0),
                   per_block(b_rest), per_block(k_rest))
    yield
    a_bar = each(lambda i_, x: _dot(i_, x[0:nr]), inv, ar)
    yield
    uv0 = each(lambda i_, m, vs: jnp.concatenate([_dot(i_, m), vs], axis=0), inv, mv, v_st)
    yield
    r_bar = each(lambda x, rb, ab: x[nr:2 * nr] + _dot(rb[:, 0:nr], ab), ar, r_bk, a_bar)
    yield
    y0 = each(_dot, r_bk, uv0)
    yield
    g_mat = each(lambda ab, br: _dot(ab, br[0:nr], _TN), a_bar, bk_rest)
    yield
    sc = each(lambda z, br: _dot(z, br, _TN), uv0, bk_rest)
    yield

    n_blk = len(blocks)
    s = per_block([st_ref[...]])
    for j, rw in enumerate(rows):
        it = slice(n_blk * j, n_blk * (j + 1))
        y = each(lambda rb, s_, y_: unstack(_head_norm(_dot(rb, s_, _NT) + y_, 1.0, 0.0, RWKV_GN_EPS)),
                 r_bar[it], s, y0[it])
        s = each(lambda s_, pe, gm_, c_: s_ * pe + _dot(s_, gm_) + c_, s, p_end[it], g_mat[it], sc[it])
        o_ref[0, rw, :] = (jnp.concatenate(y, axis=1) * lnw_ref[...] + lnb_ref[...] + bonus[j]) * g[j]
        if not has_vres:
            vf_ref[0, rw, :] = v_shift[j]
        yield
    prev_ref[...] = ps[-1][CHUNK - 1:CHUNK, :]
    st_ref[...] = jnp.concatenate(s, axis=1)


N_GLA_IN, N_RWKV_IN = 11, 18


def _linmix_kernel(*refs, n_chunks, has_vres):
    refs = list(refs)
    gla_in = refs[:N_GLA_IN]
    n_rw = N_RWKV_IN + (3 if has_vres else 0)
    rwkv_in = refs[N_GLA_IN:N_GLA_IN + n_rw]
    outs = refs[N_GLA_IN + n_rw:]
    if has_vres:
        (h_ref, vf_ref, mu, w0, w2, a0, a2, g2, kk, ka, rk, lnw, lnb, v0, v2, *consts) = rwkv_in
        o_gla_ref, o_rwkv_ref, gla_st, rwkv_st, prev_ref = outs
    else:
        (h_ref, mu, w0, w2, a0, a2, g2, kk, ka, rk, lnw, lnb, *consts) = rwkv_in
        v0 = v2 = None
        o_gla_ref, o_rwkv_ref, vf_ref, gla_st, rwkv_st, prev_ref = outs

    @pl.when(pl.program_id(1) == 0)
    def _():
        gla_st[...] = jnp.zeros_like(gla_st)
        rwkv_st[...] = jnp.zeros_like(rwkv_st)
        prev_ref[...] = jnp.zeros_like(prev_ref)

    _run_together(
        _gla_steps(*gla_in, o_gla_ref, gla_st, n_chunks),
        _rwkv_steps(h_ref, vf_ref, mu, w0, w2, a0, a2, g2, kk, ka, rk, lnw, lnb, v0, v2, *consts,
                    o_rwkv_ref, rwkv_st, prev_ref, n_chunks, has_vres))


def _linmix(hg, hr, v_first, gla_small, rwkv_small, ts=512):
    bsz, s, _ = hg.shape
    has_vres = v_first is not None
    r = np.arange(CHUNK)
    c = np.arange(LIN_WIDTH)
    ones = (c[:, None] // HEAD_DIM == c[None, :] // HEAD_DIM).astype(np.float32)
    tril16 = ((r[:, None] // SUB == r[None, :] // SUB) & (r[None, :] <= r[:, None])).astype(np.float32)
    blk16 = (r[:, None] // SUB == r[None, :] // SUB).astype(np.float32)
    same_sub = (r[:, None] // SUB == c[None, :] // HEAD_DIM).astype(np.float32)
    dm = ((c[None, :] % HEAD_DIM) - SUB * (r[:, None] // SUB)).astype(np.int32)
    gla_consts = [jnp.asarray(tril16, BF16), jnp.asarray(blk16, BF16), jnp.asarray(ones, BF16), jnp.asarray(dm),
                  jnp.asarray(ones), jnp.asarray(same_sub)]
    tril = (r[None, :] <= r[:, None]).astype(np.float32)
    i = np.arange(RW_BW)
    same = (i[:, None] // HEAD_DIM == i[None, :] // HEAD_DIM).astype(np.float32)
    t_in, s_in = i[:, None] % CHUNK, i[None, :] % CHUNK
    rwkv_consts = [jnp.asarray(tril, BF16), jnp.asarray(ones, BF16), jnp.asarray(same),
                   jnp.asarray(same * (s_in < t_in)), jnp.asarray(same * (s_in <= t_in)),
                   jnp.asarray(np.eye(RW_BW, dtype=np.float32))]

    def seq(width):
        return pl.BlockSpec((1, ts, width), lambda b, j: (b, j, 0))

    gla_args = [hg] + list(gla_small) + gla_consts
    rwkv_args = [hr] + ([v_first] if has_vres else []) + list(rwkv_small) + rwkv_consts
    assert len(gla_args) == N_GLA_IN and len(rwkv_args) == N_RWKV_IN + (3 if has_vres else 0)
    in_specs = ([seq(GLA_SLAB)] + [_full(a.shape) for a in gla_args[1:]]
                + [seq(RWKV_SLAB)] + ([seq(LIN_WIDTH)] if has_vres else [])
                + [_full(a.shape) for a in rwkv_args[(2 if has_vres else 1):]])
    n_out = 2 if has_vres else 3
    out_sd = jax.ShapeDtypeStruct((bsz, s, LIN_WIDTH), F32)
    return pl.pallas_call(
        functools.partial(_linmix_kernel, n_chunks=ts // CHUNK, has_vres=has_vres),
        grid=(bsz, s // ts),
        in_specs=in_specs,
        out_specs=[seq(LIN_WIDTH)] * n_out,
        out_shape=[out_sd] * n_out,
        scratch_shapes=[pltpu.VMEM((HEAD_DIM, LIN_WIDTH), F32),
                        pltpu.VMEM((HEAD_DIM, LIN_WIDTH), F32),
                        pltpu.VMEM((1, RWKV_SLAB), F32)],
        compiler_params=_params("parallel", "arbitrary"),
    )(*gla_args, *rwkv_args)


def _nsa_cmp_kernel(kc_ref, vc_ref, pk_ref, pv_ref, wk1h_ref, wk1l_ref, wk2h_ref, wk2l_ref,
                    wv1h_ref, wv1l_ref, wv2h_ref, wv2l_ref, ko_ref, vo_ref, *, n_blk):
    half = NSA_CMP_BLOCK // 2

    def compress(t_ref, pos_ref, w1h_ref, w1l_ref, w2h_ref, w2l_ref, o_ref):
        for g in range(NSA_KV_HEADS):
            gs = slice(HEAD_DIM * g, HEAD_DIM * (g + 1))
            top = jnp.zeros((n_blk, NSA_CMP_HIDDEN), F32)
            bot = jnp.zeros((n_blk, NSA_CMP_HIDDEN), F32)
            for l in range(half):
                rows = t_ref[0, pl.ds(l, n_blk, stride=NSA_CMP_STRIDE), :][:, gs]
                top = top + _dot3(rows + pos_ref[l:l + 1, :], w1h_ref[l], w1l_ref[l])
                bot = bot + _dot3(rows + pos_ref[half + l:half + l + 1, :], w1h_ref[half + l], w1l_ref[half + l])
            hid = top + pltpu.roll(bot, n_blk - 1, 0)
            act = 0.5 * hid * (1.0 + jnp.tanh(0.7978845608028654 * (hid + 0.044715 * hid * hid * hid)))
            o_ref[0, :, gs] = _dot3(act, w2h_ref[...], w2l_ref[...])

    compress(kc_ref, pk_ref, wk1h_ref, wk1l_ref, wk2h_ref, wk2l_ref, ko_ref)
    compress(vc_ref, pv_ref, wv1h_ref, wv1l_ref, wv2h_ref, wv2l_ref, vo_ref)


def _nsa_cmp(kv, pos_k, pos_v, wk1, wk2, wv1, wv2):
    bsz, s, _ = kv.shape
    n_blk = s // NSA_CMP_STRIDE
    wk1 = wk1.reshape(NSA_CMP_BLOCK, HEAD_DIM, NSA_CMP_HIDDEN)
    wv1 = wv1.reshape(NSA_CMP_BLOCK, HEAD_DIM, NSA_CMP_HIDDEN)
    small = [pos_k, pos_v, *_split_bf16(wk1), *_split_bf16(wk2), *_split_bf16(wv1), *_split_bf16(wv2)]
    out_spec = pl.BlockSpec((1, n_blk, LANES), lambda b: (b, 0, 0))
    out_sd = jax.ShapeDtypeStruct((bsz, n_blk, LANES), F32)
    return pl.pallas_call(
        functools.partial(_nsa_cmp_kernel, n_blk=n_blk),
        grid=(bsz,),
        in_specs=[pl.BlockSpec((1, s, LANES), lambda b: (b, 0, 0)),
                  pl.BlockSpec((1, s, LANES), lambda b: (b, 0, 1))] + [_full(a.shape) for a in small],
        out_specs=[out_spec, out_spec],
        out_shape=[out_sd, out_sd],
        compiler_params=_params("parallel"),
    )(kv, kv, *small)


def _nsa_attn_kernel(q_ref, gt_ref, kc_ref, vc_ref, ks_ref, kw_ref, vt_ref, ovt_ref,
                     o_ref, acc_ref, s_ref, cm_ref, m_ref, *, n_cmp):
    qi = pl.program_id(1)
    cols = NSA_HPG * QBLK
    n_sel_blk = ovt_ref.shape[0]
    key_off = lax.broadcasted_iota(jnp.int32, (QBLK, cols), 0)
    t_pos = qi * QBLK + (lax.broadcasted_iota(jnp.int32, (QBLK, cols), 1) & (QBLK - 1))
    gt_t = gt_ref[0].T
    zeros64 = jnp.zeros((QBLK, HEAD_DIM), F32)
    m0 = jnp.full((1, cols), MASK_NEG, F32)

    groups = list(range(NSA_KV_HEADS))

    each, run_together = _each, _run_together

    def stacked_q(g):
        pieces = []
        for hp in range(NSA_HPG):
            qh = q_ref[0, :, (g * NSA_HPG + hp) * HEAD_DIM:(g * NSA_HPG + hp + 1) * HEAD_DIM]
            pieces.append(jnp.concatenate([qh, zeros64] if g == 0 else [zeros64, qh], axis=1))
        return jnp.concatenate(pieces, axis=0)

    qs = each(stacked_q, groups)
    qs_b = each(lambda q: (q * LOG2E).astype(BF16), qs)

    def select_steps():
        valid_c = (key_off * NSA_CMP_STRIDE + (NSA_CMP_BLOCK - 1) <= t_pos) & (key_off < n_cmp)
        s_c = each(lambda q: jnp.where(valid_c, _dot3(kc_ref[0], *_split_bf16(q), _NT), MASK_NEG), qs)
        yield
        e_c = each(lambda x: jnp.exp(x - jnp.max(x, axis=0, keepdims=True)), s_c)
        yield
        p_c = each(lambda e: jnp.where(valid_c, e / jnp.sum(e, axis=0, keepdims=True), 0.0), e_c)
        yield
        o_cmp = each(lambda p, g: _dot(vc_ref[0], p, _TN)[HEAD_DIM * g:HEAD_DIM * (g + 1), :], p_c, groups)
        yield
        p_sum = each(lambda p: p[:, 0:QBLK] + p[:, QBLK:2 * QBLK] + p[:, 2 * QBLK:3 * QBLK] + p[:, 3 * QBLK:], p_c)
        imp = each(lambda p: _dot_exact(ovt_ref[...], p), p_sum)
        yield
        j = lax.broadcasted_iota(jnp.int32, (n_sel_blk, QBLK), 0)
        tq = qi * QBLK + lax.broadcasted_iota(jnp.int32, (n_sel_blk, QBLK), 1)
        cur = tq // NSA_SEL_BLOCK
        forced = (j == 0) | (j == cur) | (j == cur - 1)
        score = each(lambda x: jnp.where(forced, 1e9, jnp.where(j > cur, -1e9, x)), imp)
        rank = [jnp.zeros((n_sel_blk, QBLK), F32) for _ in groups]
        for i in range(n_sel_blk):
            rank = each(lambda rk, sc: rk + jnp.where(
                (sc[i:i + 1, :] > sc) | ((sc[i:i + 1, :] == sc) & (j > i)), 1.0, 0.0), rank, score)
            if i % 4 == 3:
                yield
        return o_cmp, each(lambda rk: jnp.where(rk < NSA_N_SELECT, 1.0, 0.0), rank)

    def key_scores(k_ref, q, kb0, n_kb, mask):
        k0 = pl.multiple_of(kb0 * QBLK, QBLK)
        s = lax.dot_general(k_ref[0, pl.ds(k0, n_kb * QBLK), :], q, _NT, preferred_element_type=F32)
        if mask is not None:
            key_pos = k0 + lax.broadcasted_iota(jnp.int32, (n_kb * QBLK, cols), 0)
            t_all = jnp.concatenate([t_pos] * n_kb, axis=0)
            valid = key_pos <= t_all
            if mask == "window":
                valid = valid & (key_pos > t_all - NSA_WINDOW)
            s = jnp.where(valid, s, MASK_NEG)
        return s

    def softmax_steps(slabs, kb0, n_kb, block_scores, col_max, m_old):
        m_new = each(jnp.maximum, m_old, col_max)
        pv = [jnp.zeros((VT_ROWS, cols), F32) for _ in slabs]
        for i in range(n_kb):
            pv = [acc + jnp.dot(vt_ref[0, kb0 + i, slab * VT_ROWS:(slab + 1) * VT_ROWS, :],
                                jnp.exp2(rd(i) - mn).astype(BF16), preferred_element_type=F32)
                  for acc, rd, mn, slab in zip(pv, block_scores, m_new, slabs)]
            yield
        for slab, m, mn, x in zip(slabs, m_old, m_new, pv):
            acc_ref[slab] = acc_ref[slab] * jnp.exp2(m - mn) + x
        return m_new

    def window_steps():
        n_win = NSA_WINDOW // QBLK + 1
        kb_win = jnp.maximum(qi - (n_win - 1), 0)
        s = []
        for g in groups:
            s.append(key_scores(kw_ref, qs_b[g], kb_win, n_win, "window"))
            yield
        col_max = each(lambda x: jnp.max(x, axis=0, keepdims=True), s)
        yield
        blocks = [lambda i, x=x: x[QBLK * i:QBLK * (i + 1), :] for x in s]
        yield from softmax_steps([NSA_KV_HEADS + g for g in groups], kb_win, n_win, blocks, col_max, [m0] * len(s))

    def normalised(slab):
        return acc_ref[slab, 0:HEAD_DIM, :] / acc_ref[slab, HEAD_DIM:HEAD_DIM + 1, :]

    acc_ref[...] = jnp.zeros_like(acc_ref)
    for g in groups:
        m_ref[g] = m0

    (o_cmp, sel), _ = run_together(select_steps(), window_steps())

    def with_selection(q, sl):
        off = jnp.where(sl > 0.5, 0.0, MASK_NEG).T.astype(BF16)
        pad = jnp.zeros((cols, LANES - n_sel_blk), BF16)
        return jnp.concatenate([q, jnp.concatenate([off] * NSA_HPG, axis=0), pad], axis=1)

    qs_sel = each(with_selection, qs_b, sel)

    def score_steps(i, mask):
        s = []
        for g in groups:
            s.append(key_scores(ks_ref, qs_sel[g], i * SLC_KB, SLC_KB, mask))
            yield
        return s, each(lambda x: jnp.max(x, axis=0, keepdims=True), s)

    def park(scores_and_max):
        for g, (x, cm) in enumerate(zip(*scores_and_max)):
            s_ref[g] = x
            cm_ref[g] = cm

    def consume_steps(i):
        blocks = [lambda b, g=g: s_ref[g, QBLK * b:QBLK * (b + 1), :] for g in groups]
        m_new = yield from softmax_steps(groups, i * SLC_KB, SLC_KB, blocks, [cm_ref[g] for g in groups],
                                         [m_ref[g] for g in groups])
        for g in groups:
            m_ref[g] = m_new[g]

    n_full = qi // SLC_KB

    @pl.when(n_full >= 1)
    def _():
        park(run_together(score_steps(0, None))[0])

    def pipelined(i, carry):
        nxt, _ = run_together(score_steps(i + 1, None), consume_steps(i))
        park(nxt)
        return carry

    lax.fori_loop(0, n_full - 1, pipelined, 0)

    @pl.when(n_full >= 1)
    def _():
        nxt, _ = run_together(score_steps(n_full, "causal"), consume_steps(n_full - 1))
        park(nxt)

    @pl.when(n_full == 0)
    def _():
        park(run_together(score_steps(0, "causal"))[0])

    run_together(consume_steps(n_full))

    o_groups = []
    for g in groups:
        def gate(branch):
            return jnp.concatenate(
                [gt_t[(g * NSA_HPG + hp) * 3 + branch:(g * NSA_HPG + hp) * 3 + branch + 1, :]
                 for hp in range(NSA_HPG)], axis=1)

        o_groups.append(gate(0) * o_cmp[g] + gate(1) * normalised(g) + gate(2) * normalised(NSA_KV_HEADS + g))

    o_all = jnp.concatenate(o_groups, axis=0)
    for hp in range(NSA_HPG):
        tile = o_all[:, QBLK * hp:QBLK * (hp + 1)].T
        for g in range(NSA_KV_HEADS):
            c0 = (g * NSA_HPG + hp) * HEAD_DIM
            o_ref[0, :, c0:c0 + HEAD_DIM] = tile[:, HEAD_DIM * g:HEAD_DIM * (g + 1)]


def _nsa_attn(qr, kk, vt, gt, kcmp, vcmp):
    bsz, s, _ = qr.shape
    n_blk = s // NSA_CMP_STRIDE
    n_cmp = n_blk - NSA_CMP_BLOCK // NSA_CMP_STRIDE + 1
    n_sel = s // NSA_SEL_BLOCK
    assert n_blk == QBLK, "compressed-score tiles are laid out one stride block per sublane row"
    c0 = np.arange(n_blk) * NSA_CMP_STRIDE
    s0 = np.arange(n_sel) * NSA_SEL_BLOCK
    lo = np.maximum(c0[:, None], s0[None, :])
    hi = np.minimum(c0[:, None] + NSA_CMP_BLOCK, s0[None, :] + NSA_SEL_BLOCK)
    overlap = (np.maximum(hi - lo, 0) / NSA_CMP_STRIDE).astype(np.float32)
    overlap[n_cmp:] = 0.0
    consts = [jnp.asarray(overlap.T, BF16)]

    cmp_spec = pl.BlockSpec((1, n_blk, LANES), lambda b, i: (b, 0, 0))
    return pl.pallas_call(
        functools.partial(_nsa_attn_kernel, n_cmp=n_cmp),
        grid=(bsz, s // QBLK),
        in_specs=[pl.BlockSpec((1, QBLK, NSA_WIDTH), lambda b, i: (b, i, 0)),
                  pl.BlockSpec((1, QBLK, LANES), lambda b, i: (b, i, 0)),
                  cmp_spec, cmp_spec,
                  pl.BlockSpec((1, s, 2 * LANES), lambda b, i: (b, 0, 0)),
                  pl.BlockSpec((1, s, LANES), lambda b, i: (b, 0, 2)),
                  pl.BlockSpec((1, s // QBLK, 4 * VT_ROWS, QBLK), lambda b, i: (b, 0, 0, 0))]
        + [_full(a.shape) for a in consts],
        out_specs=pl.BlockSpec((1, QBLK, NSA_WIDTH), lambda b, i: (b, i, 0)),
        out_shape=jax.ShapeDtypeStruct((bsz, s, NSA_WIDTH), F32),
        scratch_shapes=[pltpu.VMEM((2 * NSA_KV_HEADS, VT_ROWS, NSA_HPG * QBLK), F32),
                        pltpu.VMEM((NSA_KV_HEADS, SLC_KB * QBLK, NSA_HPG * QBLK), F32),
                        pltpu.VMEM((NSA_KV_HEADS, 1, NSA_HPG * QBLK), F32),
                        pltpu.VMEM((NSA_KV_HEADS, 1, NSA_HPG * QBLK), F32)],
        compiler_params=_params("parallel", "arbitrary"),
    )(qr, gt, kcmp, vcmp, kk, kk, vt, *consts)


def _mix_ffn_kernel(x_ref, og_ref, or_ref, on_ref, wo_ref, l1w_ref, l1b_ref, wg_ref, wu_ref, wd_ref,
                    l2w_ref, l2b_ref, o_ref, *, alpha):
    mix = (_dot(og_ref[...], wo_ref[0:LIN_WIDTH, :])
           + _dot(or_ref[...], wo_ref[LIN_WIDTH:2 * LIN_WIDTH, :])
           + _dot(on_ref[...], wo_ref[2 * LIN_WIDTH:, :]))
    x1 = _layer_norm(alpha * x_ref[...] + mix, l1w_ref[...], l1b_ref[...])
    xb = x1.astype(BF16)
    ffn = jnp.zeros_like(x1)
    for c0, c1 in FFN_SPLITS:
        gate = jnp.dot(xb, wg_ref[:, c0:c1], preferred_element_type=F32)
        up = jnp.dot(xb, wu_ref[:, c0:c1], preferred_element_type=F32)
        ffn = ffn + _dot(gate * _sigmoid(gate) * up, wd_ref[c0:c1, :])
    o_ref[...] = _layer_norm(alpha * x1 + ffn, l2w_ref[...], l2b_ref[...])


def _mix_ffn(x2, og, orw, on, wo, l1w, l1b, wg, wu, wd, l2w, l2b, alpha, tm=512):
    t = x2.shape[0]

    def rows(width):
        return pl.BlockSpec((tm, width), lambda i: (i, 0))

    def resident(a):
        return pl.BlockSpec(a.shape, lambda i: (0,) * a.ndim, pipeline_mode=pl.Buffered(1))

    consts = [wo, l1w, l1b, wg, wu, wd, l2w, l2b]
    return pl.pallas_call(
        functools.partial(_mix_ffn_kernel, alpha=alpha),
        grid=(t // tm,),
        in_specs=[rows(D_MODEL), rows(LIN_WIDTH), rows(LIN_WIDTH), rows(NSA_WIDTH)] + [resident(a) for a in consts],
        out_specs=rows(D_MODEL),
        out_shape=jax.ShapeDtypeStruct((t, D_MODEL), F32),
        compiler_params=_params("parallel"),
    )(x2, og, orw, on, *consts)


def _pad_cols(w, width):
    return jnp.pad(w, ((0, 0), (0, width - w.shape[1])))


def _pad_rows(w, height):
    return jnp.pad(w, ((0, height - w.shape[0]), (0, 0)))


def _rwkv_slab_cols(w, w_vres):
    o = 3 * LIN_WIDTH
    parts = [w[:, :o],
             _pad_cols(w[:, o:o + RWKV_DECAY_RANK], LANES),
             _pad_cols(w[:, o + RWKV_DECAY_RANK:o + RWKV_DECAY_RANK + RWKV_ICLR_RANK], LANES),
             _pad_cols(w[:, o + RWKV_DECAY_RANK + RWKV_ICLR_RANK:], 2 * LANES),
             _pad_cols(w_vres, LANES)]
    return jnp.concatenate(parts, axis=1)


def _rope_tables(s):
    half = HEAD_DIM // 2
    inv = ROPE_THETA ** (-jnp.arange(half, dtype=F32) / half)
    ang = jnp.arange(s, dtype=jnp.int32).astype(F32)[:, None] * inv
    cos, sin = jnp.cos(ang), jnp.sin(ang)
    cos2 = jnp.tile(jnp.concatenate([cos, cos], axis=1), (1, LANES // HEAD_DIM))
    sin2 = jnp.tile(jnp.concatenate([-sin, sin], axis=1), (1, LANES // HEAD_DIM))
    return cos2, sin2


def kernel(x, w_in, w_in_vres, gla_w_a2, gla_b_a, gla_ln_w, gla_ln_b, rwkv_mu, rwkv_mu_vres, rwkv_w0, rwkv_w2, rwkv_a0, rwkv_a2, rwkv_v0, rwkv_v2, rwkv_g2, rwkv_k_k, rwkv_k_a, rwkv_r_k, rwkv_ln_w, rwkv_ln_b, nsa_pos_k, nsa_pos_v, nsa_wk1, nsa_wk2, nsa_wv1, nsa_wv2, w_out, ln1_w, ln1_b, ffn_w_gate, ffn_w_up, ffn_w_down, ln2_w, ln2_b):
    bsz, s, d = x.shape
    depth = w_in.shape[0]
    alpha = float((2 * depth) ** 0.25)
    gla_cols = 4 * LIN_WIDTH + GLA_GATE_RANK
    rwkv_cols = 3 * LIN_WIDTH + RWKV_DECAY_RANK + RWKV_ICLR_RANK + RWKV_GATE_RANK
    cos2, sin2 = _rope_tables(s)
    row = lambda a: a.reshape(1, -1)

    x2 = x.reshape(bsz * s, d)
    v_first = None
    for l in range(depth):
        w = w_in[l]
        w_r = w[:, gla_cols:gla_cols + rwkv_cols]
        if l == 0:
            w_vres = jnp.zeros((d, RWKV_VRES_RANK), F32)
            mu_vres = jnp.zeros((1, RWKV_VRES_RANK), F32)
        else:
            w_vres = w_in_vres[l - 1]
            mu_vres = row(rwkv_mu_vres[l - 1])
        wg = _pad_cols(w[:, :gla_cols], GLA_SLAB).astype(BF16)
        wr = _rwkv_slab_cols(w_r, w_vres).astype(BF16)
        wn = _pad_cols(w[:, gla_cols + rwkv_cols:], NSA_SLAB).astype(BF16)
        hg, hr, qr, kvc, kk, vt, gt = _proj(x2, wg, wr, wn, cos2, sin2)
        hg = hg.reshape(bsz, s, GLA_SLAB)
        hr = hr.reshape(bsz, s, RWKV_SLAB)
        qr = qr.reshape(bsz, s, NSA_WIDTH)
        kvc = kvc.reshape(bsz, s, 2 * LANES)
        kk = kk.reshape(bsz, s, 3 * LANES)
        vt = vt.reshape(bsz, s // QBLK, 4 * VT_ROWS, QBLK)
        gt = gt.reshape(bsz, s, LANES)

        gla_small = [_hi_lo(_pad_rows(gla_w_a2[l], LANES)), row(gla_b_a[l]), row(gla_ln_w[l]), row(gla_ln_b[l])]
        mu = _rwkv_slab_cols(row(rwkv_mu[l]), mu_vres)
        rwkv_small = [mu, row(rwkv_w0[l]), _hi_lo(_pad_rows(rwkv_w2[l], LANES)), row(rwkv_a0[l]),
                      _hi_lo(_pad_rows(rwkv_a2[l], LANES)), _hi_lo(_pad_rows(rwkv_g2[l], 2 * LANES)),
                      row(rwkv_k_k[l]), row(rwkv_k_a[l]), row(rwkv_r_k[l]), row(rwkv_ln_w[l]), row(rwkv_ln_b[l])]
        if l == 0:
            o_gla, o_rwkv, v_first = _linmix(hg, hr, None, gla_small, rwkv_small)
        else:
            rwkv_small += [row(rwkv_v0[l - 1]), _hi_lo(_pad_rows(rwkv_v2[l - 1], LANES))]
            o_gla, o_rwkv = _linmix(hg, hr, v_first, gla_small, rwkv_small)

        kcmp, vcmp = _nsa_cmp(kvc, nsa_pos_k[l], nsa_pos_v[l], nsa_wk1[l], nsa_wk2[l], nsa_wv1[l], nsa_wv2[l])
        o_nsa = _nsa_attn(qr, kk, vt, gt, kcmp, vcmp)

        x2 = _mix_ffn(x2, o_gla.reshape(bsz * s, LIN_WIDTH), o_rwkv.reshape(bsz * s, LIN_WIDTH),
                      o_nsa.reshape(bsz * s, NSA_WIDTH), w_out[l].astype(BF16), row(ln1_w[l]), row(ln1_b[l]),
                      ffn_w_gate[l].astype(BF16), ffn_w_up[l].astype(BF16), ffn_w_down[l].astype(BF16),
                      row(ln2_w[l]), row(ln2_b[l]), alpha)
    return x2.reshape(bsz, s, d)
```

```python
import functools

import numpy as np
import jax
import jax.numpy as jnp
from jax import lax
from jax.experimental import pallas as pl
from jax.experimental.pallas import tpu as pltpu

F32 = jnp.float32
BF16 = jnp.bfloat16

D_MODEL = 1024
HEAD_DIM = 64
N_LIN_HEADS = 4
LIN_WIDTH = N_LIN_HEADS * HEAD_DIM
GLA_GATE_RANK = 16
GLA_TAU = 16.0
RWKV_DECAY_RANK = 64
RWKV_ICLR_RANK = 64
RWKV_VRES_RANK = 32
RWKV_GATE_RANK = 160
RWKV_GN_EPS = 64e-5
NSA_HEADS = 8
NSA_KV_HEADS = 2
NSA_HPG = NSA_HEADS // NSA_KV_HEADS
NSA_WIDTH = NSA_HEADS * HEAD_DIM
NSA_KV_WIDTH = NSA_KV_HEADS * HEAD_DIM
NSA_CMP_BLOCK = 32
NSA_CMP_STRIDE = 16
NSA_CMP_HIDDEN = 256
NSA_SEL_BLOCK = 64
NSA_N_SELECT = 16
NSA_WINDOW = 512
ROPE_THETA = 10000.0
FFN_HIDDEN = 2816
FFN_SPLITS = ((0, 1536), (1536, FFN_HIDDEN))
LN_EPS = 1e-5
MASK_NEG = -1e30

LANES = 128
CHUNK = 64
SUB = 16
RWKV_HB = 2
RW_BW = RWKV_HB * HEAD_DIM
QBLK = 128
SLC_KB = 4
VT_ROWS = HEAD_DIM + 16
LOG2E = 1.4426950408889634
VMEM_LIMIT = 56 * 1024 * 1024

GLA_SLAB = 4 * LIN_WIDTH + LANES
RWKV_SLAB = 3 * LIN_WIDTH + LANES + LANES + 2 * LANES + LANES
NSA_SLAB = NSA_WIDTH + 6 * NSA_KV_WIDTH + LANES
RW_WLR, RW_ALR, RW_GLR, RW_VLR = 768, 896, 1024, 1280


def _dot(a, b, dims=None):
    dims = dims or (((a.ndim - 1,), (0,)), ((), ()))
    return lax.dot_general(a.astype(BF16), b.astype(BF16), dims, preferred_element_type=F32)


def _split_bf16(a):
    hi = a.astype(BF16)
    return hi, (a - hi.astype(F32)).astype(BF16)


def _dot3(a, b_hi, b_lo, dims=None):
    dims = dims or (((a.ndim - 1,), (0,)), ((), ()))
    a_hi, a_lo = _split_bf16(a)
    d = functools.partial(lax.dot_general, dimension_numbers=dims, preferred_element_type=F32)
    return d(a_hi, b_hi) + (d(a_lo, b_hi) + d(a_hi, b_lo))


def _dot_exact(a, b):
    if a.dtype == BF16:
        hi, lo = _split_bf16(b)
        return jnp.dot(a, hi, preferred_element_type=F32) + jnp.dot(a, lo, preferred_element_type=F32)
    hi, lo = _split_bf16(a)
    return jnp.dot(hi, b, preferred_element_type=F32) + jnp.dot(lo, b, preferred_element_type=F32)


def _hi_lo(w):
    return jnp.stack(_split_bf16(w))


_NT = (((1,), (1,)), ((), ()))
_TN = (((0,), (0,)), ((), ()))


def _sigmoid(x):
    return 1.0 / (1.0 + jnp.exp(-x))


def _softplus(x):
    return jnp.maximum(x, 0.0) + jnp.log(1.0 + jnp.exp(-jnp.abs(x)))


def _head_norm(x, w, b, eps):
    mu = jnp.mean(x, axis=-1, keepdims=True)
    xc = x - mu
    var = jnp.mean(xc * xc, axis=-1, keepdims=True)
    return xc * lax.rsqrt(var + eps) * w + b


def _layer_norm(x, w, b):
    mu = jnp.mean(x, axis=-1, keepdims=True)
    xc = x - mu
    var = jnp.mean(xc * xc, axis=-1, keepdims=True)
    return xc * lax.rsqrt(var + LN_EPS) * w + b


def _params(*sem):
    return pltpu.CompilerParams(dimension_semantics=sem, vmem_limit_bytes=VMEM_LIMIT)


def _full(shape):
    nd = len(shape)
    return pl.BlockSpec(shape, lambda *_: (0,) * nd)


def _each(f, *lists):
    return [f(*args) for args in zip(*lists)]


def _run_together(*tracers):
    out = [None] * len(tracers)
    live = dict(enumerate(tracers))
    while live:
        for i in list(live):
            try:
                next(live[i])
            except StopIteration as stop:
                out[i] = stop.value
                del live[i]
    return out


def _proj_kernel(x_ref, wg_ref, wr_ref, wn_ref, cos_ref, sin_ref,
                 og_ref, or_ref, q_ref, kvc_ref, kk_ref, vt_ref, gt_ref, *, n_qblk, seq_len):
    xb = x_ref[...].astype(BF16)
    hn = jnp.dot(xb, wn_ref[...], preferred_element_type=F32)
    og_ref[...] = jnp.dot(xb, wg_ref[...], preferred_element_type=F32)

    cos = cos_ref[...]
    sin = sin_ref[...]
    half = HEAD_DIM // 2
    first_half = (lax.broadcasted_iota(jnp.int32, cos.shape, 1) & (HEAD_DIM - 1)) < half

    def rope(x):
        swapped = jnp.where(first_half, pltpu.roll(x, LANES - half, 1), pltpu.roll(x, half, 1))
        return x * cos + swapped * sin

    def col(i):
        return hn[:, NSA_WIDTH + LANES * i:NSA_WIDTH + LANES * (i + 1)]

    for i in range(NSA_WIDTH // LANES):
        cs = slice(LANES * i, LANES * (i + 1))
        q_ref[:, cs] = rope(hn[:, cs]) * (HEAD_DIM ** -0.5)
    or_ref[...] = jnp.dot(xb, wr_ref[...], preferred_element_type=F32)
    kvc_ref[:, 0:LANES] = rope(col(0))
    kvc_ref[:, LANES:2 * LANES] = col(1)
    kk_ref[:, 0:LANES] = rope(col(2)).astype(BF16)
    pos = (pl.program_id(0) * x_ref.shape[0]) % seq_len + lax.broadcasted_iota(jnp.int32, cos.shape, 0)
    kk_ref[:, LANES:2 * LANES] = jnp.where(
        pos // NSA_SEL_BLOCK == lax.broadcasted_iota(jnp.int32, cos.shape, 1), 1.0, 0.0).astype(BF16)
    kk_ref[:, 2 * LANES:3 * LANES] = rope(col(4)).astype(BF16)
    pad = VT_ROWS - HEAD_DIM
    ones_rows = jnp.where(lax.broadcasted_iota(jnp.int32, (pad, QBLK), 0) == 0, 1.0, 0.0).astype(BF16)
    for i in range(n_qblk):
        rs = slice(QBLK * i, QBLK * (i + 1))
        for branch, c in enumerate((3, 5)):
            v_t = col(c)[rs, :].T.astype(BF16)
            for g in range(NSA_KV_HEADS):
                r0 = (branch * NSA_KV_HEADS + g) * VT_ROWS
                vt_ref[i, r0:r0 + HEAD_DIM, :] = v_t[HEAD_DIM * g:HEAD_DIM * (g + 1), :]
                vt_ref[i, r0 + HEAD_DIM:r0 + VT_ROWS, :] = ones_rows
    gt_ref[...] = _sigmoid(hn[:, NSA_WIDTH + 6 * LANES:NSA_SLAB])


def _proj(x2, wg, wr, wn, cos2, sin2, tm=512):
    t = x2.shape[0]
    s = cos2.shape[0]
    n_qblk = tm // QBLK

    def rows(width):
        return pl.BlockSpec((tm, width), lambda i: (i, 0))

    def resident(a):
        return pl.BlockSpec(a.shape, lambda i: (0,) * a.ndim, pipeline_mode=pl.Buffered(1))

    pos_rows = pl.BlockSpec((tm, LANES), lambda i: (i % (s // tm), 0))
    return pl.pallas_call(
        functools.partial(_proj_kernel, n_qblk=n_qblk, seq_len=s),
        grid=(t // tm,),
        in_specs=[rows(D_MODEL), resident(wg), resident(wr), resident(wn), pos_rows, pos_rows],
        out_specs=[rows(GLA_SLAB), rows(RWKV_SLAB), rows(NSA_WIDTH), rows(2 * LANES), rows(3 * LANES),
                   pl.BlockSpec((n_qblk, 4 * VT_ROWS, QBLK), lambda i: (i, 0, 0)), rows(LANES)],
        out_shape=[jax.ShapeDtypeStruct((t, GLA_SLAB), F32),
                   jax.ShapeDtypeStruct((t, RWKV_SLAB), F32),
                   jax.ShapeDtypeStruct((t, NSA_WIDTH), F32),
                   jax.ShapeDtypeStruct((t, 2 * LANES), F32),
                   jax.ShapeDtypeStruct((t, 3 * LANES), BF16),
                   jax.ShapeDtypeStruct((t // QBLK, 4 * VT_ROWS, QBLK), BF16),
                   jax.ShapeDtypeStruct((t, LANES), F32)],
        compiler_params=_params("parallel"),
    )(x2, wg, wr, wn, cos2, sin2)


def _group_rows(x, u):
    return jnp.concatenate(
        [jnp.broadcast_to(x[SUB * i + u:SUB * i + u + 1, :], (SUB, x.shape[1]))
         for i in range(CHUNK // SUB)], axis=0)


def _gla_steps(h_ref, wa2_ref, ba_ref, lnw_ref, lnb_ref, tril_ref, blk_ref, ones_ref, dm_ref, same_ref,
               same_sub_ref, o_ref, st_ref, n_chunks):
    each = _each
    row_in_sub = lax.broadcasted_iota(jnp.int32, (CHUNK, LIN_WIDTH), 0) % SUB
    dm = dm_ref[...]
    heads = range(N_LIN_HEADS)
    n_sub = CHUNK // SUB

    def own_head(m, mask_ref):
        return jnp.concatenate([m] * N_LIN_HEADS, axis=0) * mask_ref[...]

    rows = [pl.ds(j * CHUNK, CHUNK) for j in range(n_chunks)]
    ps = [h_ref[0, rw, :] for rw in rows]
    q = each(lambda p: p[:, 0:256] * (HEAD_DIM ** -0.5), ps)
    k = each(lambda p: p[:, 256:512], ps)
    v = each(lambda p: p[:, 512:768], ps)
    g = each(lambda p: p[:, 768:1024], ps)
    z = each(lambda p: _dot3(p[:, 1024:1152], wa2_ref[0], wa2_ref[1]) + ba_ref[...], ps)
    yield
    log_a = each(lambda x: -_softplus(-x) * (1.0 / GLA_TAU), z)
    yield
    b = each(lambda x: _dot_exact(tril_ref[...], x), log_a)
    yield
    b_end = each(lambda x: _dot_exact(blk_ref[...], x), log_a)
    yield
    qe = each(lambda x, bb: x * jnp.exp(bb), q, b)
    kd = each(lambda x, be, bb: x * jnp.exp(be - bb), k, b_end, b)
    p_end = each(jnp.exp, b_end)
    v_st = each(lambda x: jnp.concatenate([x[:, HEAD_DIM * h:HEAD_DIM * (h + 1)] for h in heads], axis=0), v)
    yield

    b2 = each(lambda x: x * LOG2E, b)
    at = [jnp.zeros((CHUNK, LIN_WIDTH), F32) for _ in ps]
    for u0 in range(0, SUB, N_LIN_HEADS):
        us = range(u0, u0 + N_LIN_HEADS)
        es = each(lambda bb, qq, kk: jnp.concatenate(
            [jnp.exp2(jnp.where(row_in_sub <= u, _group_rows(bb, u) - bb, MASK_NEG)) * (_group_rows(qq, u) * kk)
             for u in us], axis=0), b2, q, k)
        yield
        a_rep = each(lambda e: _dot(e, ones_ref[...]), es)
        yield
        for i, u in enumerate(us):
            at = each(lambda acc, ar: acc + jnp.where(dm == u, ar[CHUNK * i:CHUNK * (i + 1), :], 0.0), at, a_rep)
        yield
    o_st = each(lambda x, vs: _dot(own_head(x, same_ref), vs, _TN), at, v_st)
    yield

    upd, q_own = [], []
    for i in range(n_sub):
        rs = slice(SUB * i, SUB * (i + 1))
        upd.append(each(lambda vs, kk: _dot(
            jnp.concatenate([vs[CHUNK * h + SUB * i:CHUNK * h + SUB * (i + 1), :] for h in heads], axis=0),
            own_head(kk[rs], same_sub_ref), _TN), v_st, kd))
        q_own.append(each(lambda x: own_head(x[rs], same_sub_ref), qe))
        yield
    st = st_ref[...]
    for c, rw in enumerate(rows):
        inter = []
        for i in range(n_sub):
            inter.append(_dot(q_own[i][c], st, _NT))
            st = st * p_end[c][SUB * i:SUB * i + 1, :] + upd[i][c]
        o_c = o_st[c] + jnp.concatenate(
            [inter[i][SUB * h:SUB * (h + 1), :] for h in heads for i in range(n_sub)], axis=0)
        o_c = _head_norm(o_c, 1.0, 0.0, LN_EPS)
        o = jnp.concatenate([o_c[CHUNK * h:CHUNK * (h + 1), :] for h in heads], axis=1)
        o_ref[0, rw, :] = (o * lnw_ref[...] + lnb_ref[...]) * (g[c] * _sigmoid(g[c]))
        yield
    st_ref[...] = st


def _rwkv_steps(h_ref, vf_ref, mu_ref, w0_ref, w2_ref, a0_ref, a2_ref, g2_ref, kk_ref, ka_ref, rk_ref,
                lnw_ref, lnb_ref, v0_ref, v2_ref, tril_ref, ones_ref, same_ref, strict_ref, incl_ref, eye_ref,
                o_ref, st_ref, prev_ref, n_chunks, has_vres):
    each = _each
    row = lax.broadcasted_iota(jnp.int32, (CHUNK, RWKV_SLAB), 0)
    blocks = range(N_LIN_HEADS // RWKV_HB)
    in_block = range(RWKV_HB)
    nr = RWKV_HB * CHUNK

    def tile_rows(m):
        return jnp.concatenate([m] * RWKV_HB, axis=0)

    def own_head(m):
        return tile_rows(m) * same_ref[...]

    def stack(m):
        return jnp.concatenate([m[:, HEAD_DIM * h:HEAD_DIM * (h + 1)] for h in in_block], axis=0)

    def unstack(m):
        return jnp.concatenate([m[CHUNK * h:CHUNK * (h + 1), :] for h in in_block], axis=1)

    def per_block(xs):
        return [x[:, RW_BW * bi:RW_BW * (bi + 1)] for x in xs for bi in blocks]

    rows = [pl.ds(j * CHUNK, CHUNK) for j in range(n_chunks)]
    ps = [h_ref[0, rw, :] for rw in rows]
    vfs = [vf_ref[0, rw, :] if has_vres else None for rw in rows]
    prev_rows = [prev_ref[...]] + [p[CHUNK - 1:CHUNK, :] for p in ps[:-1]]

    prev = each(lambda p, pr: jnp.where(row == 0, pr, pltpu.roll(p, 1, 0)), ps, prev_rows)
    xm = each(lambda p, pv: p + (pv - p) * mu_ref[...], ps, prev)
    yield
    r = each(lambda x: x[:, 0:256], xm)
    k = each(lambda x: x[:, 256:512], xm)
    v = each(lambda x: x[:, 512:768], xm)
    v_shift = v
    w_log = each(lambda x: -_softplus(-(w0_ref[...] + _dot3(jnp.tanh(x[:, RW_WLR:RW_ALR]), w2_ref[0], w2_ref[1])))
                 - 0.5, xm)
    yield
    lw = each(lambda w: -jnp.exp(w), w_log)
    a = each(lambda x: _sigmoid(a0_ref[...] + _dot3(x[:, RW_ALR:RW_GLR], a2_ref[0], a2_ref[1])), xm)
    yield
    g = each(lambda x: _dot3(_sigmoid(x[:, RW_GLR:RW_VLR]), g2_ref[0], g2_ref[1]), xm)
    yield
    if has_vres:
        mixv = each(lambda x: _sigmoid(v0_ref[...] + _dot3(x[:, RW_VLR:RWKV_SLAB], v2_ref[0], v2_ref[1])), xm)
        v = each(lambda vv, vf, mx: vv + (vf - vv) * mx, v, vfs, mixv)
        yield
    kkv = each(lambda kk: kk * kk_ref[...], k)
    nrm = each(lambda kv: jnp.sqrt(_dot_exact(kv * kv, ones_ref[...])), kkv)
    yield
    kkn = each(lambda kv, n: kv / jnp.maximum(n, 1e-12), kkv, nrm)
    k2 = each(lambda kk, aa: kk * (1.0 + (aa - 1.0) * ka_ref[...]), k, a)
    bv = each(lambda kn, aa: kn * aa, kkn, a)
    c = each(lambda w: _dot_exact(tril_ref[...], w), lw)
    yield
    c_end = each(lambda cc: cc[CHUNK - 1:CHUNK, :], c)
    bonus = each(lambda rr, kk, vv: _dot_exact(rr * kk * rk_ref[...], ones_ref[...]) * vv, r, k2, v)
    yield
    at = each(lambda kn, cc, w: -kn * jnp.exp(cc - w), kkn, c, lw)
    rt = each(lambda rr, cc: rr * jnp.exp(cc), r, c)
    yield
    e_nc = each(lambda cc: jnp.exp(-cc), c)
    bt = each(lambda b_, e: b_ * e, bv, e_nc)
    kt = each(lambda kk, e: kk * e, k2, e_nc)
    yield
    e_rest = each(lambda ce, cc: jnp.exp(ce - cc), c_end, c)
    b_rest = each(lambda b_, e: b_ * e, bv, e_rest)
    k_rest = each(lambda kk, e: kk * e, k2, e_rest)
    p_end = per_block(each(jnp.exp, c_end))
    yield

    ar = each(lambda x, y: jnp.concatenate([own_head(x), own_head(y)], axis=0), per_block(at), per_block(rt))
    yield
    bk = each(lambda x, y: jnp.concatenate([tile_rows(x), tile_rows(y)], axis=0), per_block(bt), per_block(kt))
    yield
    gm = each(lambda x, y: _dot(x, y, _NT), ar, bk)
    yield
    n_ab = each(lambda m: m[0:nr, 0:nr] * strict_ref[...], gm)
    m_ak = each(lambda m: m[0:nr, nr:2 * nr] * strict_ref[...], gm)
    r_bk = each(lambda m: m[nr:2 * nr, :] * jnp.concatenate([incl_ref[...], incl_ref[...]], axis=1), gm)
    yield
    v_st = each(stack, per_block(v))
    mv = each(_dot, m_ak, v_st)
    yield
    inv = each(lambda n: eye_ref[...] + n, n_ab)
    pw = n_ab
    for _ in range(5):
        pw = each(lambda x: _dot(x, x), pw)
        yield
        inv = each(lambda i_, x: i_ + _dot(i_, x), inv, pw)
        yield
    bk_rest = each(lambda x, y: jnp.concatenate([own_head(x), own_head(y)], axis=0),
                   per_block(b_rest), per_block(k_rest))
    yield
    a_bar = each(lambda i_, x: _dot(i_, x[0:nr]), inv, ar)
    yield
    uv0 = each(lambda i_, m, vs: jnp.concatenate([_dot(i_, m), vs], axis=0), inv, mv, v_st)
    yield
    r_bar = each(lambda x, rb, ab: x[nr:2 * nr] + _dot(rb[:, 0:nr], ab), ar, r_bk, a_bar)
    yield
    y0 = each(_dot, r_bk, uv0)
    yield
    g_mat = each(lambda ab, br: _dot(ab, br[0:nr], _TN), a_bar, bk_rest)
    yield
    sc = each(lambda z, br: _dot(z, br, _TN), uv0, bk_rest)
    yield

    n_blk = len(blocks)
    s = per_block([st_ref[...]])
    for j, rw in enumerate(rows):
        it = slice(n_blk * j, n_blk * (j + 1))
        y = each(lambda rb, s_, y_: unstack(_head_norm(_dot(rb, s_, _NT) + y_, 1.0, 0.0, RWKV_GN_EPS)),
                 r_bar[it], s, y0[it])
        s = each(lambda s_, pe, gm_, c_: s_ * pe + _dot(s_, gm_) + c_, s, p_end[it], g_mat[it], sc[it])
        o_ref[0, rw, :] = (jnp.concatenate(y, axis=1) * lnw_ref[...] + lnb_ref[...] + bonus[j]) * g[j]
        if not has_vres:
            vf_ref[0, rw, :] = v_shift[j]
        yield
    prev_ref[...] = ps[-1][CHUNK - 1:CHUNK, :]
    st_ref[...] = jnp.concatenate(s, axis=1)


N_GLA_IN, N_RWKV_IN = 11, 18


def _linmix_kernel(*refs, n_chunks, has_vres):
    refs = list(refs)
    gla_in = refs[:N_GLA_IN]
    n_rw = N_RWKV_IN + (3 if has_vres else 0)
    rwkv_in = refs[N_GLA_IN:N_GLA_IN + n_rw]
    outs = refs[N_GLA_IN + n_rw:]
    if has_vres:
        (h_ref, vf_ref, mu, w0, w2, a0, a2, g2, kk, ka, rk, lnw, lnb, v0, v2, *consts) = rwkv_in
        o_gla_ref, o_rwkv_ref, gla_st, rwkv_st, prev_ref = outs
    else:
        (h_ref, mu, w0, w2, a0, a2, g2, kk, ka, rk, lnw, lnb, *consts) = rwkv_in
        v0 = v2 = None
        o_gla_ref, o_rwkv_ref, vf_ref, gla_st, rwkv_st, prev_ref = outs

    @pl.when(pl.program_id(1) == 0)
    def _():
        gla_st[...] = jnp.zeros_like(gla_st)
        rwkv_st[...] = jnp.zeros_like(rwkv_st)
        prev_ref[...] = jnp.zeros_like(prev_ref)

    _run_together(
        _gla_steps(*gla_in, o_gla_ref, gla_st, n_chunks),
        _rwkv_steps(h_ref, vf_ref, mu, w0, w2, a0, a2, g2, kk, ka, rk, lnw, lnb, v0, v2, *consts,
                    o_rwkv_ref, rwkv_st, prev_ref, n_chunks, has_vres))


def _linmix(hg, hr, v_first, gla_small, rwkv_small, ts=512):
    bsz, s, _ = hg.shape
    has_vres = v_first is not None
    r = np.arange(CHUNK)
    c = np.arange(LIN_WIDTH)
    ones = (c[:, None] // HEAD_DIM == c[None, :] // HEAD_DIM).astype(np.float32)
    tril16 = ((r[:, None] // SUB == r[None, :] // SUB) & (r[None, :] <= r[:, None])).astype(np.float32)
    blk16 = (r[:, None] // SUB == r[None, :] // SUB).astype(np.float32)
    same_sub = (r[:, None] // SUB == c[None, :] // HEAD_DIM).astype(np.float32)
    dm = ((c[None, :] % HEAD_DIM) - SUB * (r[:, None] // SUB)).astype(np.int32)
    gla_consts = [jnp.asarray(tril16, BF16), jnp.asarray(blk16, BF16), jnp.asarray(ones, BF16), jnp.asarray(dm),
                  jnp.asarray(ones), jnp.asarray(same_sub)]
    tril = (r[None, :] <= r[:, None]).astype(np.float32)
    i = np.arange(RW_BW)
    same = (i[:, None] // HEAD_DIM == i[None, :] // HEAD_DIM).astype(np.float32)
    t_in, s_in = i[:, None] % CHUNK, i[None, :] % CHUNK
    rwkv_consts = [jnp.asarray(tril, BF16), jnp.asarray(ones, BF16), jnp.asarray(same),
                   jnp.asarray(same * (s_in < t_in)), jnp.asarray(same * (s_in <= t_in)),
                   jnp.asarray(np.eye(RW_BW, dtype=np.float32))]

    def seq(width):
        return pl.BlockSpec((1, ts, width), lambda b, j: (b, j, 0))

    gla_args = [hg] + list(gla_small) + gla_consts
    rwkv_args = [hr] + ([v_first] if has_vres else []) + list(rwkv_small) + rwkv_consts
    assert len(gla_args) == N_GLA_IN and len(rwkv_args) == N_RWKV_IN + (3 if has_vres else 0)
    in_specs = ([seq(GLA_SLAB)] + [_full(a.shape) for a in gla_args[1:]]
                + [seq(RWKV_SLAB)] + ([seq(LIN_WIDTH)] if has_vres else [])
                + [_full(a.shape) for a in rwkv_args[(2 if has_vres else 1):]])
    n_out = 2 if has_vres else 3
    out_sd = jax.ShapeDtypeStruct((bsz, s, LIN_WIDTH), F32)
    return pl.pallas_call(
        functools.partial(_linmix_kernel, n_chunks=ts // CHUNK, has_vres=has_vres),
        grid=(bsz, s // ts),
        in_specs=in_specs,
        out_specs=[seq(LIN_WIDTH)] * n_out,
        out_shape=[out_sd] * n_out,
        scratch_shapes=[pltpu.VMEM((HEAD_DIM, LIN_WIDTH), F32),
                        pltpu.VMEM((HEAD_DIM, LIN_WIDTH), F32),
                        pltpu.VMEM((1, RWKV_SLAB), F32)],
        compiler_params=_params("parallel", "arbitrary"),
    )(*gla_args, *rwkv_args)


def _nsa_cmp_kernel(kc_ref, vc_ref, pk_ref, pv_ref, wk1h_ref, wk1l_ref, wk2h_ref, wk2l_ref,
                    wv1h_ref, wv1l_ref, wv2h_ref, wv2l_ref, ko_ref, vo_ref, *, n_blk):
    half = NSA_CMP_BLOCK // 2
    k_half = half * HEAD_DIM
    lo_lanes = lax.broadcasted_iota(jnp.int32, (n_blk, LANES), 1) < HEAD_DIM

    def compress(t_ref, pos_ref, w1h_ref, w1l_ref, w2h_ref, w2l_ref, o_ref):
        flat = [[], []]
        for i in range(half // 2):
            a = t_ref[0, pl.ds(2 * i, n_blk, stride=NSA_CMP_STRIDE), :]
            b = t_ref[0, pl.ds(2 * i + 1, n_blk, stride=NSA_CMP_STRIDE), :]
            flat[0].append(jnp.where(lo_lanes, a, pltpu.roll(b, HEAD_DIM, 1)))
            flat[1].append(jnp.where(lo_lanes, pltpu.roll(a, HEAD_DIM, 1), b))
        for g in range(NSA_KV_HEADS):
            gs = slice(HEAD_DIM * g, HEAD_DIM * (g + 1))
            top = jnp.concatenate([x + pos_ref[i:i + 1, :] for i, x in enumerate(flat[g])], axis=1)
            bot = jnp.concatenate([x + pos_ref[half // 2 + i:half // 2 + i + 1, :] for i, x in enumerate(flat[g])],
                                  axis=1)
            top = _dot3(top, w1h_ref[0:k_half, :], w1l_ref[0:k_half, :])
            bot = _dot3(bot, w1h_ref[k_half:, :], w1l_ref[k_half:, :])
            hid = top + pltpu.roll(bot, n_blk - 1, 0)
            act = 0.5 * hid * (1.0 + jnp.tanh(0.7978845608028654 * (hid + 0.044715 * hid * hid * hid)))
            o_ref[0, :, gs] = _dot3(act, w2h_ref[...], w2l_ref[...])

    compress(kc_ref, pk_ref, wk1h_ref, wk1l_ref, wk2h_ref, wk2l_ref, ko_ref)
    compress(vc_ref, pv_ref, wv1h_ref, wv1l_ref, wv2h_ref, wv2l_ref, vo_ref)


def _nsa_cmp(kv, pos_k, pos_v, wk1, wk2, wv1, wv2):
    bsz, s, _ = kv.shape
    n_blk = s // NSA_CMP_STRIDE
    pos_k = pos_k.reshape(NSA_CMP_BLOCK // 2, 2 * HEAD_DIM)
    pos_v = pos_v.reshape(NSA_CMP_BLOCK // 2, 2 * HEAD_DIM)
    small = [pos_k, pos_v, *_split_bf16(wk1), *_split_bf16(wk2), *_split_bf16(wv1), *_split_bf16(wv2)]
    out_spec = pl.BlockSpec((1, n_blk, LANES), lambda b: (b, 0, 0))
    out_sd = jax.ShapeDtypeStruct((bsz, n_blk, LANES), F32)
    return pl.pallas_call(
        functools.partial(_nsa_cmp_kernel, n_blk=n_blk),
        grid=(bsz,),
        in_specs=[pl.BlockSpec((1, s, LANES), lambda b: (b, 0, 0)),
                  pl.BlockSpec((1, s, LANES), lambda b: (b, 0, 1))] + [_full(a.shape) for a in small],
        out_specs=[out_spec, out_spec],
        out_shape=[out_sd, out_sd],
        compiler_params=_params("parallel"),
    )(kv, kv, *small)


def _nsa_attn_kernel(q_ref, gt_ref, kc_ref, vc_ref, ks_ref, kw_ref, vt_ref, ovt_ref,
                     o_ref, acc_ref, s_ref, cm_ref, m_ref, *, n_cmp):
    qi = pl.program_id(1)
    cols = NSA_HPG * QBLK
    n_sel_blk = ovt_ref.shape[0]
    key_off = lax.broadcasted_iota(jnp.int32, (QBLK, cols), 0)
    t_pos = qi * QBLK + (lax.broadcasted_iota(jnp.int32, (QBLK, cols), 1) & (QBLK - 1))
    gt_t = gt_ref[0].T
    zeros64 = jnp.zeros((QBLK, HEAD_DIM), F32)
    m0 = jnp.full((1, cols), MASK_NEG, F32)

    groups = list(range(NSA_KV_HEADS))

    each, run_together = _each, _run_together

    def stacked_q(g):
        pieces = []
        for hp in range(NSA_HPG):
            qh = q_ref[0, :, (g * NSA_HPG + hp) * HEAD_DIM:(g * NSA_HPG + hp + 1) * HEAD_DIM]
            pieces.append(jnp.concatenate([qh, zeros64] if g == 0 else [zeros64, qh], axis=1))
        return jnp.concatenate(pieces, axis=0)

    qs = each(stacked_q, groups)
    qs_b = each(lambda q: (q * LOG2E).astype(BF16), qs)

    def select_steps():
        valid_c = (key_off * NSA_CMP_STRIDE + (NSA_CMP_BLOCK - 1) <= t_pos) & (key_off < n_cmp)
        s_c = each(lambda q: jnp.where(valid_c, _dot3(kc_ref[0], *_split_bf16(q), _NT), MASK_NEG), qs)
        yield
        e_c = each(lambda x: jnp.exp(x - jnp.max(x, axis=0, keepdims=True)), s_c)
        yield
        p_c = each(lambda e: jnp.where(valid_c, e / jnp.sum(e, axis=0, keepdims=True), 0.0), e_c)
        yield
        o_cmp = each(lambda p, g: _dot(vc_ref[0], p, _TN)[HEAD_DIM * g:HEAD_DIM * (g + 1), :], p_c, groups)
        yield
        p_sum = each(lambda p: p[:, 0:QBLK] + p[:, QBLK:2 * QBLK] + p[:, 2 * QBLK:3 * QBLK] + p[:, 3 * QBLK:], p_c)
        imp = each(lambda p: _dot_exact(ovt_ref[...], p), p_sum)
        yield
        j = lax.broadcasted_iota(jnp.int32, (n_sel_blk, QBLK), 0)
        tq = qi * QBLK + lax.broadcasted_iota(jnp.int32, (n_sel_blk, QBLK), 1)
        cur = tq // NSA_SEL_BLOCK
        forced = (j == 0) | (j == cur) | (j == cur - 1)
        score = each(lambda x: jnp.where(forced, 1e9, jnp.where(j > cur, -1e9, x)), imp)
        rank = [jnp.zeros((n_sel_blk, QBLK), F32) for _ in groups]
        for i in range(n_sel_blk):
            rank = each(lambda rk, sc: rk + jnp.where(
                (sc[i:i + 1, :] > sc) | ((sc[i:i + 1, :] == sc) & (j > i)), 1.0, 0.0), rank, score)
            if i % 4 == 3:
                yield
        return o_cmp, each(lambda rk: jnp.where(rk < NSA_N_SELECT, 1.0, 0.0), rank)

    def key_scores(k_ref, q, kb0, n_kb, mask):
        k0 = pl.multiple_of(kb0 * QBLK, QBLK)
        s = lax.dot_general(k_ref[0, pl.ds(k0, n_kb * QBLK), :], q, _NT, preferred_element_type=F32)
        if mask is not None:
            key_pos = k0 + lax.broadcasted_iota(jnp.int32, (n_kb * QBLK, cols), 0)
            t_all = jnp.concatenate([t_pos] * n_kb, axis=0)
            valid = key_pos <= t_all
            if mask == "window":
                valid = valid & (key_pos > t_all - NSA_WINDOW)
            s = jnp.where(valid, s, MASK_NEG)
        return s

    def softmax_steps(slabs, kb0, n_kb, block_scores, col_max, m_old):
        m_new = each(jnp.maximum, m_old, col_max)
        pv = [jnp.zeros((VT_ROWS, cols), F32) for _ in slabs]
        for i in range(n_kb):
            pv = [acc + jnp.dot(vt_ref[0, kb0 + i, slab * VT_ROWS:(slab + 1) * VT_ROWS, :],
                                jnp.exp2(rd(i) - mn).astype(BF16), preferred_element_type=F32)
                  for acc, rd, mn, slab in zip(pv, block_scores, m_new, slabs)]
            yield
        for slab, m, mn, x in zip(slabs, m_old, m_new, pv):
            acc_ref[slab] = acc_ref[slab] * jnp.exp2(m - mn) + x
        return m_new

    def window_steps():
        n_win = NSA_WINDOW // QBLK + 1
        kb_win = jnp.maximum(qi - (n_win - 1), 0)
        s = []
        for g in groups:
            s.append(key_scores(kw_ref, qs_b[g], kb_win, n_win, "window"))
            yield
        col_max = each(lambda x: jnp.max(x, axis=0, keepdims=True), s)
        yield
        blocks = [lambda i, x=x: x[QBLK * i:QBLK * (i + 1), :] for x in s]
        yield from softmax_steps([NSA_KV_HEADS + g for g in groups], kb_win, n_win, blocks, col_max, [m0] * len(s))

    def normalised(slab):
        return acc_ref[slab, 0:HEAD_DIM, :] / acc_ref[slab, HEAD_DIM:HEAD_DIM + 1, :]

    acc_ref[...] = jnp.zeros_like(acc_ref)
    for g in groups:
        m_ref[g] = m0

    (o_cmp, sel), _ = run_together(select_steps(), window_steps())

    def with_selection(q, sl):
        off = jnp.where(sl > 0.5, 0.0, MASK_NEG).T.astype(BF16)
        pad = jnp.zeros((cols, LANES - n_sel_blk), BF16)
        return jnp.concatenate([q, jnp.concatenate([off] * NSA_HPG, axis=0), pad], axis=1)

    qs_sel = each(with_selection, qs_b, sel)

    def score_steps(i, mask):
        s = []
        for g in groups:
            s.append(key_scores(ks_ref, qs_sel[g], i * SLC_KB, SLC_KB, mask))
            yield
        return s, each(lambda x: jnp.max(x, axis=0, keepdims=True), s)

    def park(scores_and_max):
        for g, (x, cm) in enumerate(zip(*scores_and_max)):
            s_ref[g] = x
            cm_ref[g] = cm

    def consume_steps(i):
        blocks = [lambda b, g=g: s_ref[g, QBLK * b:QBLK * (b + 1), :] for g in groups]
        m_new = yield from softmax_steps(groups, i * SLC_KB, SLC_KB, blocks, [cm_ref[g] for g in groups],
                                         [m_ref[g] for g in groups])
        for g in groups:
            m_ref[g] = m_new[g]

    n_full = qi // SLC_KB

    @pl.when(n_full >= 1)
    def _():
        park(run_together(score_steps(0, None))[0])

    def pipelined(i, carry):
        nxt, _ = run_together(score_steps(i + 1, None), consume_steps(i))
        park(nxt)
        return carry

    lax.fori_loop(0, n_full - 1, pipelined, 0)

    @pl.when(n_full >= 1)
    def _():
        nxt, _ = run_together(score_steps(n_full, "causal"), consume_steps(n_full - 1))
        park(nxt)

    @pl.when(n_full == 0)
    def _():
        park(run_together(score_steps(0, "causal"))[0])

    run_together(consume_steps(n_full))

    o_groups = []
    for g in groups:
        def gate(branch):
            return jnp.concatenate(
                [gt_t[(g * NSA_HPG + hp) * 3 + branch:(g * NSA_HPG + hp) * 3 + branch + 1, :]
                 for hp in range(NSA_HPG)], axis=1)

        o_groups.append(gate(0) * o_cmp[g] + gate(1) * normalised(g) + gate(2) * normalised(NSA_KV_HEADS + g))

    o_all = jnp.concatenate(o_groups, axis=0)
    for hp in range(NSA_HPG):
        tile = o_all[:, QBLK * hp:QBLK * (hp + 1)].T
        for g in range(NSA_KV_HEADS):
            c0 = (g * NSA_HPG + hp) * HEAD_DIM
            o_ref[0, :, c0:c0 + HEAD_DIM] = tile[:, HEAD_DIM * g:HEAD_DIM * (g + 1)]


def _nsa_attn(qr, kk, vt, gt, kcmp, vcmp):
    bsz, s, _ = qr.shape
    n_blk = s // NSA_CMP_STRIDE
    n_cmp = n_blk - NSA_CMP_BLOCK // NSA_CMP_STRIDE + 1
    n_sel = s // NSA_SEL_BLOCK
    assert n_blk == QBLK, "compressed-score tiles are laid out one stride block per sublane row"
    c0 = np.arange(n_blk) * NSA_CMP_STRIDE
    s0 = np.arange(n_sel) * NSA_SEL_BLOCK
    lo = np.maximum(c0[:, None], s0[None, :])
    hi = np.minimum(c0[:, None] + NSA_CMP_BLOCK, s0[None, :] + NSA_SEL_BLOCK)
    overlap = (np.maximum(hi - lo, 0) / NSA_CMP_STRIDE).astype(np.float32)
    overlap[n_cmp:] = 0.0
    consts = [jnp.asarray(overlap.T, BF16)]

    cmp_spec = pl.BlockSpec((1, n_blk, LANES), lambda b, i: (b, 0, 0))
    return pl.pallas_call(
        functools.partial(_nsa_attn_kernel, n_cmp=n_cmp),
        grid=(bsz, s // QBLK),
        in_specs=[pl.BlockSpec((1, QBLK, NSA_WIDTH), lambda b, i: (b, i, 0)),
                  pl.BlockSpec((1, QBLK, LANES), lambda b, i: (b, i, 0)),
                  cmp_spec, cmp_spec,
                  pl.BlockSpec((1, s, 2 * LANES), lambda b, i: (b, 0, 0)),
                  pl.BlockSpec((1, s, LANES), lambda b, i: (b, 0, 2)),
                  pl.BlockSpec((1, s // QBLK, 4 * VT_ROWS, QBLK), lambda b, i: (b, 0, 0, 0))]
        + [_full(a.shape) for a in consts],
        out_specs=pl.BlockSpec((1, QBLK, NSA_WIDTH), lambda b, i: (b, i, 0)),
        out_shape=jax.ShapeDtypeStruct((bsz, s, NSA_WIDTH), F32),
        scratch_shapes=[pltpu.VMEM((2 * NSA_KV_HEADS, VT_ROWS, NSA_HPG * QBLK), F32),
                        pltpu.VMEM((NSA_KV_HEADS, SLC_KB * QBLK, NSA_HPG * QBLK), F32),
                        pltpu.VMEM((NSA_KV_HEADS, 1, NSA_HPG * QBLK), F32),
                        pltpu.VMEM((NSA_KV_HEADS, 1, NSA_HPG * QBLK), F32)],
        compiler_params=_params("parallel", "arbitrary"),
    )(qr, gt, kcmp, vcmp, kk, kk, vt, *consts)


def _mix_ffn_kernel(x_ref, og_ref, or_ref, on_ref, wo_ref, l1w_ref, l1b_ref, wg_ref, wu_ref, wd_ref,
                    l2w_ref, l2b_ref, o_ref, *, alpha):
    mix = (_dot(og_ref[...], wo_ref[0:LIN_WIDTH, :])
           + _dot(or_ref[...], wo_ref[LIN_WIDTH:2 * LIN_WIDTH, :])
           + _dot(on_ref[...], wo_ref[2 * LIN_WIDTH:, :]))
    x1 = _layer_norm(alpha * x_ref[...] + mix, l1w_ref[...], l1b_ref[...])
    xb = x1.astype(BF16)
    ffn = jnp.zeros_like(x1)
    for c0, c1 in FFN_SPLITS:
        gate = jnp.dot(xb, wg_ref[:, c0:c1], preferred_element_type=F32)
        up = jnp.dot(xb, wu_ref[:, c0:c1], preferred_element_type=F32)
        ffn = ffn + _dot(gate * _sigmoid(gate) * up, wd_ref[c0:c1, :])
    o_ref[...] = _layer_norm(alpha * x1 + ffn, l2w_ref[...], l2b_ref[...])


def _mix_ffn(x2, og, orw, on, wo, l1w, l1b, wg, wu, wd, l2w, l2b, alpha, tm=512):
    t = x2.shape[0]

    def rows(width):
        return pl.BlockSpec((tm, width), lambda i: (i, 0))

    def resident(a):
        return pl.BlockSpec(a.shape, lambda i: (0,) * a.ndim, pipeline_mode=pl.Buffered(1))

    consts = [wo, l1w, l1b, wg, wu, wd, l2w, l2b]
    return pl.pallas_call(
        functools.partial(_mix_ffn_kernel, alpha=alpha),
        grid=(t // tm,),
        in_specs=[rows(D_MODEL), rows(LIN_WIDTH), rows(LIN_WIDTH), rows(NSA_WIDTH)] + [resident(a) for a in consts],
        out_specs=rows(D_MODEL),
        out_shape=jax.ShapeDtypeStruct((t, D_MODEL), F32),
        compiler_params=_params("parallel"),
    )(x2, og, orw, on, *consts)


def _pad_cols(w, width):
    return jnp.pad(w, ((0, 0), (0, width - w.shape[1])))


def _pad_rows(w, height):
    return jnp.pad(w, ((0, height - w.shape[0]), (0, 0)))


def _rwkv_slab_cols(w, w_vres):
    o = 3 * LIN_WIDTH
    parts = [w[:, :o],
             _pad_cols(w[:, o:o + RWKV_DECAY_RANK], LANES),
             _pad_cols(w[:, o + RWKV_DECAY_RANK:o + RWKV_DECAY_RANK + RWKV_ICLR_RANK], LANES),
             _pad_cols(w[:, o + RWKV_DECAY_RANK + RWKV_ICLR_RANK:], 2 * LANES),
             _pad_cols(w_vres, LANES)]
    return jnp.concatenate(parts, axis=1)


def _rope_tables(s):
    half = HEAD_DIM // 2
    inv = ROPE_THETA ** (-jnp.arange(half, dtype=F32) / half)
    ang = jnp.arange(s, dtype=jnp.int32).astype(F32)[:, None] * inv
    cos, sin = jnp.cos(ang), jnp.sin(ang)
    cos2 = jnp.tile(jnp.concatenate([cos, cos], axis=1), (1, LANES // HEAD_DIM))
    sin2 = jnp.tile(jnp.concatenate([-sin, sin], axis=1), (1, LANES // HEAD_DIM))
    return cos2, sin2


def kernel(x, w_in, w_in_vres, gla_w_a2, gla_b_a, gla_ln_w, gla_ln_b, rwkv_mu, rwkv_mu_vres, rwkv_w0, rwkv_w2, rwkv_a0, rwkv_a2, rwkv_v0, rwkv_v2, rwkv_g2, rwkv_k_k, rwkv_k_a, rwkv_r_k, rwkv_ln_w, rwkv_ln_b, nsa_pos_k, nsa_pos_v, nsa_wk1, nsa_wk2, nsa_wv1, nsa_wv2, w_out, ln1_w, ln1_b, ffn_w_gate, ffn_w_up, ffn_w_down, ln2_w, ln2_b):
    bsz, s, d = x.shape
    depth = w_in.shape[0]
    alpha = float((2 * depth) ** 0.25)
    gla_cols = 4 * LIN_WIDTH + GLA_GATE_RANK
    rwkv_cols = 3 * LIN_WIDTH + RWKV_DECAY_RANK + RWKV_ICLR_RANK + RWKV_GATE_RANK
    cos2, sin2 = _rope_tables(s)
    row = lambda a: a.reshape(1, -1)

    x2 = x.reshape(bsz * s, d)
    v_first = None
    for l in range(depth):
        w = w_in[l]
        w_r = w[:, gla_cols:gla_cols + rwkv_cols]
        if l == 0:
            w_vres = jnp.zeros((d, RWKV_VRES_RANK), F32)
            mu_vres = jnp.zeros((1, RWKV_VRES_RANK), F32)
        else:
            w_vres = w_in_vres[l - 1]
            mu_vres = row(rwkv_mu_vres[l - 1])
        wg = _pad_cols(w[:, :gla_cols], GLA_SLAB).astype(BF16)
        wr = _rwkv_slab_cols(w_r, w_vres).astype(BF16)
        wn = _pad_cols(w[:, gla_cols + rwkv_cols:], NSA_SLAB).astype(BF16)
        hg, hr, qr, kvc, kk, vt, gt = _proj(x2, wg, wr, wn, cos2, sin2)
        hg = hg.reshape(bsz, s, GLA_SLAB)
        hr = hr.reshape(bsz, s, RWKV_SLAB)
        qr = qr.reshape(bsz, s, NSA_WIDTH)
        kvc = kvc.reshape(bsz, s, 2 * LANES)
        kk = kk.reshape(bsz, s, 3 * LANES)
        vt = vt.reshape(bsz, s // QBLK, 4 * VT_ROWS, QBLK)
        gt = gt.reshape(bsz, s, LANES)

        gla_small = [_hi_lo(_pad_rows(gla_w_a2[l], LANES)), row(gla_b_a[l]), row(gla_ln_w[l]), row(gla_ln_b[l])]
        mu = _rwkv_slab_cols(row(rwkv_mu[l]), mu_vres)
        rwkv_small = [mu, row(rwkv_w0[l]), _hi_lo(_pad_rows(rwkv_w2[l], LANES)), row(rwkv_a0[l]),
                      _hi_lo(_pad_rows(rwkv_a2[l], LANES)), _hi_lo(_pad_rows(rwkv_g2[l], 2 * LANES)),
                      row(rwkv_k_k[l]), row(rwkv_k_a[l]), row(rwkv_r_k[l]), row(rwkv_ln_w[l]), row(rwkv_ln_b[l])]
        if l == 0:
            o_gla, o_rwkv, v_first = _linmix(hg, hr, None, gla_small, rwkv_small)
        else:
            rwkv_small += [row(rwkv_v0[l - 1]), _hi_lo(_pad_rows(rwkv_v2[l - 1], LANES))]
            o_gla, o_rwkv = _linmix(hg, hr, v_first, gla_small, rwkv_small)

        kcmp, vcmp = _nsa_cmp(kvc, nsa_pos_k[l], nsa_pos_v[l], nsa_wk1[l], nsa_wk2[l], nsa_wv1[l], nsa_wv2[l])
        o_nsa = _nsa_attn(qr, kk, vt, gt, kcmp, vcmp)

        x2 = _mix_ffn(x2, o_gla.reshape(bsz * s, LIN_WIDTH), o_rwkv.reshape(bsz * s, LIN_WIDTH),
                      o_nsa.reshape(bsz * s, NSA_WIDTH), w_out[l].astype(BF16), row(ln1_w[l]), row(ln1_b[l]),
                      ffn_w_gate[l].astype(BF16), ffn_w_up[l].astype(BF16), ffn_w_down[l].astype(BF16),
                      row(ln2_w[l]), row(ln2_b[l]), alpha)
    return x2.reshape(bsz, s, d)
```

```python
import functools

import numpy as np
import jax
import jax.numpy as jnp
from jax import lax
from jax.experimental import pallas as pl
from jax.experimental.pallas import tpu as pltpu

F32 = jnp.float32
BF16 = jnp.bfloat16

D_MODEL = 1024
HEAD_DIM = 64
N_LIN_HEADS = 4
LIN_WIDTH = N_LIN_HEADS * HEAD_DIM
GLA_GATE_RANK = 16
GLA_TAU = 16.0
RWKV_DECAY_RANK = 64
RWKV_ICLR_RANK = 64
RWKV_VRES_RANK = 32
RWKV_GATE_RANK = 160
RWKV_GN_EPS = 64e-5
NSA_HEADS = 8
NSA_KV_HEADS = 2
NSA_HPG = NSA_HEADS // NSA_KV_HEADS
NSA_WIDTH = NSA_HEADS * HEAD_DIM
NSA_KV_WIDTH = NSA_KV_HEADS * HEAD_DIM
NSA_CMP_BLOCK = 32
NSA_CMP_STRIDE = 16
NSA_CMP_HIDDEN = 256
NSA_SEL_BLOCK = 64
NSA_N_SELECT = 16
NSA_WINDOW = 512
ROPE_THETA = 10000.0
FFN_HIDDEN = 2816
FFN_SPLITS = ((0, 1536), (1536, FFN_HIDDEN))
LN_EPS = 1e-5
MASK_NEG = -1e30

LANES = 128
CHUNK = 64
SUB = 8
RWKV_HB = 2
RW_BW = RWKV_HB * HEAD_DIM
QBLK = 128
SLC_KB = 4
VT_ROWS = HEAD_DIM + 16
LOG2E = 1.4426950408889634
VMEM_LIMIT = 56 * 1024 * 1024

GLA_SLAB = 4 * LIN_WIDTH + LANES
RWKV_SLAB = 3 * LIN_WIDTH + LANES + LANES + 2 * LANES + LANES
NSA_SLAB = NSA_WIDTH + 6 * NSA_KV_WIDTH + LANES
RW_WLR, RW_ALR, RW_GLR, RW_VLR = 768, 896, 1024, 1280


def _dot(a, b, dims=None):
    dims = dims or (((a.ndim - 1,), (0,)), ((), ()))
    return lax.dot_general(a.astype(BF16), b.astype(BF16), dims, preferred_element_type=F32)


def _split_bf16(a):
    hi = a.astype(BF16)
    return hi, (a - hi.astype(F32)).astype(BF16)


def _dot3(a, b_hi, b_lo, dims=None):
    dims = dims or (((a.ndim - 1,), (0,)), ((), ()))
    a_hi, a_lo = _split_bf16(a)
    d = functools.partial(lax.dot_general, dimension_numbers=dims, preferred_element_type=F32)
    return d(a_hi, b_hi) + (d(a_lo, b_hi) + d(a_hi, b_lo))


def _dot_exact(a, b):
    if a.dtype == BF16:
        hi, lo = _split_bf16(b)
        return jnp.dot(a, hi, preferred_element_type=F32) + jnp.dot(a, lo, preferred_element_type=F32)
    hi, lo = _split_bf16(a)
    return jnp.dot(hi, b, preferred_element_type=F32) + jnp.dot(lo, b, preferred_element_type=F32)


def _hi_lo(w):
    return jnp.stack(_split_bf16(w))


_NT = (((1,), (1,)), ((), ()))
_TN = (((0,), (0,)), ((), ()))


def _sigmoid(x):
    return 1.0 / (1.0 + jnp.exp(-x))


def _softplus(x):
    return jnp.maximum(x, 0.0) + jnp.log(1.0 + jnp.exp(-jnp.abs(x)))


def _head_norm(x, w, b, eps):
    mu = jnp.mean(x, axis=-1, keepdims=True)
    xc = x - mu
    var = jnp.mean(xc * xc, axis=-1, keepdims=True)
    return xc * lax.rsqrt(var + eps) * w + b


def _layer_norm(x, w, b):
    mu = jnp.mean(x, axis=-1, keepdims=True)
    xc = x - mu
    var = jnp.mean(xc * xc, axis=-1, keepdims=True)
    return xc * lax.rsqrt(var + LN_EPS) * w + b


def _params(*sem):
    return pltpu.CompilerParams(dimension_semantics=sem, vmem_limit_bytes=VMEM_LIMIT)


def _full(shape):
    nd = len(shape)
    return pl.BlockSpec(shape, lambda *_: (0,) * nd)


def _each(f, *lists):
    return [f(*args) for args in zip(*lists)]


def _run_together(*tracers):
    out = [None] * len(tracers)
    live = dict(enumerate(tracers))
    while live:
        for i in list(live):
            try:
                next(live[i])
            except StopIteration as stop:
                out[i] = stop.value
                del live[i]
    return out


def _proj_kernel(x_ref, wg_ref, wr_ref, wn_ref, cos_ref, sin_ref,
                 og_ref, or_ref, q_ref, kvc_ref, kk_ref, vt_ref, gt_ref, *, n_qblk, seq_len):
    xb = x_ref[...].astype(BF16)
    hn = jnp.dot(xb, wn_ref[...], preferred_element_type=F32)
    og_ref[...] = jnp.dot(xb, wg_ref[...], preferred_element_type=F32)

    cos = cos_ref[...]
    sin = sin_ref[...]
    half = HEAD_DIM // 2
    first_half = (lax.broadcasted_iota(jnp.int32, cos.shape, 1) & (HEAD_DIM - 1)) < half

    def rope(x):
        swapped = jnp.where(first_half, pltpu.roll(x, LANES - half, 1), pltpu.roll(x, half, 1))
        return x * cos + swapped * sin

    def col(i):
        return hn[:, NSA_WIDTH + LANES * i:NSA_WIDTH + LANES * (i + 1)]

    for i in range(NSA_WIDTH // LANES):
        cs = slice(LANES * i, LANES * (i + 1))
        q_ref[:, cs] = rope(hn[:, cs]) * (HEAD_DIM ** -0.5)
    or_ref[...] = jnp.dot(xb, wr_ref[...], preferred_element_type=F32)
    kvc_ref[:, 0:LANES] = rope(col(0))
    kvc_ref[:, LANES:2 * LANES] = col(1)
    kk_ref[:, 0:LANES] = rope(col(2)).astype(BF16)
    pos = (pl.program_id(0) * x_ref.shape[0]) % seq_len + lax.broadcasted_iota(jnp.int32, cos.shape, 0)
    kk_ref[:, LANES:2 * LANES] = jnp.where(
        pos // NSA_SEL_BLOCK == lax.broadcasted_iota(jnp.int32, cos.shape, 1), 1.0, 0.0).astype(BF16)
    kk_ref[:, 2 * LANES:3 * LANES] = rope(col(4)).astype(BF16)
    pad = VT_ROWS - HEAD_DIM
    ones_rows = jnp.where(lax.broadcasted_iota(jnp.int32, (pad, QBLK), 0) == 0, 1.0, 0.0).astype(BF16)
    for i in range(n_qblk):
        rs = slice(QBLK * i, QBLK * (i + 1))
        for branch, c in enumerate((3, 5)):
            v_t = col(c)[rs, :].T.astype(BF16)
            for g in range(NSA_KV_HEADS):
                r0 = (branch * NSA_KV_HEADS + g) * VT_ROWS
                vt_ref[i, r0:r0 + HEAD_DIM, :] = v_t[HEAD_DIM * g:HEAD_DIM * (g + 1), :]
                vt_ref[i, r0 + HEAD_DIM:r0 + VT_ROWS, :] = ones_rows
    gt_ref[...] = _sigmoid(hn[:, NSA_WIDTH + 6 * LANES:NSA_SLAB])


def _proj(x2, wg, wr, wn, cos2, sin2, tm=512):
    t = x2.shape[0]
    s = cos2.shape[0]
    n_qblk = tm // QBLK

    def rows(width):
        return pl.BlockSpec((tm, width), lambda i: (i, 0))

    def resident(a):
        return pl.BlockSpec(a.shape, lambda i: (0,) * a.ndim, pipeline_mode=pl.Buffered(1))

    pos_rows = pl.BlockSpec((tm, LANES), lambda i: (i % (s // tm), 0))
    return pl.pallas_call(
        functools.partial(_proj_kernel, n_qblk=n_qblk, seq_len=s),
        grid=(t // tm,),
        in_specs=[rows(D_MODEL), resident(wg), resident(wr), resident(wn), pos_rows, pos_rows],
        out_specs=[rows(GLA_SLAB), rows(RWKV_SLAB), rows(NSA_WIDTH), rows(2 * LANES), rows(3 * LANES),
                   pl.BlockSpec((n_qblk, 4 * VT_ROWS, QBLK), lambda i: (i, 0, 0)), rows(LANES)],
        out_shape=[jax.ShapeDtypeStruct((t, GLA_SLAB), F32),
                   jax.ShapeDtypeStruct((t, RWKV_SLAB), F32),
                   jax.ShapeDtypeStruct((t, NSA_WIDTH), F32),
                   jax.ShapeDtypeStruct((t, 2 * LANES), F32),
                   jax.ShapeDtypeStruct((t, 3 * LANES), BF16),
                   jax.ShapeDtypeStruct((t // QBLK, 4 * VT_ROWS, QBLK), BF16),
                   jax.ShapeDtypeStruct((t, LANES), F32)],
        compiler_params=_params("parallel"),
    )(x2, wg, wr, wn, cos2, sin2)


def _group_rows(x, u):
    return jnp.concatenate(
        [jnp.broadcast_to(x[SUB * i + u:SUB * i + u + 1, :], (SUB, x.shape[1]))
         for i in range(CHUNK // SUB)], axis=0)


def _gla_steps(h_ref, wa2_ref, ba_ref, lnw_ref, lnb_ref, tril_ref, blk_ref, ones_ref, dm_ref, same_ref,
               same_sub_ref, o_ref, st_ref, n_chunks):
    each = _each
    row_in_sub = lax.broadcasted_iota(jnp.int32, (CHUNK, LIN_WIDTH), 0) % SUB
    dm = dm_ref[...]
    heads = range(N_LIN_HEADS)
    n_sub = CHUNK // SUB

    def own_head(m, mask_ref):
        return jnp.concatenate([m] * N_LIN_HEADS, axis=0) * mask_ref[...]

    rows = [pl.ds(j * CHUNK, CHUNK) for j in range(n_chunks)]
    ps = [h_ref[0, rw, :] for rw in rows]
    q = each(lambda p: p[:, 0:256] * (HEAD_DIM ** -0.5), ps)
    k = each(lambda p: p[:, 256:512], ps)
    v = each(lambda p: p[:, 512:768], ps)
    g = each(lambda p: p[:, 768:1024], ps)
    z = each(lambda p: _dot3(p[:, 1024:1152], wa2_ref[0], wa2_ref[1]) + ba_ref[...], ps)
    yield
    log_a = each(lambda x: -_softplus(-x) * (1.0 / GLA_TAU), z)
    yield
    b = each(lambda x: _dot_exact(tril_ref[...], x), log_a)
    yield
    b_end = each(lambda x: _dot_exact(blk_ref[...], x), log_a)
    yield
    qe = each(lambda x, bb: x * jnp.exp(bb), q, b)
    kd = each(lambda x, be, bb: x * jnp.exp(be - bb), k, b_end, b)
    p_end = each(jnp.exp, b_end)
    v_st = each(lambda x: jnp.concatenate([x[:, HEAD_DIM * h:HEAD_DIM * (h + 1)] for h in heads], axis=0), v)
    yield

    b2 = each(lambda x: x * LOG2E, b)
    at = [jnp.zeros((CHUNK, LIN_WIDTH), F32) for _ in ps]
    for u0 in range(0, SUB, N_LIN_HEADS):
        us = range(u0, u0 + N_LIN_HEADS)
        es = each(lambda bb, qq, kk: jnp.concatenate(
            [jnp.exp2(jnp.where(row_in_sub <= u, _group_rows(bb, u) - bb, MASK_NEG)) * (_group_rows(qq, u) * kk)
             for u in us], axis=0), b2, q, k)
        yield
        a_rep = each(lambda e: _dot(e, ones_ref[...]), es)
        yield
        for i, u in enumerate(us):
            at = each(lambda acc, ar: acc + jnp.where(dm == u, ar[CHUNK * i:CHUNK * (i + 1), :], 0.0), at, a_rep)
        yield
    o_st = each(lambda x, vs: _dot(own_head(x, same_ref), vs, _TN), at, v_st)
    yield

    upd, q_own = [], []
    for i in range(n_sub):
        rs = slice(SUB * i, SUB * (i + 1))
        upd.append(each(lambda vs, kk: _dot(
            jnp.concatenate([vs[CHUNK * h + SUB * i:CHUNK * h + SUB * (i + 1), :] for h in heads], axis=0),
            own_head(kk[rs], same_sub_ref), _TN), v_st, kd))
        q_own.append(each(lambda x: own_head(x[rs], same_sub_ref), qe))
        yield
    st = st_ref[...]
    for c, rw in enumerate(rows):
        inter = []
        for i in range(n_sub):
            inter.append(_dot(q_own[i][c], st, _NT))
            st = st * p_end[c][SUB * i:SUB * i + 1, :] + upd[i][c]
        o_c = o_st[c] + jnp.concatenate(
            [inter[i][SUB * h:SUB * (h + 1), :] for h in heads for i in range(n_sub)], axis=0)
        o_c = _head_norm(o_c, 1.0, 0.0, LN_EPS)
        o = jnp.concatenate([o_c[CHUNK * h:CHUNK * (h + 1), :] for h in heads], axis=1)
        o_ref[0, rw, :] = (o * lnw_ref[...] + lnb_ref[...]) * (g[c] * _sigmoid(g[c]))
        yield
    st_ref[...] = st


def _rwkv_steps(h_ref, vf_ref, mu_ref, w0_ref, w2_ref, a0_ref, a2_ref, g2_ref, kk_ref, ka_ref, rk_ref,
                lnw_ref, lnb_ref, v0_ref, v2_ref, tril_ref, ones_ref, same_ref, strict_ref, incl_ref, eye_ref,
                o_ref, st_ref, prev_ref, n_chunks, has_vres):
    each = _each
    row = lax.broadcasted_iota(jnp.int32, (CHUNK, RWKV_SLAB), 0)
    blocks = range(N_LIN_HEADS // RWKV_HB)
    in_block = range(RWKV_HB)
    nr = RWKV_HB * CHUNK

    def tile_rows(m):
        return jnp.concatenate([m] * RWKV_HB, axis=0)

    def own_head(m):
        return tile_rows(m) * same_ref[...]

    def stack(m):
        return jnp.concatenate([m[:, HEAD_DIM * h:HEAD_DIM * (h + 1)] for h in in_block], axis=0)

    def unstack(m):
        return jnp.concatenate([m[CHUNK * h:CHUNK * (h + 1), :] for h in in_block], axis=1)

    def per_block(xs):
        return [x[:, RW_BW * bi:RW_BW * (bi + 1)] for x in xs for bi in blocks]

    rows = [pl.ds(j * CHUNK, CHUNK) for j in range(n_chunks)]
    ps = [h_ref[0, rw, :] for rw in rows]
    vfs = [vf_ref[0, rw, :] if has_vres else None for rw in rows]
    prev_rows = [prev_ref[...]] + [p[CHUNK - 1:CHUNK, :] for p in ps[:-1]]

    prev = each(lambda p, pr: jnp.where(row == 0, pr, pltpu.roll(p, 1, 0)), ps, prev_rows)
    xm = each(lambda p, pv: p + (pv - p) * mu_ref[...], ps, prev)
    yield
    r = each(lambda x: x[:, 0:256], xm)
    k = each(lambda x: x[:, 256:512], xm)
    v = each(lambda x: x[:, 512:768], xm)
    v_shift = v
    w_log = each(lambda x: -_softplus(-(w0_ref[...] + _dot3(jnp.tanh(x[:, RW_WLR:RW_ALR]), w2_ref[0], w2_ref[1])))
                 - 0.5, xm)
    yield
    lw = each(lambda w: -jnp.exp(w), w_log)
    a = each(lambda x: _sigmoid(a0_ref[...] + _dot3(x[:, RW_ALR:RW_GLR], a2_ref[0], a2_ref[1])), xm)
    yield
    g = each(lambda x: _dot3(_sigmoid(x[:, RW_GLR:RW_VLR]), g2_ref[0], g2_ref[1]), xm)
    yield
    if has_vres:
        mixv = each(lambda x: _sigmoid(v0_ref[...] + _dot3(x[:, RW_VLR:RWKV_SLAB], v2_ref[0], v2_ref[1])), xm)
        v = each(lambda vv, vf, mx: vv + (vf - vv) * mx, v, vfs, mixv)
        yield
    kkv = each(lambda kk: kk * kk_ref[...], k)
    nrm = each(lambda kv: jnp.sqrt(_dot_exact(kv * kv, ones_ref[...])), kkv)
    yield
    kkn = each(lambda kv, n: kv / jnp.maximum(n, 1e-12), kkv, nrm)
    k2 = each(lambda kk, aa: kk * (1.0 + (aa - 1.0) * ka_ref[...]), k, a)
    bv = each(lambda kn, aa: kn * aa, kkn, a)
    c = each(lambda w: _dot_exact(tril_ref[...], w), lw)
    yield
    c_end = each(lambda cc: cc[CHUNK - 1:CHUNK, :], c)
    bonus = each(lambda rr, kk, vv: _dot_exact(rr * kk * rk_ref[...], ones_ref[...]) * vv, r, k2, v)
    yield
    at = each(lambda kn, cc, w: -kn * jnp.exp(cc - w), kkn, c, lw)
    rt = each(lambda rr, cc: rr * jnp.exp(cc), r, c)
    yield
    e_nc = each(lambda cc: jnp.exp(-cc), c)
    bt = each(lambda b_, e: b_ * e, bv, e_nc)
    kt = each(lambda kk, e: kk * e, k2, e_nc)
    yield
    e_rest = each(lambda ce, cc: jnp.exp(ce - cc), c_end, c)
    b_rest = each(lambda b_, e: b_ * e, bv, e_rest)
    k_rest = each(lambda kk, e: kk * e, k2, e_rest)
    p_end = per_block(each(jnp.exp, c_end))
    yield

    ar = each(lambda x, y: jnp.concatenate([own_head(x), own_head(y)], axis=0), per_block(at), per_block(rt))
    yield
    bk = each(lambda x, y: jnp.concatenate([tile_rows(x), tile_rows(y)], axis=0), per_block(bt), per_block(kt))
    yield
    gm = each(lambda x, y: _dot(x, y, _NT), ar, bk)
    yield
    n_ab = each(lambda m: m[0:nr, 0:nr] * strict_ref[...], gm)
    m_ak = each(lambda m: m[0:nr, nr:2 * nr] * strict_ref[...], gm)
    r_bk = each(lambda m: m[nr:2 * nr, :] * jnp.concatenate([incl_ref[...], incl_ref[...]], axis=1), gm)
    yield
    v_st = each(stack, per_block(v))
    mv = each(_dot, m_ak, v_st)
    yield
    inv = each(lambda n: eye_ref[...] + n, n_ab)
    pw = n_ab
    for _ in range(5):
        pw = each(lambda x: _dot(x, x), pw)
        yield
        inv = each(lambda i_, x: i_ + _dot(i_, x), inv, pw)
        yield
    bk_rest = each(lambda x, y: jnp.concatenate([own_head(x), own_head(y)], axis=0),
                   per_block(b_rest), per_block(k_rest))
    yield
    a_bar = each(lambda i_, x: _dot(i_, x[0:nr]), inv, ar)
    yield
    uv0 = each(lambda i_, m, vs: jnp.concatenate([_dot(i_, m), vs], axis=0), inv, mv, v_st)
    yield
    r_bar = each(lambda x, rb, ab: x[nr:2 * nr] + _dot(rb[:, 0:nr], ab), ar, r_bk, a_bar)
    yield
    y0 = each(_dot, r_bk, uv0)
    yield
    g_mat = each(lambda ab, br: _dot(ab, br[0:nr], _TN), a_bar, bk_rest)
    yield
    sc = each(lambda z, br: _dot(z, br, _TN), uv0, bk_rest)
    yield

    n_blk = len(blocks)
    s = per_block([st_ref[...]])
    for j, rw in enumerate(rows):
        it = slice(n_blk * j, n_blk * (j + 1))
        y = each(lambda rb, s_, y_: unstack(_head_norm(_dot(rb, s_, _NT) + y_, 1.0, 0.0, RWKV_GN_EPS)),
                 r_bar[it], s, y0[it])
        s = each(lambda s_, pe, gm_, c_: s_ * pe + _dot(s_, gm_) + c_, s, p_end[it], g_mat[it], sc[it])
        o_ref[0, rw, :] = (jnp.concatenate(y, axis=1) * lnw_ref[...] + lnb_ref[...] + bonus[j]) * g[j]
        if not has_vres:
            vf_ref[0, rw, :] = v_shift[j]
        yield
    prev_ref[...] = ps[-1][CHUNK - 1:CHUNK, :]
    st_ref[...] = jnp.concatenate(s, axis=1)


N_GLA_IN, N_RWKV_IN = 11, 18


def _linmix_kernel(*refs, n_chunks, has_vres):
    refs = list(refs)
    gla_in = refs[:N_GLA_IN]
    n_rw = N_RWKV_IN + (3 if has_vres else 0)
    rwkv_in = refs[N_GLA_IN:N_GLA_IN + n_rw]
    outs = refs[N_GLA_IN + n_rw:]
    if has_vres:
        (h_ref, vf_ref, mu, w0, w2, a0, a2, g2, kk, ka, rk, lnw, lnb, v0, v2, *consts) = rwkv_in
        o_gla_ref, o_rwkv_ref, gla_st, rwkv_st, prev_ref = outs
    else:
        (h_ref, mu, w0, w2, a0, a2, g2, kk, ka, rk, lnw, lnb, *consts) = rwkv_in
        v0 = v2 = None
        o_gla_ref, o_rwkv_ref, vf_ref, gla_st, rwkv_st, prev_ref = outs

    @pl.when(pl.program_id(1) == 0)
    def _():
        gla_st[...] = jnp.zeros_like(gla_st)
        rwkv_st[...] = jnp.zeros_like(rwkv_st)
        prev_ref[...] = jnp.zeros_like(prev_ref)

    _run_together(
        _gla_steps(*gla_in, o_gla_ref, gla_st, n_chunks),
        _rwkv_steps(h_ref, vf_ref, mu, w0, w2, a0, a2, g2, kk, ka, rk, lnw, lnb, v0, v2, *consts,
                    o_rwkv_ref, rwkv_st, prev_ref, n_chunks, has_vres))


def _linmix(hg, hr, v_first, gla_small, rwkv_small, ts=512):
    bsz, s, _ = hg.shape
    has_vres = v_first is not None
    r = np.arange(CHUNK)
    c = np.arange(LIN_WIDTH)
    ones = (c[:, None] // HEAD_DIM == c[None, :] // HEAD_DIM).astype(np.float32)
    tril_sub = ((r[:, None] // SUB == r[None, :] // SUB) & (r[None, :] <= r[:, None])).astype(np.float32)
    blk_sub = (r[:, None] // SUB == r[None, :] // SUB).astype(np.float32)
    hs = np.arange(N_LIN_HEADS * SUB)
    same_sub = (hs[:, None] // SUB == c[None, :] // HEAD_DIM).astype(np.float32)
    dm = ((c[None, :] % HEAD_DIM) - SUB * (r[:, None] // SUB)).astype(np.int32)
    gla_consts = [jnp.asarray(tril_sub, BF16), jnp.asarray(blk_sub, BF16), jnp.asarray(ones, BF16), jnp.asarray(dm),
                  jnp.asarray(ones), jnp.asarray(same_sub)]
    tril = (r[None, :] <= r[:, None]).astype(np.float32)
    i = np.arange(RW_BW)
    same = (i[:, None] // HEAD_DIM == i[None, :] // HEAD_DIM).astype(np.float32)
    t_in, s_in = i[:, None] % CHUNK, i[None, :] % CHUNK
    rwkv_consts = [jnp.asarray(tril, BF16), jnp.asarray(ones, BF16), jnp.asarray(same),
                   jnp.asarray(same * (s_in < t_in)), jnp.asarray(same * (s_in <= t_in)),
                   jnp.asarray(np.eye(RW_BW, dtype=np.float32))]

    def seq(width):
        return pl.BlockSpec((1, ts, width), lambda b, j: (b, j, 0))

    gla_args = [hg] + list(gla_small) + gla_consts
    rwkv_args = [hr] + ([v_first] if has_vres else []) + list(rwkv_small) + rwkv_consts
    assert len(gla_args) == N_GLA_IN and len(rwkv_args) == N_RWKV_IN + (3 if has_vres else 0)
    in_specs = ([seq(GLA_SLAB)] + [_full(a.shape) for a in gla_args[1:]]
                + [seq(RWKV_SLAB)] + ([seq(LIN_WIDTH)] if has_vres else [])
                + [_full(a.shape) for a in rwkv_args[(2 if has_vres else 1):]])
    n_out = 2 if has_vres else 3
    out_sd = jax.ShapeDtypeStruct((bsz, s, LIN_WIDTH), F32)
    return pl.pallas_call(
        functools.partial(_linmix_kernel, n_chunks=ts // CHUNK, has_vres=has_vres),
        grid=(bsz, s // ts),
        in_specs=in_specs,
        out_specs=[seq(LIN_WIDTH)] * n_out,
        out_shape=[out_sd] * n_out,
        scratch_shapes=[pltpu.VMEM((HEAD_DIM, LIN_WIDTH), F32),
                        pltpu.VMEM((HEAD_DIM, LIN_WIDTH), F32),
                        pltpu.VMEM((1, RWKV_SLAB), F32)],
        compiler_params=_params("parallel", "arbitrary"),
    )(*gla_args, *rwkv_args)


def _nsa_cmp_kernel(kc_ref, vc_ref, pk_ref, pv_ref, wk1h_ref, wk1l_ref, wk2h_ref, wk2l_ref,
                    wv1h_ref, wv1l_ref, wv2h_ref, wv2l_ref, ko_ref, vo_ref, *, n_blk):
    half = NSA_CMP_BLOCK // 2
    k_half = half * HEAD_DIM
    lo_lanes = lax.broadcasted_iota(jnp.int32, (n_blk, LANES), 1) < HEAD_DIM

    def compress(t_ref, pos_ref, w1h_ref, w1l_ref, w2h_ref, w2l_ref, o_ref):
        flat = [[], []]
        for i in range(half // 2):
            a = t_ref[0, pl.ds(2 * i, n_blk, stride=NSA_CMP_STRIDE), :]
            b = t_ref[0, pl.ds(2 * i + 1, n_blk, stride=NSA_CMP_STRIDE), :]
            flat[0].append(jnp.where(lo_lanes, a, pltpu.roll(b, HEAD_DIM, 1)))
            flat[1].append(jnp.where(lo_lanes, pltpu.roll(a, HEAD_DIM, 1), b))
        for g in range(NSA_KV_HEADS):
            gs = slice(HEAD_DIM * g, HEAD_DIM * (g + 1))
            top = jnp.concatenate([x + pos_ref[i:i + 1, :] for i, x in enumerate(flat[g])], axis=1)
            bot = jnp.concatenate([x + pos_ref[half // 2 + i:half // 2 + i + 1, :] for i, x in enumerate(flat[g])],
                                  axis=1)
            top = _dot3(top, w1h_ref[0:k_half, :], w1l_ref[0:k_half, :])
            bot = _dot3(bot, w1h_ref[k_half:, :], w1l_ref[k_half:, :])
            hid = top + pltpu.roll(bot, n_blk - 1, 0)
            act = 0.5 * hid * (1.0 + jnp.tanh(0.7978845608028654 * (hid + 0.044715 * hid * hid * hid)))
            o_ref[0, :, gs] = _dot3(act, w2h_ref[...], w2l_ref[...])

    compress(kc_ref, pk_ref, wk1h_ref, wk1l_ref, wk2h_ref, wk2l_ref, ko_ref)
    compress(vc_ref, pv_ref, wv1h_ref, wv1l_ref, wv2h_ref, wv2l_ref, vo_ref)


def _nsa_cmp(kv, pos_k, pos_v, wk1, wk2, wv1, wv2):
    bsz, s, _ = kv.shape
    n_blk = s // NSA_CMP_STRIDE
    pos_k = pos_k.reshape(NSA_CMP_BLOCK // 2, 2 * HEAD_DIM)
    pos_v = pos_v.reshape(NSA_CMP_BLOCK // 2, 2 * HEAD_DIM)
    small = [pos_k, pos_v, *_split_bf16(wk1), *_split_bf16(wk2), *_split_bf16(wv1), *_split_bf16(wv2)]
    out_spec = pl.BlockSpec((1, n_blk, LANES), lambda b: (b, 0, 0))
    out_sd = jax.ShapeDtypeStruct((bsz, n_blk, LANES), F32)
    return pl.pallas_call(
        functools.partial(_nsa_cmp_kernel, n_blk=n_blk),
        grid=(bsz,),
        in_specs=[pl.BlockSpec((1, s, LANES), lambda b: (b, 0, 0)),
                  pl.BlockSpec((1, s, LANES), lambda b: (b, 0, 1))] + [_full(a.shape) for a in small],
        out_specs=[out_spec, out_spec],
        out_shape=[out_sd, out_sd],
        compiler_params=_params("parallel"),
    )(kv, kv, *small)


def _nsa_attn_kernel(q_ref, gt_ref, kc_ref, vc_ref, ks_ref, kw_ref, vt_ref, ovt_ref,
                     o_ref, acc_ref, s_ref, cm_ref, m_ref, *, n_cmp):
    qi = pl.program_id(1)
    cols = NSA_HPG * QBLK
    n_sel_blk = ovt_ref.shape[0]
    key_off = lax.broadcasted_iota(jnp.int32, (QBLK, cols), 0)
    t_pos = qi * QBLK + (lax.broadcasted_iota(jnp.int32, (QBLK, cols), 1) & (QBLK - 1))
    gt_t = gt_ref[0].T
    zeros64 = jnp.zeros((QBLK, HEAD_DIM), F32)
    m0 = jnp.full((1, cols), MASK_NEG, F32)

    groups = list(range(NSA_KV_HEADS))

    each, run_together = _each, _run_together

    def stacked_q(g):
        pieces = []
        for hp in range(NSA_HPG):
            qh = q_ref[0, :, (g * NSA_HPG + hp) * HEAD_DIM:(g * NSA_HPG + hp + 1) * HEAD_DIM]
            pieces.append(jnp.concatenate([qh, zeros64] if g == 0 else [zeros64, qh], axis=1))
        return jnp.concatenate(pieces, axis=0)

    qs = each(stacked_q, groups)
    qs_b = each(lambda q: (q * LOG2E).astype(BF16), qs)

    def select_steps():
        valid_c = (key_off * NSA_CMP_STRIDE + (NSA_CMP_BLOCK - 1) <= t_pos) & (key_off < n_cmp)
        s_c = each(lambda q: jnp.where(valid_c, _dot3(kc_ref[0], *_split_bf16(q), _NT), MASK_NEG), qs)
        yield
        e_c = each(lambda x: jnp.exp(x - jnp.max(x, axis=0, keepdims=True)), s_c)
        yield
        p_c = each(lambda e: jnp.where(valid_c, e / jnp.sum(e, axis=0, keepdims=True), 0.0), e_c)
        yield
        o_cmp = each(lambda p, g: _dot(vc_ref[0], p, _TN)[HEAD_DIM * g:HEAD_DIM * (g + 1), :], p_c, groups)
        yield
        p_sum = each(lambda p: p[:, 0:QBLK] + p[:, QBLK:2 * QBLK] + p[:, 2 * QBLK:3 * QBLK] + p[:, 3 * QBLK:], p_c)
        imp = each(lambda p: _dot_exact(ovt_ref[...], p), p_sum)
        yield
        j = lax.broadcasted_iota(jnp.int32, (n_sel_blk, QBLK), 0)
        tq = qi * QBLK + lax.broadcasted_iota(jnp.int32, (n_sel_blk, QBLK), 1)
        cur = tq // NSA_SEL_BLOCK
        forced = (j == 0) | (j == cur) | (j == cur - 1)
        score = each(lambda x: jnp.where(forced, 1e9, jnp.where(j > cur, -1e9, x)), imp)
        rank = [jnp.zeros((n_sel_blk, QBLK), F32) for _ in groups]
        for i in range(n_sel_blk):
            rank = each(lambda rk, sc: rk + jnp.where(
                (sc[i:i + 1, :] > sc) | ((sc[i:i + 1, :] == sc) & (j > i)), 1.0, 0.0), rank, score)
            if i % 4 == 3:
                yield
        return o_cmp, each(lambda rk: jnp.where(rk < NSA_N_SELECT, 1.0, 0.0), rank)

    def key_scores(k_ref, q, kb0, n_kb, mask):
        k0 = pl.multiple_of(kb0 * QBLK, QBLK)
        s = lax.dot_general(k_ref[0, pl.ds(k0, n_kb * QBLK), :], q, _NT, preferred_element_type=F32)
        if mask is not None:
            key_pos = k0 + lax.broadcasted_iota(jnp.int32, (n_kb * QBLK, cols), 0)
            t_all = jnp.concatenate([t_pos] * n_kb, axis=0)
            valid = key_pos <= t_all
            if mask == "window":
                valid = valid & (key_pos > t_all - NSA_WINDOW)
            s = jnp.where(valid, s, MASK_NEG)
        return s

    def softmax_steps(slabs, kb0, n_kb, block_scores, col_max, m_old):
        m_new = each(jnp.maximum, m_old, col_max)
        pv = [jnp.zeros((VT_ROWS, cols), F32) for _ in slabs]
        for i in range(n_kb):
            pv = [acc + jnp.dot(vt_ref[0, kb0 + i, slab * VT_ROWS:(slab + 1) * VT_ROWS, :],
                                jnp.exp2(rd(i) - mn).astype(BF16), preferred_element_type=F32)
                  for acc, rd, mn, slab in zip(pv, block_scores, m_new, slabs)]
            yield
        for slab, m, mn, x in zip(slabs, m_old, m_new, pv):
            acc_ref[slab] = acc_ref[slab] * jnp.exp2(m - mn) + x
        return m_new

    def window_steps():
        n_win = NSA_WINDOW // QBLK + 1
        kb_win = jnp.maximum(qi - (n_win - 1), 0)
        s = []
        for g in groups:
            s.append(key_scores(kw_ref, qs_b[g], kb_win, n_win, "window"))
            yield
        col_max = each(lambda x: jnp.max(x, axis=0, keepdims=True), s)
        yield
        blocks = [lambda i, x=x: x[QBLK * i:QBLK * (i + 1), :] for x in s]
        yield from softmax_steps([NSA_KV_HEADS + g for g in groups], kb_win, n_win, blocks, col_max, [m0] * len(s))

    def normalised(slab):
        return acc_ref[slab, 0:HEAD_DIM, :] / acc_ref[slab, HEAD_DIM:HEAD_DIM + 1, :]

    acc_ref[...] = jnp.zeros_like(acc_ref)
    for g in groups:
        m_ref[g] = m0

    (o_cmp, sel), _ = run_together(select_steps(), window_steps())

    def with_selection(q, sl):
        off = jnp.where(sl > 0.5, 0.0, MASK_NEG).T.astype(BF16)
        pad = jnp.zeros((cols, LANES - n_sel_blk), BF16)
        return jnp.concatenate([q, jnp.concatenate([off] * NSA_HPG, axis=0), pad], axis=1)

    qs_sel = each(with_selection, qs_b, sel)

    def score_steps(i, mask):
        s = []
        for g in groups:
            s.append(key_scores(ks_ref, qs_sel[g], i * SLC_KB, SLC_KB, mask))
            yield
        return s, each(lambda x: jnp.max(x, axis=0, keepdims=True), s)

    def park(scores_and_max):
        for g, (x, cm) in enumerate(zip(*scores_and_max)):
            s_ref[g] = x
            cm_ref[g] = cm

    def consume_steps(i):
        blocks = [lambda b, g=g: s_ref[g, QBLK * b:QBLK * (b + 1), :] for g in groups]
        m_new = yield from softmax_steps(groups, i * SLC_KB, SLC_KB, blocks, [cm_ref[g] for g in groups],
                                         [m_ref[g] for g in groups])
        for g in groups:
            m_ref[g] = m_new[g]

    n_full = qi // SLC_KB

    @pl.when(n_full >= 1)
    def _():
        park(run_together(score_steps(0, None))[0])

    def pipelined(i, carry):
        nxt, _ = run_together(score_steps(i + 1, None), consume_steps(i))
        park(nxt)
        return carry

    lax.fori_loop(0, n_full - 1, pipelined, 0)

    @pl.when(n_full >= 1)
    def _():
        nxt, _ = run_together(score_steps(n_full, "causal"), consume_steps(n_full - 1))
        park(nxt)

    @pl.when(n_full == 0)
    def _():
        park(run_together(score_steps(0, "causal"))[0])

    run_together(consume_steps(n_full))

    o_groups = []
    for g in groups:
        def gate(branch):
            return jnp.concatenate(
                [gt_t[(g * NSA_HPG + hp) * 3 + branch:(g * NSA_HPG + hp) * 3 + branch + 1, :]
                 for hp in range(NSA_HPG)], axis=1)

        o_groups.append(gate(0) * o_cmp[g] + gate(1) * normalised(g) + gate(2) * normalised(NSA_KV_HEADS + g))

    o_all = jnp.concatenate(o_groups, axis=0)
    for hp in range(NSA_HPG):
        tile = o_all[:, QBLK * hp:QBLK * (hp + 1)].T
        for g in range(NSA_KV_HEADS):
            c0 = (g * NSA_HPG + hp) * HEAD_DIM
            o_ref[0, :, c0:c0 + HEAD_DIM] = tile[:, HEAD_DIM * g:HEAD_DIM * (g + 1)]


def _nsa_attn(qr, kk, vt, gt, kcmp, vcmp):
    bsz, s, _ = qr.shape
    n_blk = s // NSA_CMP_STRIDE
    n_cmp = n_blk - NSA_CMP_BLOCK // NSA_CMP_STRIDE + 1
    n_sel = s // NSA_SEL_BLOCK
    assert n_blk == QBLK, "compressed-score tiles are laid out one stride block per sublane row"
    c0 = np.arange(n_blk) * NSA_CMP_STRIDE
    s0 = np.arange(n_sel) * NSA_SEL_BLOCK
    lo = np.maximum(c0[:, None], s0[None, :])
    hi = np.minimum(c0[:, None] + NSA_CMP_BLOCK, s0[None, :] + NSA_SEL_BLOCK)
    overlap = (np.maximum(hi - lo, 0) / NSA_CMP_STRIDE).astype(np.float32)
    overlap[n_cmp:] = 0.0
    consts = [jnp.asarray(overlap.T, BF16)]

    cmp_spec = pl.BlockSpec((1, n_blk, LANES), lambda b, i: (b, 0, 0))
    return pl.pallas_call(
        functools.partial(_nsa_attn_kernel, n_cmp=n_cmp),
        grid=(bsz, s // QBLK),
        in_specs=[pl.BlockSpec((1, QBLK, NSA_WIDTH), lambda b, i: (b, i, 0)),
                  pl.BlockSpec((1, QBLK, LANES), lambda b, i: (b, i, 0)),
                  cmp_spec, cmp_spec,
                  pl.BlockSpec((1, s, 2 * LANES), lambda b, i: (b, 0, 0)),
                  pl.BlockSpec((1, s, LANES), lambda b, i: (b, 0, 2)),
                  pl.BlockSpec((1, s // QBLK, 4 * VT_ROWS, QBLK), lambda b, i: (b, 0, 0, 0))]
        + [_full(a.shape) for a in consts],
        out_specs=pl.BlockSpec((1, QBLK, NSA_WIDTH), lambda b, i: (b, i, 0)),
        out_shape=jax.ShapeDtypeStruct((bsz, s, NSA_WIDTH), F32),
        scratch_shapes=[pltpu.VMEM((2 * NSA_KV_HEADS, VT_ROWS, NSA_HPG * QBLK), F32),
                        pltpu.VMEM((NSA_KV_HEADS, SLC_KB * QBLK, NSA_HPG * QBLK), F32),
                        pltpu.VMEM((NSA_KV_HEADS, 1, NSA_HPG * QBLK), F32),
                        pltpu.VMEM((NSA_KV_HEADS, 1, NSA_HPG * QBLK), F32)],
        compiler_params=_params("parallel", "arbitrary"),
    )(qr, gt, kcmp, vcmp, kk, kk, vt, *consts)


def _mix_ffn_kernel(x_ref, og_ref, or_ref, on_ref, wo_ref, l1w_ref, l1b_ref, wg_ref, wu_ref, wd_ref,
                    l2w_ref, l2b_ref, o_ref, *, alpha):
    mix = (_dot(og_ref[...], wo_ref[0:LIN_WIDTH, :])
           + _dot(or_ref[...], wo_ref[LIN_WIDTH:2 * LIN_WIDTH, :])
           + _dot(on_ref[...], wo_ref[2 * LIN_WIDTH:, :]))
    x1 = _layer_norm(alpha * x_ref[...] + mix, l1w_ref[...], l1b_ref[...])
    xb = x1.astype(BF16)
    ffn = jnp.zeros_like(x1)
    for c0, c1 in FFN_SPLITS:
        gate = jnp.dot(xb, wg_ref[:, c0:c1], preferred_element_type=F32)
        up = jnp.dot(xb, wu_ref[:, c0:c1], preferred_element_type=F32)
        ffn = ffn + _dot(gate * _sigmoid(gate) * up, wd_ref[c0:c1, :])
    o_ref[...] = _layer_norm(alpha * x1 + ffn, l2w_ref[...], l2b_ref[...])


def _mix_ffn(x2, og, orw, on, wo, l1w, l1b, wg, wu, wd, l2w, l2b, alpha, tm=512):
    t = x2.shape[0]

    def rows(width):
        return pl.BlockSpec((tm, width), lambda i: (i, 0))

    def resident(a):
        return pl.BlockSpec(a.shape, lambda i: (0,) * a.ndim, pipeline_mode=pl.Buffered(1))

    consts = [wo, l1w, l1b, wg, wu, wd, l2w, l2b]
    return pl.pallas_call(
        functools.partial(_mix_ffn_kernel, alpha=alpha),
        grid=(t // tm,),
        in_specs=[rows(D_MODEL), rows(LIN_WIDTH), rows(LIN_WIDTH), rows(NSA_WIDTH)] + [resident(a) for a in consts],
        out_specs=rows(D_MODEL),
        out_shape=jax.ShapeDtypeStruct((t, D_MODEL), F32),
        compiler_params=_params("parallel"),
    )(x2, og, orw, on, *consts)


def _pad_cols(w, width):
    return jnp.pad(w, ((0, 0), (0, width - w.shape[1])))


def _pad_rows(w, height):
    return jnp.pad(w, ((0, height - w.shape[0]), (0, 0)))


def _rwkv_slab_cols(w, w_vres):
    o = 3 * LIN_WIDTH
    parts = [w[:, :o],
             _pad_cols(w[:, o:o + RWKV_DECAY_RANK], LANES),
             _pad_cols(w[:, o + RWKV_DECAY_RANK:o + RWKV_DECAY_RANK + RWKV_ICLR_RANK], LANES),
             _pad_cols(w[:, o + RWKV_DECAY_RANK + RWKV_ICLR_RANK:], 2 * LANES),
             _pad_cols(w_vres, LANES)]
    return jnp.concatenate(parts, axis=1)


def _rope_tables(s):
    half = HEAD_DIM // 2
    inv = ROPE_THETA ** (-jnp.arange(half, dtype=F32) / half)
    ang = jnp.arange(s, dtype=jnp.int32).astype(F32)[:, None] * inv
    cos, sin = jnp.cos(ang), jnp.sin(ang)
    cos2 = jnp.tile(jnp.concatenate([cos, cos], axis=1), (1, LANES // HEAD_DIM))
    sin2 = jnp.tile(jnp.concatenate([-sin, sin], axis=1), (1, LANES // HEAD_DIM))
    return cos2, sin2


def kernel(x, w_in, w_in_vres, gla_w_a2, gla_b_a, gla_ln_w, gla_ln_b, rwkv_mu, rwkv_mu_vres, rwkv_w0, rwkv_w2, rwkv_a0, rwkv_a2, rwkv_v0, rwkv_v2, rwkv_g2, rwkv_k_k, rwkv_k_a, rwkv_r_k, rwkv_ln_w, rwkv_ln_b, nsa_pos_k, nsa_pos_v, nsa_wk1, nsa_wk2, nsa_wv1, nsa_wv2, w_out, ln1_w, ln1_b, ffn_w_gate, ffn_w_up, ffn_w_down, ln2_w, ln2_b):
    bsz, s, d = x.shape
    depth = w_in.shape[0]
    alpha = float((2 * depth) ** 0.25)
    gla_cols = 4 * LIN_WIDTH + GLA_GATE_RANK
    rwkv_cols = 3 * LIN_WIDTH + RWKV_DECAY_RANK + RWKV_ICLR_RANK + RWKV_GATE_RANK
    cos2, sin2 = _rope_tables(s)
    row = lambda a: a.reshape(1, -1)

    x2 = x.reshape(bsz * s, d)
    v_first = None
    for l in range(depth):
        w = w_in[l]
        w_r = w[:, gla_cols:gla_cols + rwkv_cols]
        if l == 0:
            w_vres = jnp.zeros((d, RWKV_VRES_RANK), F32)
            mu_vres = jnp.zeros((1, RWKV_VRES_RANK), F32)
        else:
            w_vres = w_in_vres[l - 1]
            mu_vres = row(rwkv_mu_vres[l - 1])
        wg = _pad_cols(w[:, :gla_cols], GLA_SLAB).astype(BF16)
        wr = _rwkv_slab_cols(w_r, w_vres).astype(BF16)
        wn = _pad_cols(w[:, gla_cols + rwkv_cols:], NSA_SLAB).astype(BF16)
        hg, hr, qr, kvc, kk, vt, gt = _proj(x2, wg, wr, wn, cos2, sin2)
        hg = hg.reshape(bsz, s, GLA_SLAB)
        hr = hr.reshape(bsz, s, RWKV_SLAB)
        qr = qr.reshape(bsz, s, NSA_WIDTH)
        kvc = kvc.reshape(bsz, s, 2 * LANES)
        kk = kk.reshape(bsz, s, 3 * LANES)
        vt = vt.reshape(bsz, s // QBLK, 4 * VT_ROWS, QBLK)
        gt = gt.reshape(bsz, s, LANES)

        gla_small = [_hi_lo(_pad_rows(gla_w_a2[l], LANES)), row(gla_b_a[l]), row(gla_ln_w[l]), row(gla_ln_b[l])]
        mu = _rwkv_slab_cols(row(rwkv_mu[l]), mu_vres)
        rwkv_small = [mu, row(rwkv_w0[l]), _hi_lo(_pad_rows(rwkv_w2[l], LANES)), row(rwkv_a0[l]),
                      _hi_lo(_pad_rows(rwkv_a2[l], LANES)), _hi_lo(_pad_rows(rwkv_g2[l], 2 * LANES)),
                      row(rwkv_k_k[l]), row(rwkv_k_a[l]), row(rwkv_r_k[l]), row(rwkv_ln_w[l]), row(rwkv_ln_b[l])]
        if l == 0:
            o_gla, o_rwkv, v_first = _linmix(hg, hr, None, gla_small, rwkv_small)
        else:
            rwkv_small += [row(rwkv_v0[l - 1]), _hi_lo(_pad_rows(rwkv_v2[l - 1], LANES))]
            o_gla, o_rwkv = _linmix(hg, hr, v_first, gla_small, rwkv_small)

        kcmp, vcmp = _nsa_cmp(kvc, nsa_pos_k[l], nsa_pos_v[l], nsa_wk1[l], nsa_wk2[l], nsa_wv1[l], nsa_wv2[l])
        o_nsa = _nsa_attn(qr, kk, vt, gt, kcmp, vcmp)

        x2 = _mix_ffn(x2, o_gla.reshape(bsz * s, LIN_WIDTH), o_rwkv.reshape(bsz * s, LIN_WIDTH),
                      o_nsa.reshape(bsz * s, NSA_WIDTH), w_out[l].astype(BF16), row(ln1_w[l]), row(ln1_b[l]),
                      ffn_w_gate[l].astype(BF16), ffn_w_up[l].astype(BF16), ffn_w_down[l].astype(BF16),
                      row(ln2_w[l]), row(ln2_b[l]), alpha)
    return x2.reshape(bsz, s, d)
```

```python
import functools

import numpy as np
import jax
import jax.numpy as jnp
from jax import lax
from jax.experimental import pallas as pl
from jax.experimental.pallas import tpu as pltpu

F32 = jnp.float32
BF16 = jnp.bfloat16

D_MODEL = 1024
HEAD_DIM = 64
N_LIN_HEADS = 4
LIN_WIDTH = N_LIN_HEADS * HEAD_DIM
GLA_GATE_RANK = 16
GLA_TAU = 16.0
RWKV_DECAY_RANK = 64
RWKV_ICLR_RANK = 64
RWKV_VRES_RANK = 32
RWKV_GATE_RANK = 160
RWKV_GN_EPS = 64e-5
NSA_HEADS = 8
NSA_KV_HEADS = 2
NSA_HPG = NSA_HEADS // NSA_KV_HEADS
NSA_WIDTH = NSA_HEADS * HEAD_DIM
NSA_KV_WIDTH = NSA_KV_HEADS * HEAD_DIM
NSA_CMP_BLOCK = 32
NSA_CMP_STRIDE = 16
NSA_CMP_HIDDEN = 256
NSA_SEL_BLOCK = 64
NSA_N_SELECT = 16
NSA_WINDOW = 512
ROPE_THETA = 10000.0
FFN_HIDDEN = 2816
FFN_SPLITS = ((0, 1536), (1536, FFN_HIDDEN))
LN_EPS = 1e-5
MASK_NEG = -1e30

LANES = 128
CHUNK = 64
SUB = 8
RWKV_HB = 2
RW_BW = RWKV_HB * HEAD_DIM
QBLK = 128
SLC_KB = 4
VT_ROWS = HEAD_DIM + 16
LOG2E = 1.4426950408889634
VMEM_LIMIT = 56 * 1024 * 1024

GLA_SLAB = 4 * LIN_WIDTH + LANES
RWKV_SLAB = 3 * LIN_WIDTH + LANES + LANES + 2 * LANES + LANES
NSA_SLAB = NSA_WIDTH + 6 * NSA_KV_WIDTH + LANES
RW_WLR, RW_ALR, RW_GLR, RW_VLR = 768, 896, 1024, 1280


def _dot(a, b, dims=None):
    dims = dims or (((a.ndim - 1,), (0,)), ((), ()))
    return lax.dot_general(a.astype(BF16), b.astype(BF16), dims, preferred_element_type=F32)


def _split_bf16(a):
    hi = a.astype(BF16)
    return hi, (a - hi.astype(F32)).astype(BF16)


def _dot3(a, b_hi, b_lo, dims=None):
    dims = dims or (((a.ndim - 1,), (0,)), ((), ()))
    a_hi, a_lo = _split_bf16(a)
    d = functools.partial(lax.dot_general, dimension_numbers=dims, preferred_element_type=F32)
    return d(a_hi, b_hi) + (d(a_lo, b_hi) + d(a_hi, b_lo))


def _dot_exact(a, b):
    if a.dtype == BF16:
        hi, lo = _split_bf16(b)
        return jnp.dot(a, hi, preferred_element_type=F32) + jnp.dot(a, lo, preferred_element_type=F32)
    hi, lo = _split_bf16(a)
    return jnp.dot(hi, b, preferred_element_type=F32) + jnp.dot(lo, b, preferred_element_type=F32)


def _hi_lo(w):
    return jnp.stack(_split_bf16(w))


_NT = (((1,), (1,)), ((), ()))
_TN = (((0,), (0,)), ((), ()))


def _sigmoid(x):
    return 1.0 / (1.0 + jnp.exp(-x))


def _softplus(x):
    return jnp.maximum(x, 0.0) + jnp.log(1.0 + jnp.exp(-jnp.abs(x)))


def _head_norm(x, w, b, eps):
    mu = jnp.mean(x, axis=-1, keepdims=True)
    xc = x - mu
    var = jnp.mean(xc * xc, axis=-1, keepdims=True)
    return xc * lax.rsqrt(var + eps) * w + b


def _layer_norm(x, w, b):
    mu = jnp.mean(x, axis=-1, keepdims=True)
    xc = x - mu
    var = jnp.mean(xc * xc, axis=-1, keepdims=True)
    return xc * lax.rsqrt(var + LN_EPS) * w + b


def _params(*sem):
    return pltpu.CompilerParams(dimension_semantics=sem, vmem_limit_bytes=VMEM_LIMIT)


def _full(shape):
    nd = len(shape)
    return pl.BlockSpec(shape, lambda *_: (0,) * nd)


def _each(f, *lists):
    return [f(*args) for args in zip(*lists)]


def _run_together(*tracers):
    out = [None] * len(tracers)
    live = dict(enumerate(tracers))
    while live:
        for i in list(live):
            try:
                next(live[i])
            except StopIteration as stop:
                out[i] = stop.value
                del live[i]
    return out


def _proj_kernel(x_ref, wg_ref, wr_ref, wn_ref, cos_ref, sin_ref,
                 og_ref, or_ref, q_ref, kvc_ref, kk_ref, vt_ref, gt_ref, *, n_qblk, seq_len):
    xb = x_ref[...].astype(BF16)
    hn = jnp.dot(xb, wn_ref[...], preferred_element_type=F32)
    og_ref[...] = jnp.dot(xb, wg_ref[...], preferred_element_type=F32)

    cos = cos_ref[...]
    sin = sin_ref[...]
    half = HEAD_DIM // 2
    first_half = (lax.broadcasted_iota(jnp.int32, cos.shape, 1) & (HEAD_DIM - 1)) < half

    def rope(x):
        swapped = jnp.where(first_half, pltpu.roll(x, LANES - half, 1), pltpu.roll(x, half, 1))
        return x * cos + swapped * sin

    def col(i):
        return hn[:, NSA_WIDTH + LANES * i:NSA_WIDTH + LANES * (i + 1)]

    for i in range(NSA_WIDTH // LANES):
        cs = slice(LANES * i, LANES * (i + 1))
        q_ref[:, cs] = rope(hn[:, cs]) * (HEAD_DIM ** -0.5)
    or_ref[...] = jnp.dot(xb, wr_ref[...], preferred_element_type=F32)
    kvc_ref[:, 0:LANES] = rope(col(0))
    kvc_ref[:, LANES:2 * LANES] = col(1)
    kk_ref[:, 0:LANES] = rope(col(2)).astype(BF16)
    pos = (pl.program_id(0) * x_ref.shape[0]) % seq_len + lax.broadcasted_iota(jnp.int32, cos.shape, 0)
    kk_ref[:, LANES:2 * LANES] = jnp.where(
        pos // NSA_SEL_BLOCK == lax.broadcasted_iota(jnp.int32, cos.shape, 1), 1.0, 0.0).astype(BF16)
    kk_ref[:, 2 * LANES:3 * LANES] = rope(col(4)).astype(BF16)
    pad = VT_ROWS - HEAD_DIM
    ones_rows = jnp.where(lax.broadcasted_iota(jnp.int32, (pad, QBLK), 0) == 0, 1.0, 0.0).astype(BF16)
    for i in range(n_qblk):
        rs = slice(QBLK * i, QBLK * (i + 1))
        for branch, c in enumerate((3, 5)):
            v_t = col(c)[rs, :].T.astype(BF16)
            for g in range(NSA_KV_HEADS):
                r0 = (branch * NSA_KV_HEADS + g) * VT_ROWS
                vt_ref[i, r0:r0 + HEAD_DIM, :] = v_t[HEAD_DIM * g:HEAD_DIM * (g + 1), :]
                vt_ref[i, r0 + HEAD_DIM:r0 + VT_ROWS, :] = ones_rows
    gt_ref[...] = _sigmoid(hn[:, NSA_WIDTH + 6 * LANES:NSA_SLAB])


def _proj(x2, wg, wr, wn, cos2, sin2, tm=512):
    t = x2.shape[0]
    s = cos2.shape[0]
    n_qblk = tm // QBLK

    def rows(width):
        return pl.BlockSpec((tm, width), lambda i: (i, 0))

    def resident(a):
        return pl.BlockSpec(a.shape, lambda i: (0,) * a.ndim, pipeline_mode=pl.Buffered(1))

    pos_rows = pl.BlockSpec((tm, LANES), lambda i: (i % (s // tm), 0))
    return pl.pallas_call(
        functools.partial(_proj_kernel, n_qblk=n_qblk, seq_len=s),
        grid=(t // tm,),
        in_specs=[rows(D_MODEL), resident(wg), resident(wr), resident(wn), pos_rows, pos_rows],
        out_specs=[rows(GLA_SLAB), rows(RWKV_SLAB), rows(NSA_WIDTH), rows(2 * LANES), rows(3 * LANES),
                   pl.BlockSpec((n_qblk, 4 * VT_ROWS, QBLK), lambda i: (i, 0, 0)), rows(LANES)],
        out_shape=[jax.ShapeDtypeStruct((t, GLA_SLAB), F32),
                   jax.ShapeDtypeStruct((t, RWKV_SLAB), F32),
                   jax.ShapeDtypeStruct((t, NSA_WIDTH), F32),
                   jax.ShapeDtypeStruct((t, 2 * LANES), F32),
                   jax.ShapeDtypeStruct((t, 3 * LANES), BF16),
                   jax.ShapeDtypeStruct((t // QBLK, 4 * VT_ROWS, QBLK), BF16),
                   jax.ShapeDtypeStruct((t, LANES), F32)],
        compiler_params=_params("parallel"),
    )(x2, wg, wr, wn, cos2, sin2)


def _group_rows(x, u):
    return jnp.concatenate(
        [jnp.broadcast_to(x[SUB * i + u:SUB * i + u + 1, :], (SUB, x.shape[1]))
         for i in range(CHUNK // SUB)], axis=0)


def _gla_steps(h_ref, wa2_ref, ba_ref, lnw_ref, lnb_ref, tril_ref, blk_ref, ones_ref, dm_ref, same_ref,
               same_sub_ref, o_ref, st_ref, n_chunks):
    each = _each
    row_in_sub = lax.broadcasted_iota(jnp.int32, (CHUNK, LIN_WIDTH), 0) % SUB
    dm = dm_ref[...]
    heads = range(N_LIN_HEADS)
    n_sub = CHUNK // SUB

    def own_head(m, mask_ref):
        return jnp.concatenate([m] * N_LIN_HEADS, axis=0) * mask_ref[...]

    rows = [pl.ds(j * CHUNK, CHUNK) for j in range(n_chunks)]
    ps = [h_ref[0, rw, :] for rw in rows]
    q = each(lambda p: p[:, 0:256] * (HEAD_DIM ** -0.5), ps)
    k = each(lambda p: p[:, 256:512], ps)
    v = each(lambda p: p[:, 512:768], ps)
    g = each(lambda p: p[:, 768:1024], ps)
    z = each(lambda p: _dot3(p[:, 1024:1152], wa2_ref[0], wa2_ref[1]) + ba_ref[...], ps)
    yield
    log_a = each(lambda x: -_softplus(-x) * (1.0 / GLA_TAU), z)
    yield
    b = each(lambda x: _dot_exact(tril_ref[...], x), log_a)
    yield
    b_end = each(lambda x: _dot_exact(blk_ref[...], x), log_a)
    yield
    qe = each(lambda x, bb: x * jnp.exp(bb), q, b)
    kd = each(lambda x, be, bb: x * jnp.exp(be - bb), k, b_end, b)
    p_end = each(jnp.exp, b_end)
    v_st = each(lambda x: jnp.concatenate([x[:, HEAD_DIM * h:HEAD_DIM * (h + 1)] for h in heads], axis=0), v)
    yield

    b2 = each(lambda x: x * LOG2E, b)
    at = [jnp.zeros((CHUNK, LIN_WIDTH), F32) for _ in ps]
    for u0 in range(0, SUB, N_LIN_HEADS):
        us = range(u0, u0 + N_LIN_HEADS)
        es = each(lambda bb, qq, kk: jnp.concatenate(
            [jnp.exp2(jnp.where(row_in_sub <= u, _group_rows(bb, u) - bb, MASK_NEG)) * (_group_rows(qq, u) * kk)
             for u in us], axis=0), b2, q, k)
        yield
        a_rep = each(lambda e: _dot(e, ones_ref[...]), es)
        yield
        for i, u in enumerate(us):
            at = each(lambda acc, ar: acc + jnp.where(dm == u, ar[CHUNK * i:CHUNK * (i + 1), :], 0.0), at, a_rep)
        yield
    o_st = each(lambda x, vs: _dot(own_head(x, same_ref), vs, _TN), at, v_st)
    yield

    upd, q_own = [], []
    for i in range(n_sub):
        rs = slice(SUB * i, SUB * (i + 1))
        upd.append(each(lambda vs, kk: _dot(
            jnp.concatenate([vs[CHUNK * h + SUB * i:CHUNK * h + SUB * (i + 1), :] for h in heads], axis=0),
            own_head(kk[rs], same_sub_ref), _TN), v_st, kd))
        q_own.append(each(lambda x: own_head(x[rs], same_sub_ref), qe))
        yield
    st = st_ref[...]
    for c, rw in enumerate(rows):
        inter = []
        for i in range(n_sub):
            inter.append(_dot(q_own[i][c], st, _NT))
            st = st * p_end[c][SUB * i:SUB * i + 1, :] + upd[i][c]
        o_c = o_st[c] + jnp.concatenate(
            [inter[i][SUB * h:SUB * (h + 1), :] for h in heads for i in range(n_sub)], axis=0)
        o_c = _head_norm(o_c, 1.0, 0.0, LN_EPS)
        o = jnp.concatenate([o_c[CHUNK * h:CHUNK * (h + 1), :] for h in heads], axis=1)
        o_ref[0, rw, :] = (o * lnw_ref[...] + lnb_ref[...]) * (g[c] * _sigmoid(g[c]))
        yield
    st_ref[...] = st


def _rwkv_steps(h_ref, vf_ref, mu_ref, w0_ref, w2_ref, a0_ref, a2_ref, g2_ref, kk_ref, ka_ref, rk_ref,
                lnw_ref, lnb_ref, v0_ref, v2_ref, tril_ref, ones_ref, same_ref, strict_ref, incl_ref, eye_ref,
                o_ref, st_ref, prev_ref, n_chunks, has_vres):
    each = _each
    row = lax.broadcasted_iota(jnp.int32, (CHUNK, RWKV_SLAB), 0)
    blocks = range(N_LIN_HEADS // RWKV_HB)
    in_block = range(RWKV_HB)
    nr = RWKV_HB * CHUNK

    def tile_rows(m):
        return jnp.concatenate([m] * RWKV_HB, axis=0)

    def own_head(m):
        return tile_rows(m) * same_ref[...]

    def stack(m):
        return jnp.concatenate([m[:, HEAD_DIM * h:HEAD_DIM * (h + 1)] for h in in_block], axis=0)

    def unstack(m):
        return jnp.concatenate([m[CHUNK * h:CHUNK * (h + 1), :] for h in in_block], axis=1)

    def per_block(xs):
        return [x[:, RW_BW * bi:RW_BW * (bi + 1)] for x in xs for bi in blocks]

    rows = [pl.ds(j * CHUNK, CHUNK) for j in range(n_chunks)]
    ps = [h_ref[0, rw, :] for rw in rows]
    vfs = [vf_ref[0, rw, :] if has_vres else None for rw in rows]
    prev_rows = [prev_ref[...]] + [p[CHUNK - 1:CHUNK, :] for p in ps[:-1]]

    prev = each(lambda p, pr: jnp.where(row == 0, pr, pltpu.roll(p, 1, 0)), ps, prev_rows)
    xm = each(lambda p, pv: p + (pv - p) * mu_ref[...], ps, prev)
    yield
    r = each(lambda x: x[:, 0:256], xm)
    k = each(lambda x: x[:, 256:512], xm)
    v = each(lambda x: x[:, 512:768], xm)
    v_shift = v
    w_log = each(lambda x: -_softplus(-(w0_ref[...] + _dot3(jnp.tanh(x[:, RW_WLR:RW_ALR]), w2_ref[0], w2_ref[1])))
                 - 0.5, xm)
    yield
    lw = each(lambda w: -jnp.exp(w), w_log)
    a = each(lambda x: _sigmoid(a0_ref[...] + _dot3(x[:, RW_ALR:RW_GLR], a2_ref[0], a2_ref[1])), xm)
    yield
    g = each(lambda x: _dot3(_sigmoid(x[:, RW_GLR:RW_VLR]), g2_ref[0], g2_ref[1]), xm)
    yield
    if has_vres:
        mixv = each(lambda x: _sigmoid(v0_ref[...] + _dot3(x[:, RW_VLR:RWKV_SLAB], v2_ref[0], v2_ref[1])), xm)
        v = each(lambda vv, vf, mx: vv + (vf - vv) * mx, v, vfs, mixv)
        yield
    kkv = each(lambda kk: kk * kk_ref[...], k)
    nrm = each(lambda kv: jnp.sqrt(_dot_exact(kv * kv, ones_ref[...])), kkv)
    yield
    kkn = each(lambda kv, n: kv / jnp.maximum(n, 1e-12), kkv, nrm)
    k2 = each(lambda kk, aa: kk * (1.0 + (aa - 1.0) * ka_ref[...]), k, a)
    bv = each(lambda kn, aa: kn * aa, kkn, a)
    c = each(lambda w: _dot_exact(tril_ref[...], w), lw)
    yield
    c_end = each(lambda cc: cc[CHUNK - 1:CHUNK, :], c)
    bonus = each(lambda rr, kk, vv: _dot_exact(rr * kk * rk_ref[...], ones_ref[...]) * vv, r, k2, v)
    yield
    at = each(lambda kn, cc, w: -kn * jnp.exp(cc - w), kkn, c, lw)
    rt = each(lambda rr, cc: rr * jnp.exp(cc), r, c)
    yield
    e_nc = each(lambda cc: jnp.exp(-cc), c)
    bt = each(lambda b_, e: b_ * e, bv, e_nc)
    kt = each(lambda kk, e: kk * e, k2, e_nc)
    yield
    e_rest = each(lambda ce, cc: jnp.exp(ce - cc), c_end, c)
    b_rest = each(lambda b_, e: b_ * e, bv, e_rest)
    k_rest = each(lambda kk, e: kk * e, k2, e_rest)
    p_end = per_block(each(jnp.exp, c_end))
    yield

    ar = each(lambda x, y: jnp.concatenate([own_head(x), own_head(y)], axis=0), per_block(at), per_block(rt))
    yield
    bk = each(lambda x, y: jnp.concatenate([tile_rows(x), tile_rows(y)], axis=0), per_block(bt), per_block(kt))
    yield
    gm = each(lambda x, y: _dot(x, y, _NT), ar, bk)
    yield
    n_ab = each(lambda m: m[0:nr, 0:nr] * strict_ref[...], gm)
    m_ak = each(lambda m: m[0:nr, nr:2 * nr] * strict_ref[...], gm)
    r_bk = each(lambda m: m[nr:2 * nr, :] * jnp.concatenate([incl_ref[...], incl_ref[...]], axis=1), gm)
    yield
    v_st = each(stack, per_block(v))
    mv = each(_dot, m_ak, v_st)
    yield
    inv = each(lambda n: eye_ref[...] + n, n_ab)
    pw = n_ab
    for _ in range(5):
        pw = each(lambda x: _dot(x, x), pw)
        yield
        inv = each(lambda i_, x: i_ + _dot(i_, x), inv, pw)
        yield
    bk_rest = each(lambda x, y: jnp.concatenate([own_head(x), own_head(y)], axis=0),
                   per_block(b_rest), per_block(k_rest))
    yield
    a_bar = each(lambda i_, x: _dot(i_, x[0:nr]), inv, ar)
    yield
    uv0 = each(lambda i_, m, vs: jnp.concatenate([_dot(i_, m), vs], axis=0), inv, mv, v_st)
    yield
    r_bar = each(lambda x, rb, ab: x[nr:2 * nr] + _dot(rb[:, 0:nr], ab), ar, r_bk, a_bar)
    yield
    y0 = each(_dot, r_bk, uv0)
    yield
    g_mat = each(lambda ab, br: _dot(ab, br[0:nr], _TN), a_bar, bk_rest)
    yield
    sc = each(lambda z, br: _dot(z, br, _TN), uv0, bk_rest)
    yield

    n_blk = len(blocks)
    s = per_block([st_ref[...]])
    for j, rw in enumerate(rows):
        it = slice(n_blk * j, n_blk * (j + 1))
        y = each(lambda rb, s_, y_: unstack(_head_norm(_dot(rb, s_, _NT) + y_, 1.0, 0.0, RWKV_GN_EPS)),
                 r_bar[it], s, y0[it])
        s = each(lambda s_, pe, gm_, c_: s_ * pe + _dot(s_, gm_) + c_, s, p_end[it], g_mat[it], sc[it])
        o_ref[0, rw, :] = (jnp.concatenate(y, axis=1) * lnw_ref[...] + lnb_ref[...] + bonus[j]) * g[j]
        if not has_vres:
            vf_ref[0, rw, :] = v_shift[j]
        yield
    prev_ref[...] = ps[-1][CHUNK - 1:CHUNK, :]
    st_ref[...] = jnp.concatenate(s, axis=1)


N_GLA_IN, N_RWKV_IN = 11, 18


def _linmix_kernel(*refs, n_chunks, has_vres):
    refs = list(refs)
    gla_in = refs[:N_GLA_IN]
    n_rw = N_RWKV_IN + (3 if has_vres else 0)
    rwkv_in = refs[N_GLA_IN:N_GLA_IN + n_rw]
    outs = refs[N_GLA_IN + n_rw:]
    if has_vres:
        (h_ref, vf_ref, mu, w0, w2, a0, a2, g2, kk, ka, rk, lnw, lnb, v0, v2, *consts) = rwkv_in
        o_gla_ref, o_rwkv_ref, gla_st, rwkv_st, prev_ref = outs
    else:
        (h_ref, mu, w0, w2, a0, a2, g2, kk, ka, rk, lnw, lnb, *consts) = rwkv_in
        v0 = v2 = None
        o_gla_ref, o_rwkv_ref, vf_ref, gla_st, rwkv_st, prev_ref = outs

    @pl.when(pl.program_id(1) == 0)
    def _():
        gla_st[...] = jnp.zeros_like(gla_st)
        rwkv_st[...] = jnp.zeros_like(rwkv_st)
        prev_ref[...] = jnp.zeros_like(prev_ref)

    _run_together(
        _gla_steps(*gla_in, o_gla_ref, gla_st, n_chunks),
        _rwkv_steps(h_ref, vf_ref, mu, w0, w2, a0, a2, g2, kk, ka, rk, lnw, lnb, v0, v2, *consts,
                    o_rwkv_ref, rwkv_st, prev_ref, n_chunks, has_vres))


def _linmix(hg, hr, v_first, gla_small, rwkv_small, ts=512):
    bsz, s, _ = hg.shape
    has_vres = v_first is not None
    r = np.arange(CHUNK)
    c = np.arange(LIN_WIDTH)
    ones = (c[:, None] // HEAD_DIM == c[None, :] // HEAD_DIM).astype(np.float32)
    tril_sub = ((r[:, None] // SUB == r[None, :] // SUB) & (r[None, :] <= r[:, None])).astype(np.float32)
    blk_sub = (r[:, None] // SUB == r[None, :] // SUB).astype(np.float32)
    hs = np.arange(N_LIN_HEADS * SUB)
    same_sub = (hs[:, None] // SUB == c[None, :] // HEAD_DIM).astype(np.float32)
    dm = ((c[None, :] % HEAD_DIM) - SUB * (r[:, None] // SUB)).astype(np.int32)
    gla_consts = [jnp.asarray(tril_sub, BF16), jnp.asarray(blk_sub, BF16), jnp.asarray(ones, BF16), jnp.asarray(dm),
                  jnp.asarray(ones), jnp.asarray(same_sub)]
    tril = (r[None, :] <= r[:, None]).astype(np.float32)
    i = np.arange(RW_BW)
    same = (i[:, None] // HEAD_DIM == i[None, :] // HEAD_DIM).astype(np.float32)
    t_in, s_in = i[:, None] % CHUNK, i[None, :] % CHUNK
    rwkv_consts = [jnp.asarray(tril, BF16), jnp.asarray(ones, BF16), jnp.asarray(same),
                   jnp.asarray(same * (s_in < t_in)), jnp.asarray(same * (s_in <= t_in)),
                   jnp.asarray(np.eye(RW_BW, dtype=np.float32))]

    def seq(width):
        return pl.BlockSpec((1, ts, width), lambda b, j: (b, j, 0))

    gla_args = [hg] + list(gla_small) + gla_consts
    rwkv_args = [hr] + ([v_first] if has_vres else []) + list(rwkv_small) + rwkv_consts
    assert len(gla_args) == N_GLA_IN and len(rwkv_args) == N_RWKV_IN + (3 if has_vres else 0)
    in_specs = ([seq(GLA_SLAB)] + [_full(a.shape) for a in gla_args[1:]]
                + [seq(RWKV_SLAB)] + ([seq(LIN_WIDTH)] if has_vres else [])
                + [_full(a.shape) for a in rwkv_args[(2 if has_vres else 1):]])
    n_out = 2 if has_vres else 3
    out_sd = jax.ShapeDtypeStruct((bsz, s, LIN_WIDTH), F32)
    return pl.pallas_call(
        functools.partial(_linmix_kernel, n_chunks=ts // CHUNK, has_vres=has_vres),
        grid=(bsz, s // ts),
        in_specs=in_specs,
        out_specs=[seq(LIN_WIDTH)] * n_out,
        out_shape=[out_sd] * n_out,
        scratch_shapes=[pltpu.VMEM((HEAD_DIM, LIN_WIDTH), F32),
                        pltpu.VMEM((HEAD_DIM, LIN_WIDTH), F32),
                        pltpu.VMEM((1, RWKV_SLAB), F32)],
        compiler_params=_params("parallel", "arbitrary"),
    )(*gla_args, *rwkv_args)


def _nsa_cmp_kernel(kc_ref, vc_ref, pk_ref, pv_ref, wk1h_ref, wk1l_ref, wk2h_ref, wk2l_ref,
                    wv1h_ref, wv1l_ref, wv2h_ref, wv2l_ref, ko_ref, vo_ref, *, n_blk):
    half = NSA_CMP_BLOCK // 2
    k_half = half * HEAD_DIM
    lo_lanes = lax.broadcasted_iota(jnp.int32, (n_blk, LANES), 1) < HEAD_DIM

    def compress(t_ref, pos_ref, w1h_ref, w1l_ref, w2h_ref, w2l_ref, o_ref):
        flat = [[], []]
        for i in range(half // 2):
            a = t_ref[0, pl.ds(2 * i, n_blk, stride=NSA_CMP_STRIDE), :]
            b = t_ref[0, pl.ds(2 * i + 1, n_blk, stride=NSA_CMP_STRIDE), :]
            flat[0].append(jnp.where(lo_lanes, a, pltpu.roll(b, HEAD_DIM, 1)))
            flat[1].append(jnp.where(lo_lanes, pltpu.roll(a, HEAD_DIM, 1), b))
        for g in range(NSA_KV_HEADS):
            gs = slice(HEAD_DIM * g, HEAD_DIM * (g + 1))
            top = jnp.concatenate([x + pos_ref[i:i + 1, :] for i, x in enumerate(flat[g])], axis=1)
            bot = jnp.concatenate([x + pos_ref[half // 2 + i:half // 2 + i + 1, :] for i, x in enumerate(flat[g])],
                                  axis=1)
            top = _dot3(top, w1h_ref[0:k_half, :], w1l_ref[0:k_half, :])
            bot = _dot3(bot, w1h_ref[k_half:, :], w1l_ref[k_half:, :])
            hid = top + pltpu.roll(bot, n_blk - 1, 0)
            act = 0.5 * hid * (1.0 + jnp.tanh(0.7978845608028654 * (hid + 0.044715 * hid * hid * hid)))
            o_ref[0, :, gs] = _dot3(act, w2h_ref[...], w2l_ref[...])

    compress(kc_ref, pk_ref, wk1h_ref, wk1l_ref, wk2h_ref, wk2l_ref, ko_ref)
    compress(vc_ref, pv_ref, wv1h_ref, wv1l_ref, wv2h_ref, wv2l_ref, vo_ref)


def _nsa_cmp(kv, pos_k, pos_v, wk1, wk2, wv1, wv2):
    bsz, s, _ = kv.shape
    n_blk = s // NSA_CMP_STRIDE
    pos_k = pos_k.reshape(NSA_CMP_BLOCK // 2, 2 * HEAD_DIM)
    pos_v = pos_v.reshape(NSA_CMP_BLOCK // 2, 2 * HEAD_DIM)
    small = [pos_k, pos_v, *_split_bf16(wk1), *_split_bf16(wk2), *_split_bf16(wv1), *_split_bf16(wv2)]
    out_spec = pl.BlockSpec((1, n_blk, LANES), lambda b: (b, 0, 0))
    out_sd = jax.ShapeDtypeStruct((bsz, n_blk, LANES), F32)
    return pl.pallas_call(
        functools.partial(_nsa_cmp_kernel, n_blk=n_blk),
        grid=(bsz,),
        in_specs=[pl.BlockSpec((1, s, LANES), lambda b: (b, 0, 0)),
                  pl.BlockSpec((1, s, LANES), lambda b: (b, 0, 1))] + [_full(a.shape) for a in small],
        out_specs=[out_spec, out_spec],
        out_shape=[out_sd, out_sd],
        compiler_params=_params("parallel"),
    )(kv, kv, *small)


def _nsa_attn_kernel(q_ref, gt_ref, kc_ref, vc_ref, ks_ref, kw_ref, vt_ref, ovt_ref,
                     o_ref, acc_ref, s_ref, cm_ref, m_ref, *, n_cmp):
    qi = pl.program_id(1)
    cols = NSA_HPG * QBLK
    n_sel_blk = ovt_ref.shape[0]
    key_off = lax.broadcasted_iota(jnp.int32, (QBLK, cols), 0)
    t_pos = qi * QBLK + (lax.broadcasted_iota(jnp.int32, (QBLK, cols), 1) & (QBLK - 1))
    gt_t = gt_ref[0].T
    zeros64 = jnp.zeros((QBLK, HEAD_DIM), F32)
    m0 = jnp.full((1, cols), MASK_NEG, F32)

    groups = list(range(NSA_KV_HEADS))

    each, run_together = _each, _run_together

    def stacked_q(g):
        pieces = []
        for hp in range(NSA_HPG):
            qh = q_ref[0, :, (g * NSA_HPG + hp) * HEAD_DIM:(g * NSA_HPG + hp + 1) * HEAD_DIM]
            pieces.append(jnp.concatenate([qh, zeros64] if g == 0 else [zeros64, qh], axis=1))
        return jnp.concatenate(pieces, axis=0)

    qs = each(stacked_q, groups)
    qs_b = each(lambda q: (q * LOG2E).astype(BF16), qs)

    def select_steps():
        valid_c = (key_off * NSA_CMP_STRIDE + (NSA_CMP_BLOCK - 1) <= t_pos) & (key_off < n_cmp)
        s_c = each(lambda q: jnp.where(valid_c, _dot3(kc_ref[0], *_split_bf16(q), _NT), MASK_NEG), qs)
        yield
        e_c = each(lambda x: jnp.exp(x - jnp.max(x, axis=0, keepdims=True)), s_c)
        yield
        p_c = each(lambda e: jnp.where(valid_c, e / jnp.sum(e, axis=0, keepdims=True), 0.0), e_c)
        yield
        o_cmp = each(lambda p, g: _dot(vc_ref[0], p, _TN)[HEAD_DIM * g:HEAD_DIM * (g + 1), :], p_c, groups)
        yield
        p_sum = each(lambda p: p[:, 0:QBLK] + p[:, QBLK:2 * QBLK] + p[:, 2 * QBLK:3 * QBLK] + p[:, 3 * QBLK:], p_c)
        imp = each(lambda p: _dot_exact(ovt_ref[...], p), p_sum)
        yield
        j = lax.broadcasted_iota(jnp.int32, (n_sel_blk, QBLK), 0)
        tq = qi * QBLK + lax.broadcasted_iota(jnp.int32, (n_sel_blk, QBLK), 1)
        cur = tq // NSA_SEL_BLOCK
        forced = (j == 0) | (j == cur) | (j == cur - 1)
        score = each(lambda x: jnp.where(forced, 1e9, jnp.where(j > cur, -1e9, x)), imp)
        rank = [jnp.zeros((n_sel_blk, QBLK), F32) for _ in groups]
        for i in range(n_sel_blk):
            rank = each(lambda rk, sc: rk + jnp.where(
                (sc[i:i + 1, :] > sc) | ((sc[i:i + 1, :] == sc) & (j > i)), 1.0, 0.0), rank, score)
            if i % 4 == 3:
                yield
        return o_cmp, each(lambda rk: jnp.where(rk < NSA_N_SELECT, 1.0, 0.0), rank)

    def key_scores(k_ref, q, kb0, n_kb, mask):
        k0 = pl.multiple_of(kb0 * QBLK, QBLK)
        s = lax.dot_general(k_ref[0, pl.ds(k0, n_kb * QBLK), :], q, _NT, preferred_element_type=F32)
        if mask is not None:
            key_pos = k0 + lax.broadcasted_iota(jnp.int32, (n_kb * QBLK, cols), 0)
            t_all = jnp.concatenate([t_pos] * n_kb, axis=0)
            valid = key_pos <= t_all
            if mask == "window":
                valid = valid & (key_pos > t_all - NSA_WINDOW)
            s = jnp.where(valid, s, MASK_NEG)
        return s

    def softmax_steps(slabs, kb0, n_kb, block_scores, col_max, m_old):
        m_new = each(jnp.maximum, m_old, col_max)
        pv = [jnp.zeros((VT_ROWS, cols), F32) for _ in slabs]
        halves = (slice(0, cols // 2), slice(cols // 2, cols))

        def block_pv(vt_blk, x, mn):
            return jnp.concatenate(
                [jnp.dot(vt_blk, jnp.exp2(x[:, h] - mn[:, h]).astype(BF16), preferred_element_type=F32)
                 for h in halves], axis=1)

        for i in range(n_kb):
            pv = [acc + block_pv(vt_ref[0, kb0 + i, slab * VT_ROWS:(slab + 1) * VT_ROWS, :], rd(i), mn)
                  for acc, rd, mn, slab in zip(pv, block_scores, m_new, slabs)]
            yield
        for slab, m, mn, x in zip(slabs, m_old, m_new, pv):
            acc_ref[slab] = acc_ref[slab] * jnp.exp2(m - mn) + x
        return m_new

    def window_steps():
        n_win = NSA_WINDOW // QBLK + 1
        kb_win = jnp.maximum(qi - (n_win - 1), 0)
        s = []
        for g in groups:
            s.append(key_scores(kw_ref, qs_b[g], kb_win, n_win, "window"))
            yield
        col_max = each(lambda x: jnp.max(x, axis=0, keepdims=True), s)
        yield
        blocks = [lambda i, x=x: x[QBLK * i:QBLK * (i + 1), :] for x in s]
        yield from softmax_steps([NSA_KV_HEADS + g for g in groups], kb_win, n_win, blocks, col_max, [m0] * len(s))

    def normalised(slab):
        return acc_ref[slab, 0:HEAD_DIM, :] / acc_ref[slab, HEAD_DIM:HEAD_DIM + 1, :]

    acc_ref[...] = jnp.zeros_like(acc_ref)
    for g in groups:
        m_ref[g] = m0

    (o_cmp, sel), _ = run_together(select_steps(), window_steps())

    def with_selection(q, sl):
        off = jnp.where(sl > 0.5, 0.0, MASK_NEG).T.astype(BF16)
        pad = jnp.zeros((cols, LANES - n_sel_blk), BF16)
        return jnp.concatenate([q, jnp.concatenate([off] * NSA_HPG, axis=0), pad], axis=1)

    qs_sel = each(with_selection, qs_b, sel)

    def score_steps(i, mask):
        s = []
        for g in groups:
            s.append(key_scores(ks_ref, qs_sel[g], i * SLC_KB, SLC_KB, mask))
            yield
        return s, each(lambda x: jnp.max(x, axis=0, keepdims=True), s)

    def park(scores_and_max):
        for g, (x, cm) in enumerate(zip(*scores_and_max)):
            s_ref[g] = x
            cm_ref[g] = cm

    def consume_steps(i):
        blocks = [lambda b, g=g: s_ref[g, QBLK * b:QBLK * (b + 1), :] for g in groups]
        m_new = yield from softmax_steps(groups, i * SLC_KB, SLC_KB, blocks, [cm_ref[g] for g in groups],
                                         [m_ref[g] for g in groups])
        for g in groups:
            m_ref[g] = m_new[g]

    n_full = qi // SLC_KB

    @pl.when(n_full >= 1)
    def _():
        park(run_together(score_steps(0, None))[0])

    def pipelined(i, carry):
        nxt, _ = run_together(score_steps(i + 1, None), consume_steps(i))
        park(nxt)
        return carry

    lax.fori_loop(0, n_full - 1, pipelined, 0)

    @pl.when(n_full >= 1)
    def _():
        nxt, _ = run_together(score_steps(n_full, "causal"), consume_steps(n_full - 1))
        park(nxt)

    @pl.when(n_full == 0)
    def _():
        park(run_together(score_steps(0, "causal"))[0])

    run_together(consume_steps(n_full))

    o_groups = []
    for g in groups:
        def gate(branch):
            return jnp.concatenate(
                [gt_t[(g * NSA_HPG + hp) * 3 + branch:(g * NSA_HPG + hp) * 3 + branch + 1, :]
                 for hp in range(NSA_HPG)], axis=1)

        o_groups.append(gate(0) * o_cmp[g] + gate(1) * normalised(g) + gate(2) * normalised(NSA_KV_HEADS + g))

    o_all = jnp.concatenate(o_groups, axis=0)
    for hp in range(NSA_HPG):
        tile = o_all[:, QBLK * hp:QBLK * (hp + 1)].T
        for g in range(NSA_KV_HEADS):
            c0 = (g * NSA_HPG + hp) * HEAD_DIM
            o_ref[0, :, c0:c0 + HEAD_DIM] = tile[:, HEAD_DIM * g:HEAD_DIM * (g + 1)]


def _nsa_attn(qr, kk, vt, gt, kcmp, vcmp):
    bsz, s, _ = qr.shape
    n_blk = s // NSA_CMP_STRIDE
    n_cmp = n_blk - NSA_CMP_BLOCK // NSA_CMP_STRIDE + 1
    n_sel = s // NSA_SEL_BLOCK
    assert n_blk == QBLK, "compressed-score tiles are laid out one stride block per sublane row"
    c0 = np.arange(n_blk) * NSA_CMP_STRIDE
    s0 = np.arange(n_sel) * NSA_SEL_BLOCK
    lo = np.maximum(c0[:, None], s0[None, :])
    hi = np.minimum(c0[:, None] + NSA_CMP_BLOCK, s0[None, :] + NSA_SEL_BLOCK)
    overlap = (np.maximum(hi - lo, 0) / NSA_CMP_STRIDE).astype(np.float32)
    overlap[n_cmp:] = 0.0
    consts = [jnp.asarray(overlap.T, BF16)]

    cmp_spec = pl.BlockSpec((1, n_blk, LANES), lambda b, i: (b, 0, 0))
    return pl.pallas_call(
        functools.partial(_nsa_attn_kernel, n_cmp=n_cmp),
        grid=(bsz, s // QBLK),
        in_specs=[pl.BlockSpec((1, QBLK, NSA_WIDTH), lambda b, i: (b, i, 0)),
                  pl.BlockSpec((1, QBLK, LANES), lambda b, i: (b, i, 0)),
                  cmp_spec, cmp_spec,
                  pl.BlockSpec((1, s, 2 * LANES), lambda b, i: (b, 0, 0)),
                  pl.BlockSpec((1, s, LANES), lambda b, i: (b, 0, 2)),
                  pl.BlockSpec((1, s // QBLK, 4 * VT_ROWS, QBLK), lambda b, i: (b, 0, 0, 0))]
        + [_full(a.shape) for a in consts],
        out_specs=pl.BlockSpec((1, QBLK, NSA_WIDTH), lambda b, i: (b, i, 0)),
        out_shape=jax.ShapeDtypeStruct((bsz, s, NSA_WIDTH), F32),
        scratch_shapes=[pltpu.VMEM((2 * NSA_KV_HEADS, VT_ROWS, NSA_HPG * QBLK), F32),
                        pltpu.VMEM((NSA_KV_HEADS, SLC_KB * QBLK, NSA_HPG * QBLK), F32),
                        pltpu.VMEM((NSA_KV_HEADS, 1, NSA_HPG * QBLK), F32),
                        pltpu.VMEM((NSA_KV_HEADS, 1, NSA_HPG * QBLK), F32)],
        compiler_params=_params("parallel", "arbitrary"),
    )(qr, gt, kcmp, vcmp, kk, kk, vt, *consts)


def _mix_ffn_kernel(x_ref, og_ref, or_ref, on_ref, wo_ref, l1w_ref, l1b_ref, wg_ref, wu_ref, wd_ref,
                    l2w_ref, l2b_ref, o_ref, *, alpha):
    mix = (_dot(og_ref[...], wo_ref[0:LIN_WIDTH, :])
           + _dot(or_ref[...], wo_ref[LIN_WIDTH:2 * LIN_WIDTH, :])
           + _dot(on_ref[...], wo_ref[2 * LIN_WIDTH:, :]))
    x1 = _layer_norm(alpha * x_ref[...] + mix, l1w_ref[...], l1b_ref[...])
    xb = x1.astype(BF16)
    ffn = jnp.zeros_like(x1)
    for c0, c1 in FFN_SPLITS:
        gate = jnp.dot(xb, wg_ref[:, c0:c1], preferred_element_type=F32)
        up = jnp.dot(xb, wu_ref[:, c0:c1], preferred_element_type=F32)
        ffn = ffn + _dot(gate * _sigmoid(gate) * up, wd_ref[c0:c1, :])
    o_ref[...] = _layer_norm(alpha * x1 + ffn, l2w_ref[...], l2b_ref[...])


def _mix_ffn(x2, og, orw, on, wo, l1w, l1b, wg, wu, wd, l2w, l2b, alpha, tm=512):
    t = x2.shape[0]

    def rows(width):
        return pl.BlockSpec((tm, width), lambda i: (i, 0))

    def resident(a):
        return pl.BlockSpec(a.shape, lambda i: (0,) * a.ndim, pipeline_mode=pl.Buffered(1))

    consts = [wo, l1w, l1b, wg, wu, wd, l2w, l2b]
    return pl.pallas_call(
        functools.partial(_mix_ffn_kernel, alpha=alpha),
        grid=(t // tm,),
        in_specs=[rows(D_MODEL), rows(LIN_WIDTH), rows(LIN_WIDTH), rows(NSA_WIDTH)] + [resident(a) for a in consts],
        out_specs=rows(D_MODEL),
        out_shape=jax.ShapeDtypeStruct((t, D_MODEL), F32),
        compiler_params=_params("parallel"),
    )(x2, og, orw, on, *consts)


def _pad_cols(w, width):
    return jnp.pad(w, ((0, 0), (0, width - w.shape[1])))


def _pad_rows(w, height):
    return jnp.pad(w, ((0, height - w.shape[0]), (0, 0)))


def _rwkv_slab_cols(w, w_vres):
    o = 3 * LIN_WIDTH
    parts = [w[:, :o],
             _pad_cols(w[:, o:o + RWKV_DECAY_RANK], LANES),
             _pad_cols(w[:, o + RWKV_DECAY_RANK:o + RWKV_DECAY_RANK + RWKV_ICLR_RANK], LANES),
             _pad_cols(w[:, o + RWKV_DECAY_RANK + RWKV_ICLR_RANK:], 2 * LANES),
             _pad_cols(w_vres, LANES)]
    return jnp.concatenate(parts, axis=1)


def _rope_tables(s):
    half = HEAD_DIM // 2
    inv = ROPE_THETA ** (-jnp.arange(half, dtype=F32) / half)
    ang = jnp.arange(s, dtype=jnp.int32).astype(F32)[:, None] * inv
    cos, sin = jnp.cos(ang), jnp.sin(ang)
    cos2 = jnp.tile(jnp.concatenate([cos, cos], axis=1), (1, LANES // HEAD_DIM))
    sin2 = jnp.tile(jnp.concatenate([-sin, sin], axis=1), (1, LANES // HEAD_DIM))
    return cos2, sin2


def kernel(x, w_in, w_in_vres, gla_w_a2, gla_b_a, gla_ln_w, gla_ln_b, rwkv_mu, rwkv_mu_vres, rwkv_w0, rwkv_w2, rwkv_a0, rwkv_a2, rwkv_v0, rwkv_v2, rwkv_g2, rwkv_k_k, rwkv_k_a, rwkv_r_k, rwkv_ln_w, rwkv_ln_b, nsa_pos_k, nsa_pos_v, nsa_wk1, nsa_wk2, nsa_wv1, nsa_wv2, w_out, ln1_w, ln1_b, ffn_w_gate, ffn_w_up, ffn_w_down, ln2_w, ln2_b):
    bsz, s, d = x.shape
    depth = w_in.shape[0]
    alpha = float((2 * depth) ** 0.25)
    gla_cols = 4 * LIN_WIDTH + GLA_GATE_RANK
    rwkv_cols = 3 * LIN_WIDTH + RWKV_DECAY_RANK + RWKV_ICLR_RANK + RWKV_GATE_RANK
    cos2, sin2 = _rope_tables(s)
    row = lambda a: a.reshape(1, -1)

    x2 = x.reshape(bsz * s, d)
    v_first = None
    for l in range(depth):
        w = w_in[l]
        w_r = w[:, gla_cols:gla_cols + rwkv_cols]
        if l == 0:
            w_vres = jnp.zeros((d, RWKV_VRES_RANK), F32)
            mu_vres = jnp.zeros((1, RWKV_VRES_RANK), F32)
        else:
            w_vres = w_in_vres[l - 1]
            mu_vres = row(rwkv_mu_vres[l - 1])
        wg = _pad_cols(w[:, :gla_cols], GLA_SLAB).astype(BF16)
        wr = _rwkv_slab_cols(w_r, w_vres).astype(BF16)
        wn = _pad_cols(w[:, gla_cols + rwkv_cols:], NSA_SLAB).astype(BF16)
        hg, hr, qr, kvc, kk, vt, gt = _proj(x2, wg, wr, wn, cos2, sin2)
        hg = hg.reshape(bsz, s, GLA_SLAB)
        hr = hr.reshape(bsz, s, RWKV_SLAB)
        qr = qr.reshape(bsz, s, NSA_WIDTH)
        kvc = kvc.reshape(bsz, s, 2 * LANES)
        kk = kk.reshape(bsz, s, 3 * LANES)
        vt = vt.reshape(bsz, s // QBLK, 4 * VT_ROWS, QBLK)
        gt = gt.reshape(bsz, s, LANES)

        gla_small = [_hi_lo(_pad_rows(gla_w_a2[l], LANES)), row(gla_b_a[l]), row(gla_ln_w[l]), row(gla_ln_b[l])]
        mu = _rwkv_slab_cols(row(rwkv_mu[l]), mu_vres)
        rwkv_small = [mu, row(rwkv_w0[l]), _hi_lo(_pad_rows(rwkv_w2[l], LANES)), row(rwkv_a0[l]),
                      _hi_lo(_pad_rows(rwkv_a2[l], LANES)), _hi_lo(_pad_rows(rwkv_g2[l], 2 * LANES)),
                      row(rwkv_k_k[l]), row(rwkv_k_a[l]), row(rwkv_r_k[l]), row(rwkv_ln_w[l]), row(rwkv_ln_b[l])]
        if l == 0:
            o_gla, o_rwkv, v_first = _linmix(hg, hr, None, gla_small, rwkv_small)
        else:
            rwkv_small += [row(rwkv_v0[l - 1]), _hi_lo(_pad_rows(rwkv_v2[l - 1], LANES))]
            o_gla, o_rwkv = _linmix(hg, hr, v_first, gla_small, rwkv_small)

        kcmp, vcmp = _nsa_cmp(kvc, nsa_pos_k[l], nsa_pos_v[l], nsa_wk1[l], nsa_wk2[l], nsa_wv1[l], nsa_wv2[l])
        o_nsa = _nsa_attn(qr, kk, vt, gt, kcmp, vcmp)

        x2 = _mix_ffn(x2, o_gla.reshape(bsz * s, LIN_WIDTH), o_rwkv.reshape(bsz * s, LIN_WIDTH),
                      o_nsa.reshape(bsz * s, NSA_WIDTH), w_out[l].astype(BF16), row(ln1_w[l]), row(ln1_b[l]),
                      ffn_w_gate[l].astype(BF16), ffn_w_up[l].astype(BF16), ffn_w_down[l].astype(BF16),
                      row(ln2_w[l]), row(ln2_b[l]), alpha)
    return x2.reshape(bsz, s, d)
```
